```python
import math
import jax
import jax.numpy as jnp
from jax import lax
import numpy as np

D_MODEL = 1024
BATCH = 32
SEQ = 2048
DEPTH = 4

GRID_W = 64
CTX_LEN = 256
N_MIXERS = 4
HEAD_DIM = 64
ATTN_SCALE = HEAD_DIM ** -0.5
ROPE_THETA = 10000.0
NORM_EPS = 1e-6
Q_BLOCK = 128
N_MOD = 6
DA_HEADS = D_MODEL // (2 * HEAD_DIM)
NA_HEADS = D_MODEL // HEAD_DIM
NA_ROWS_MAX = 8
NA_COLS = 16
LRU_WIDTH = D_MODEL
LRU_BLOCK = 256
LRU_BLOCKS = LRU_WIDTH // LRU_BLOCK
LRU_CONV = 4
LRU_C = 8.0
GQA_Q_HEADS = D_MODEL // HEAD_DIM
GQA_KV_HEADS = GQA_Q_HEADS // 4
D_FF = 4 * D_MODEL

kernel_name = 'hybrid_interleaved_diffusion_block'


def n_uses(kind):
    return len(range(kind, DEPTH, N_MIXERS))


def rms_norm(x, g):
    xf = x.astype(jnp.float32)
    y = xf * lax.rsqrt(jnp.mean(xf * xf, axis=-1, keepdims=True) + NORM_EPS)
    return (y * g.astype(jnp.float32)).astype(x.dtype)


def modulate(h, shift, scale):
    return h * (1 + scale) + shift


def squared_relu_mlp(h, w1, w2):
    return jnp.square(jax.nn.relu(h @ w1)) @ w2


def axial_rope_table(n_tok):
    t = jnp.arange(n_tok)
    row = (t // GRID_W).astype(jnp.float32)
    col = (t % GRID_W).astype(jnp.float32)
    n_freq = HEAD_DIM // 4
    inv = ROPE_THETA ** (-jnp.arange(n_freq, dtype=jnp.float32) / n_freq)
    ang = jnp.concatenate([row[:, None] * inv, col[:, None] * inv], axis=-1)
    return jnp.cos(ang), jnp.sin(ang)


def apply_axial_rope(x, cos, sin):
    T = x.shape[1]
    half = x.shape[-1] // 2
    shp = (1, T) + (1,) * (x.ndim - 3) + (half,)
    cs = cos.reshape(shp)
    sn = sin.reshape(shp)
    xf = x.astype(jnp.float32).reshape(x.shape[:-1] + (half, 2))
    x0 = xf[..., 0]
    x1 = xf[..., 1]
    out = jnp.stack([x0 * cs - x1 * sn, x0 * sn + x1 * cs], axis=-1)
    return out.reshape(x.shape).astype(x.dtype)


def sweep_query_blocks(fn, q, k, v):
    B, S = q.shape[:2]
    nb = S // Q_BLOCK
    qb = q.reshape((B, nb, Q_BLOCK) + q.shape[2:]).swapaxes(0, 1)
    ob = lax.map(lambda qq: fn(qq, k, v), qb)
    return ob.swapaxes(0, 1).reshape((B, S) + ob.shape[3:])


def gqa_attend(q, k, v):
    B, Tq, Hq, dh = q.shape
    Hkv = k.shape[2]
    qg = q.reshape(B, Tq, Hkv, Hq // Hkv, dh)
    s = jnp.einsum('bqhgd,bkhd->bhgqk', qg, k).astype(jnp.float32) * ATTN_SCALE
    p = jax.nn.softmax(s, axis=-1).astype(v.dtype)
    o = jnp.einsum('bhgqk,bkhd->bqhgd', p, v)
    return o.reshape(B, Tq, Hq, dh)


def diff_attend(q, k, v, lam):
    s = jnp.einsum('bqhmd,bkhmd->bhmqk', q, k).astype(jnp.float32) * ATTN_SCALE
    p = jax.nn.softmax(s, axis=-1)
    w = (p[:, :, 0] - lam * p[:, :, 1]).astype(v.dtype)
    return jnp.einsum('bhqk,bkhe->bqhe', w, v)


def diff_attention_mixer(h_lat, h_ctx, p, layer_idx, cos, sin, need_ctx):
    w_qkv, q_gain, k_gain, lq1, lk1, lq2, lk2, subln_g, w_o = p
    lam_init = 0.8 - 0.6 * math.exp(-0.3 * layer_idx)
    lam = (jnp.exp(jnp.sum(lq1.astype(jnp.float32) * lk1.astype(jnp.float32)))
           - jnp.exp(jnp.sum(lq2.astype(jnp.float32) * lk2.astype(jnp.float32))) + lam_init)

    def proj_q(h):
        B, T, _ = h.shape
        q = (h @ w_qkv[:, :D_MODEL]).reshape(B, T, DA_HEADS, 2, HEAD_DIM)
        return rms_norm(q, q_gain)

    def proj_kv(h):
        B, T, _ = h.shape
        k, v = jnp.split(h @ w_qkv[:, D_MODEL:], 2, axis=-1)
        k = rms_norm(k.reshape(B, T, DA_HEADS, 2, HEAD_DIM), k_gain)
        return k, v.reshape(B, T, DA_HEADS, 2 * HEAD_DIM)

    def finish(o):
        B, T = o.shape[:2]
        o = rms_norm(o, subln_g) * (1.0 - lam_init)
        return o.reshape(B, T, D_MODEL) @ w_o

    q_l = apply_axial_rope(proj_q(h_lat), cos, sin)
    k_l, v_l = proj_kv(h_lat)
    k_l = apply_axial_rope(k_l, cos, sin)
    k_c, v_c = proj_kv(h_ctx)
    k_all = jnp.concatenate([k_c, k_l], axis=1)
    v_all = jnp.concatenate([v_c, v_l], axis=1)
    o_l = sweep_query_blocks(lambda qb, kb, vb: diff_attend(qb, kb, vb, lam), q_l, k_all, v_all)
    y_l = finish(o_l)
    y_c = finish(diff_attend(proj_q(h_ctx), k_c, v_c, lam)) if need_ctx else None
    return y_l, y_c


def heads_q(h, w_q, n_heads, gain):
    B, T, _ = h.shape
    return rms_norm((h @ w_q).reshape(B, T, n_heads, HEAD_DIM), gain)


def heads_kv(h, w_kv, n_heads, gain):
    B, T, _ = h.shape
    k, v = jnp.split(h @ w_kv, 2, axis=-1)
    k = rms_norm(k.reshape(B, T, n_heads, HEAD_DIM), gain)
    return k, v.reshape(B, T, n_heads, HEAD_DIM)


def neighbourhood_mixer(h_lat, h_ctx, p, need_ctx):
    w_qkv, q_gain, k_gain, rpb, w_o = p
    B, S, _ = h_lat.shape
    rows = S // GRID_W
    kr = min(NA_ROWS_MAX, rows)
    n_loc = kr * GRID_W
    w_q = w_qkv[:, :D_MODEL]
    w_kv = w_qkv[:, D_MODEL:]
    q_l = heads_q(h_lat, w_q, NA_HEADS, q_gain)
    k_l, v_l = heads_kv(h_lat, w_kv, NA_HEADS, k_gain)
    k_c, v_c = heads_kv(h_ctx, w_kv, NA_HEADS, k_gain)
    k_grid = k_l.reshape(B, rows, GRID_W, NA_HEADS, HEAD_DIM)
    v_grid = v_l.reshape(B, rows, GRID_W, NA_HEADS, HEAD_DIM)
    q_rows = q_l.reshape(B, rows, GRID_W, NA_HEADS, HEAD_DIM).swapaxes(0, 1)
    r_idx = jnp.arange(rows)
    row_start = jnp.clip(r_idx - kr // 2, 0, rows - kr)
    col = jnp.arange(GRID_W)
    col_start = jnp.clip(col - NA_COLS // 2, 0, GRID_W - NA_COLS)
    col_in = (col[None, :] >= col_start[:, None]) & (col[None, :] < col_start[:, None] + NA_COLS)
    col_bias_idx = jnp.clip(col[None, :] - col[:, None], -(NA_COLS - 1), NA_COLS - 1) + NA_COLS - 1

    def row_block(args):
        q_row, r, r0 = args
        k_band = lax.dynamic_slice_in_dim(k_grid, r0, kr, axis=1)
        v_band = lax.dynamic_slice_in_dim(v_grid, r0, kr, axis=1)
        s_loc = jnp.einsum('bqhd,brkhd->bhqrk', q_row, k_band).astype(jnp.float32) * ATTN_SCALE
        row_off = r0 + jnp.arange(kr) - r + NA_ROWS_MAX - 1
        bias = rpb[:, row_off][:, :, col_bias_idx].transpose(0, 2, 1, 3)
        s_loc = jnp.where(col_in[None, None, :, None, :], s_loc + bias[None].astype(jnp.float32), -jnp.inf)
        s_ctx = jnp.einsum('bqhd,bkhd->bhqk', q_row, k_c).astype(jnp.float32) * ATTN_SCALE
        s_all = jnp.concatenate([s_loc.reshape(B, NA_HEADS, GRID_W, n_loc), s_ctx], axis=-1)
        prob = jax.nn.softmax(s_all, axis=-1).astype(v_band.dtype)
        p_loc = prob[..., :n_loc].reshape(B, NA_HEADS, GRID_W, kr, GRID_W)
        o = (jnp.einsum('bhqrk,brkhd->bqhd', p_loc, v_band)
             + jnp.einsum('bhqk,bkhd->bqhd', prob[..., n_loc:], v_c))
        return o.reshape(B, GRID_W, NA_HEADS * HEAD_DIM)

    o = lax.map(row_block, (q_rows, r_idx, row_start))
    y_l = o.swapaxes(0, 1).reshape(B, S, D_MODEL) @ w_o
    y_c = None
    if need_ctx:
        q_c = heads_q(h_ctx, w_q, NA_HEADS, q_gain)
        y_c = gqa_attend(q_c, k_c, v_c).reshape(B, h_ctx.shape[1], D_MODEL) @ w_o
    return y_l, y_c


def centred_depthwise_conv(u, w, b):
    T = u.shape[1]
    left = LRU_CONV // 2
    right = LRU_CONV - 1 - left
    up = jnp.pad(u, ((0, 0), (left, right), (0, 0)))
    out = b + up[:, 0:T] * w[0]
    for j in range(1, LRU_CONV):
        out = out + up[:, j:j + T] * w[j]
    return out


def rglru_coeffs(u, w_a, b_a, w_x, b_x, lam):
    B, T, _ = u.shape
    ub = u.reshape(B, T, LRU_BLOCKS, LRU_BLOCK)
    r = jax.nn.sigmoid((jnp.einsum('btnd,nde->btne', ub, w_a).reshape(B, T, LRU_WIDTH) + b_a).astype(jnp.float32))
    i = jax.nn.sigmoid((jnp.einsum('btnd,nde->btne', ub, w_x).reshape(B, T, LRU_WIDTH) + b_x).astype(jnp.float32))
    log_a = -LRU_C * r * jax.nn.softplus(-lam.astype(jnp.float32))
    a = jnp.exp(log_a)
    b = jnp.sqrt(-jnp.expm1(2.0 * log_a)) * (i * u.astype(jnp.float32))
    return a, b


def scan_combine(left, right):
    a1, b1 = left
    a2, b2 = right
    return a1 * a2, a2 * b1 + b2


def linear_scan(a, b, h0, reverse):
    if reverse:
        b = b.at[:, -1].add(a[:, -1] * h0)
    else:
        b = b.at[:, 0].add(a[:, 0] * h0)
    _, h = lax.associative_scan(scan_combine, (a, b), reverse=reverse, axis=1)
    return h


def rglru_mixer(h_lat, h_ctx, p, need_ctx):
    (w_in, conv_w, conv_b, f_wa, f_ba, f_wx, f_bx, f_lam,
     b_wa, b_ba, b_wx, b_bx, b_lam, w_o) = p
    w_gate = w_in[:, :LRU_WIDTH]
    w_rec = w_in[:, LRU_WIDTH:]
    u_l = centred_depthwise_conv(h_lat @ w_rec, conv_w, conv_b)
    u_c = centred_depthwise_conv(h_ctx @ w_rec, conv_w, conv_b)
    h0 = jnp.zeros((h_ctx.shape[0], LRU_WIDTH), jnp.float32)
    a, b = rglru_coeffs(u_c, f_wa, f_ba, f_wx, f_bx, f_lam)
    hc_f = linear_scan(a, b, h0, False)
    a, b = rglru_coeffs(u_c, b_wa, b_ba, b_wx, b_bx, b_lam)
    hc_b = linear_scan(a, b, h0, True)
    a, b = rglru_coeffs(u_l, f_wa, f_ba, f_wx, f_bx, f_lam)
    hl_f = linear_scan(a, b, hc_f[:, -1], False)
    a, b = rglru_coeffs(u_l, b_wa, b_ba, b_wx, b_bx, b_lam)
    hl_b = linear_scan(a, b, hc_b[:, 0], True)

    def out(hf, hb, h):
        gate = jax.nn.gelu(h @ w_gate)
        return ((hf + hb).astype(gate.dtype) * gate) @ w_o

    y_l = out(hl_f, hl_b, h_lat)
    y_c = out(hc_f, hc_b, h_ctx) if need_ctx else None
    return y_l, y_c


def gqa_mixer(h_lat, h_ctx, p, cos, sin, need_ctx):
    w_qkv, q_gain, k_gain, w_o = p
    B, S, _ = h_lat.shape
    nq = GQA_Q_HEADS * HEAD_DIM
    w_q = w_qkv[:, :nq]
    w_kv = w_qkv[:, nq:]
    q_l = apply_axial_rope(heads_q(h_lat, w_q, GQA_Q_HEADS, q_gain), cos, sin)
    k_l, v_l = heads_kv(h_lat, w_kv, GQA_KV_HEADS, k_gain)
    k_l = apply_axial_rope(k_l, cos, sin)
    k_c, v_c = heads_kv(h_ctx, w_kv, GQA_KV_HEADS, k_gain)
    k_all = jnp.concatenate([k_c, k_l], axis=1)
    v_all = jnp.concatenate([v_c, v_l], axis=1)
    o_l = sweep_query_blocks(gqa_attend, q_l, k_all, v_all)
    y_l = o_l.reshape(B, S, D_MODEL) @ w_o
    y_c = None
    if need_ctx:
        q_c = heads_q(h_ctx, w_q, GQA_Q_HEADS, q_gain)
        y_c = gqa_attend(q_c, k_c, v_c).reshape(B, h_ctx.shape[1], D_MODEL) @ w_o
    return y_l, y_c


def setup_inputs(seed: int = 0) -> dict:
    key = jax.random.key(seed)
    ks = iter(jax.random.split(key, 64))
    D = D_MODEL

    def normal(shape, scale):
        return jax.random.normal(next(ks), shape, jnp.float32) * scale

    def gain(shape):
        return 1.0 + normal(shape, 0.02)

    def lru_lambda(n):
        a0 = jax.random.uniform(next(ks), (n, LRU_WIDTH), jnp.float32, minval=0.9, maxval=0.999)
        s = a0 ** (1.0 / LRU_C)
        return jnp.log(s) - jnp.log1p(-s)

    na, nb, nc, nd = n_uses(0), n_uses(1), n_uses(2), n_uses(3)
    return {
        'x': normal((BATCH, SEQ, D), 1.0),
        'c': normal((BATCH, D), 1.0),
        'ctx': normal((BATCH, CTX_LEN, D), 1.0),
        'c_ctx': normal((D,), 1.0),
        'norm1_g': gain((DEPTH, D)),
        'norm2_g': gain((DEPTH, D)),
        'w_mod': normal((DEPTH, D, N_MOD * D), 0.5 * D ** -0.5),
        'b_mod': normal((DEPTH, N_MOD * D), 0.02),
        'w_mlp1': normal((DEPTH, D, D_FF), D ** -0.5),
        'w_mlp2': normal((DEPTH, D_FF, D), D_FF ** -0.5),
        'a_w_qkv': normal((na, D, 3 * D), D ** -0.5),
        'a_q_norm_g': gain((na, HEAD_DIM)),
        'a_k_norm_g': gain((na, HEAD_DIM)),
        'a_lambda_q1': normal((na, HEAD_DIM), 0.1),
        'a_lambda_k1': normal((na, HEAD_DIM), 0.1),
        'a_lambda_q2': normal((na, HEAD_DIM), 0.1),
        'a_lambda_k2': normal((na, HEAD_DIM), 0.1),
        'a_subln_g': gain((na, 2 * HEAD_DIM)),
        'a_w_o': normal((na, D, D), D ** -0.5),
        'b_w_qkv': normal((nb, D, 3 * D), D ** -0.5),
        'b_q_norm_g': gain((nb, HEAD_DIM)),
        'b_k_norm_g': gain((nb, HEAD_DIM)),
        'b_rpb': normal((nb, NA_HEADS, 2 * NA_ROWS_MAX - 1, 2 * NA_COLS - 1), 0.02),
        'b_w_o': normal((nb, D, D), D ** -0.5),
        'c_w_in': normal((nc, D, 2 * LRU_WIDTH), D ** -0.5),
        'c_conv_w': normal((nc, LRU_CONV, LRU_WIDTH), LRU_CONV ** -0.5),
        'c_conv_b': normal((nc, LRU_WIDTH), 0.01),
        'c_fwd_w_a': normal((nc, LRU_BLOCKS, LRU_BLOCK, LRU_BLOCK), LRU_BLOCK ** -0.5),
        'c_fwd_b_a': normal((nc, LRU_WIDTH), 0.01),
        'c_fwd_w_x': normal((nc, LRU_BLOCKS, LRU_BLOCK, LRU_BLOCK), LRU_BLOCK ** -0.5),
        'c_fwd_b_x': normal((nc, LRU_WIDTH), 0.01),
        'c_fwd_lam': lru_lambda(nc),
        'c_bwd_w_a': normal((nc, LRU_BLOCKS, LRU_BLOCK, LRU_BLOCK), LRU_BLOCK ** -0.5),
        'c_bwd_b_a': normal((nc, LRU_WIDTH), 0.01),
        'c_bwd_w_x': normal((nc, LRU_BLOCKS, LRU_BLOCK, LRU_BLOCK), LRU_BLOCK ** -0.5),
        'c_bwd_b_x': normal((nc, LRU_WIDTH), 0.01),
        'c_bwd_lam': lru_lambda(nc),
        'c_w_o': normal((nc, LRU_WIDTH, D), LRU_WIDTH ** -0.5),
        'd_w_qkv': normal((nd, D, (GQA_Q_HEADS + 2 * GQA_KV_HEADS) * HEAD_DIM), D ** -0.5),
        'd_q_norm_g': gain((nd, HEAD_DIM)),
        'd_k_norm_g': gain((nd, HEAD_DIM)),
        'd_w_o': normal((nd, D, D), D ** -0.5),
    }


def reference(x, c, ctx, c_ctx, norm1_g, norm2_g, w_mod, b_mod, w_mlp1, w_mlp2,
              a_w_qkv, a_q_norm_g, a_k_norm_g, a_lambda_q1, a_lambda_k1, a_lambda_q2, a_lambda_k2,
              a_subln_g, a_w_o,
              b_w_qkv, b_q_norm_g, b_k_norm_g, b_rpb, b_w_o,
              c_w_in, c_conv_w, c_conv_b, c_fwd_w_a, c_fwd_b_a, c_fwd_w_x, c_fwd_b_x, c_fwd_lam,
              c_bwd_w_a, c_bwd_b_a, c_bwd_w_x, c_bwd_b_x, c_bwd_lam, c_w_o,
              d_w_qkv, d_q_norm_g, d_k_norm_g, d_w_o):
    S = x.shape[1]
    cos, sin = axial_rope_table(S)
    mixer_params = (
        (a_w_qkv, a_q_norm_g, a_k_norm_g, a_lambda_q1, a_lambda_k1, a_lambda_q2, a_lambda_k2, a_subln_g, a_w_o),
        (b_w_qkv, b_q_norm_g, b_k_norm_g, b_rpb, b_w_o),
        (c_w_in, c_conv_w, c_conv_b, c_fwd_w_a, c_fwd_b_a, c_fwd_w_x, c_fwd_b_x, c_fwd_lam,
         c_bwd_w_a, c_bwd_b_a, c_bwd_w_x, c_bwd_b_x, c_bwd_lam, c_w_o),
        (d_w_qkv, d_q_norm_g, d_k_norm_g, d_w_o),
    )
    for i in range(DEPTH):
        kind = i % N_MIXERS
        p = tuple(arr[i // N_MIXERS] for arr in mixer_params[kind])
        need_ctx = i < DEPTH - 1
        mod_l = jnp.split((jax.nn.silu(c) @ w_mod[i] + b_mod[i])[:, None, :], N_MOD, axis=-1)
        mod_c = jnp.split((jax.nn.silu(c_ctx) @ w_mod[i] + b_mod[i])[None, None, :], N_MOD, axis=-1)
        h_l = modulate(rms_norm(x, norm1_g[i]), mod_l[0], mod_l[1])
        h_c = modulate(rms_norm(ctx, norm1_g[i]), mod_c[0], mod_c[1])
        if kind == 0:
            y_l, y_c = diff_attention_mixer(h_l, h_c, p, i, cos, sin, need_ctx)
        elif kind == 1:
            y_l, y_c = neighbourhood_mixer(h_l, h_c, p, need_ctx)
        elif kind == 2:
            y_l, y_c = rglru_mixer(h_l, h_c, p, need_ctx)
        else:
            y_l, y_c = gqa_mixer(h_l, h_c, p, cos, sin, need_ctx)
        x = x + mod_l[2] * y_l
        x = x + mod_l[5] * squared_relu_mlp(modulate(rms_norm(x, norm2_g[i]), mod_l[3], mod_l[4]),
                                            w_mlp1[i], w_mlp2[i])
        if need_ctx:
            ctx = ctx + mod_c[2] * y_c
            ctx = ctx + mod_c[5] * squared_relu_mlp(modulate(rms_norm(ctx, norm2_g[i]), mod_c[3], mod_c[4]),
                                                    w_mlp1[i], w_mlp2[i])
    return x
```

```python
import functools
import math

import jax
import jax.numpy as jnp
from jax import lax
from jax.experimental import pallas as pl
from jax.experimental.pallas import tpu as pltpu

F32 = jnp.float32
BF16 = jnp.bfloat16

D_MODEL = 1024
HEAD_DIM = 64
GRID_W = 64
N_MOD = 6
ATTN_SCALE = HEAD_DIM ** -0.5
ROPE_THETA = 10000.0
NORM_EPS = 1e-6
DA_HEADS = D_MODEL // (2 * HEAD_DIM)
NA_ROWS_MAX = 8
NA_COLS = 16
LRU_BLOCK = 256
LRU_CONV = 4
LRU_C = 8.0
GQA_KV_HEADS = 4
D_FF = 4 * D_MODEL

LANES = 128
SUBLANES = 8
MXU_DIM = 256
VMEM_LIMIT = 56 * 1024 * 1024

TOK_TILE = 256
N_PAIRS = D_MODEL // LANES
MASK_VALUE = -1e30
NA_Q_ROWS = TOK_TILE // GRID_W
NA_K_ROWS = 12
NA_BAND = NA_K_ROWS * GRID_W


def _params(n_axes):
    return pltpu.CompilerParams(
        dimension_semantics=("arbitrary",) * n_axes, vmem_limit_bytes=VMEM_LIMIT)


def _resident(shape):
    zeros = (0,) * len(shape)
    return pl.BlockSpec(shape, lambda *_: zeros, pipeline_mode=pl.Buffered(1))


def _dot(a, b):
    return jnp.dot(a, b, preferred_element_type=F32)


def _dot_nt(a, b):
    return lax.dot_general(a, b, (((1,), (1,)), ((), ())), preferred_element_type=F32)


def _norm_mod(x, gain, shift, scale):
    y = x * lax.rsqrt(jnp.mean(x * x, axis=-1, keepdims=True) + NORM_EPS) * gain
    return y * (1.0 + scale) + shift


def _group_mean_matrix():
    r = lax.broadcasted_iota(jnp.int32, (MXU_DIM, MXU_DIM), 0) // HEAD_DIM
    c = lax.broadcasted_iota(jnp.int32, (MXU_DIM, MXU_DIM), 1) // HEAD_DIM
    return jnp.where(r == c, 1.0 / HEAD_DIM, 0.0).astype(BF16)


def _head_rms(x, gain, mean_mat):
    sq = (x * x).astype(BF16)
    ms = jnp.concatenate(
        [_dot(sq[:, j:j + MXU_DIM], mean_mat) for j in range(0, x.shape[1], MXU_DIM)], axis=-1)
    return x * lax.rsqrt(ms + NORM_EPS) * gain


def _rope(x, cos, sin):
    width = x.shape[1]
    reps = width // LANES
    c = jnp.concatenate([cos] * reps, axis=-1)
    s = jnp.concatenate([sin] * reps, axis=-1)
    lane = lax.broadcasted_iota(jnp.int32, x.shape, 1)
    partner = jnp.where((lane & 1) == 0, pltpu.roll(x, width - 1, 1), pltpu.roll(x, 1, 1))
    return x * c + partner * s


def _mods_kernel(c_ref, w_ref, b_ref, o_ref):
    c = c_ref[...]
    a = (c * jax.nn.sigmoid(c)).astype(BF16)
    o_ref[0] = _dot(a, w_ref[0].astype(BF16)) + b_ref[0]


def _modulation(c_all, w_mod, b_mod):
    depth, d, n = w_mod.shape
    rows = c_all.shape[0]
    tn = n // 4
    return pl.pallas_call(
        _mods_kernel,
        grid=(depth, n // tn),
        in_specs=[pl.BlockSpec((rows, d), lambda l, j: (0, 0)),
                  pl.BlockSpec((1, d, tn), lambda l, j: (l, 0, j)),
                  pl.BlockSpec((1, 1, tn), lambda l, j: (l, 0, j))],
        out_specs=pl.BlockSpec((1, rows, tn), lambda l, j: (l, 0, j)),
        out_shape=jax.ShapeDtypeStruct((depth, rows, n), F32),
        compiler_params=_params(2),
        name="adaln_modulation",
    )(c_all, w_mod, b_mod.reshape(depth, 1, n))


def _pre_attn_kernel(*refs, k_width, rope, dup_kv):
    if rope:
        (x_ref, mod_ref, ng_ref, w_ref, qg_ref, kg_ref, cos_ref, sin_ref,
         q_ref, k_ref, v_ref) = refs
    else:
        x_ref, mod_ref, ng_ref, w_ref, qg_ref, kg_ref, q_ref, k_ref, v_ref = refs
    m = mod_ref[0]
    h = _norm_mod(x_ref[0], ng_ref[...], m[0:1, :], m[1:2, :]).astype(BF16)
    mean_mat = _group_mean_matrix()
    d = D_MODEL
    q = _head_rms(_dot(h, w_ref[:, 0:d]), qg_ref[...], mean_mat)
    k = _head_rms(_dot(h, w_ref[:, d:d + k_width]), kg_ref[...], mean_mat)
    v = _dot(h, w_ref[:, d + k_width:d + 2 * k_width])
    if rope:
        q = _rope(q, cos_ref[...], sin_ref[...])
        k = _rope(k, cos_ref[...], sin_ref[...])
    q = q * ATTN_SCALE
    for p in range(N_PAIRS):
        q_ref[0, p] = q[:, p * LANES:(p + 1) * LANES].astype(BF16)
    if dup_kv:
        for g in range(k_width // HEAD_DIM):
            kh = k[:, g * HEAD_DIM:(g + 1) * HEAD_DIM]
            vh = v[:, g * HEAD_DIM:(g + 1) * HEAD_DIM]
            k_ref[0, g] = jnp.concatenate([kh, kh], axis=-1).astype(BF16)
            v_ref[0, g] = jnp.concatenate([vh, vh], axis=-1).astype(BF16)
    else:
        for p in range(k_width // LANES):
            k_ref[0, p] = k[:, p * LANES:(p + 1) * LANES].astype(BF16)
            v_ref[0, p] = v[:, p * LANES:(p + 1) * LANES].astype(BF16)


def _mod_spec(nct):
    return pl.BlockSpec((1, N_MOD, D_MODEL),
                        lambda b, t: (2 * b + jnp.where(t >= nct, 1, 0), 0, 0))


def _pre_attn(xs, modsel, norm_g, w_qkv, q_gain, k_gain, rope_tabs, *, k_width, dup_kv, nct):
    bsz, n_tok, d = xs.shape
    nt = n_tok // TOK_TILE
    n_w = w_qkv.shape[1]
    kv_slots = k_width // HEAD_DIM if dup_kv else k_width // LANES
    rope = rope_tabs is not None
    in_specs = [pl.BlockSpec((1, TOK_TILE, d), lambda b, t: (b, t, 0)),
                _mod_spec(nct),
                _resident((1, d)),
                _resident((d, n_w)),
                _resident((1, d)),
                _resident((1, k_width))]
    args = [xs, modsel, norm_g.reshape(1, d), w_qkv,
            jnp.tile(q_gain, d // HEAD_DIM).reshape(1, d),
            jnp.tile(k_gain, k_width // HEAD_DIM).reshape(1, k_width)]
    if rope:
        in_specs += [pl.BlockSpec((TOK_TILE, LANES), lambda b, t: (t, 0))] * 2
        args += list(rope_tabs)
    qo = jax.ShapeDtypeStruct((bsz, N_PAIRS, n_tok, LANES), BF16)
    kvo = jax.ShapeDtypeStruct((bsz, kv_slots, n_tok, LANES), BF16)
    return pl.pallas_call(
        functools.partial(_pre_attn_kernel, k_width=k_width, rope=rope, dup_kv=dup_kv),
        grid=(bsz, nt),
        in_specs=in_specs,
        out_specs=[pl.BlockSpec((1, N_PAIRS, TOK_TILE, LANES), lambda b, t: (b, 0, t, 0)),
                   pl.BlockSpec((1, kv_slots, TOK_TILE, LANES), lambda b, t: (b, 0, t, 0)),
                   pl.BlockSpec((1, kv_slots, TOK_TILE, LANES), lambda b, t: (b, 0, t, 0))],
        out_shape=[qo, kvo, kvo],
        compiler_params=_params(2),
        name="norm_mod_qkv",
    )(*args)


def _pre_lru_kernel(x_ref, mod_ref, ng_ref, w_ref, gate_ref, rec_ref):
    m = mod_ref[0]
    h = _norm_mod(x_ref[0], ng_ref[...], m[0:1, :], m[1:2, :]).astype(BF16)
    d = D_MODEL
    gate_ref[0] = jax.nn.gelu(_dot(h, w_ref[:, 0:d])).astype(BF16)
    rec_ref[0] = _dot(h, w_ref[:, d:2 * d])


def _pre_lru(xs, modsel, norm_g, w_in, *, nct):
    bsz, n_tok, d = xs.shape
    nt = n_tok // TOK_TILE
    tok_spec = pl.BlockSpec((1, TOK_TILE, d), lambda b, t: (b, t, 0))
    return pl.pallas_call(
        _pre_lru_kernel,
        grid=(bsz, nt),
        in_specs=[tok_spec, _mod_spec(nct), _resident((1, d)), _resident((d, 2 * d))],
        out_specs=[tok_spec, tok_spec],
        out_shape=[jax.ShapeDtypeStruct((bsz, n_tok, d), BF16),
                   jax.ShapeDtypeStruct((bsz, n_tok, d), F32)],
        compiler_params=_params(2),
        name="norm_mod_lru_in",
    )(xs, modsel, norm_g.reshape(1, d), w_in)


def _split_pair(q2):
    low = lax.broadcasted_iota(jnp.int32, q2.shape, 1) < HEAD_DIM
    zero = jnp.zeros_like(q2)
    return jnp.where(low, q2, zero), jnp.where(low, zero, q2)


def _exp_parts(parts):
    m = functools.reduce(jnp.maximum, [jnp.max(s, axis=-1, keepdims=True) for s in parts])
    es = [jnp.exp(s - m) for s in parts]
    denom = functools.reduce(jnp.add, [jnp.sum(e, axis=-1, keepdims=True) for e in es])
    return es, denom


def _diff_attn_kernel(lam_ref, sg_ref, q_ref, k_ref, v_ref, o_ref, *, nct, n_ctx, n_tok, lam_init):
    t = pl.program_id(1)
    lv = lam_ref[...]
    lam = (jnp.exp(jnp.sum(lv[0:1, :] * lv[1:2, :], axis=-1, keepdims=True))
           - jnp.exp(jnp.sum(lv[2:3, :] * lv[3:4, :], axis=-1, keepdims=True)) + lam_init)

    def run(n_keys):
        def head(h, carry):
            q1, q2 = _split_pair(q_ref[0, h])
            k2 = k_ref[0, h, pl.ds(0, n_keys), :]
            v2 = v_ref[0, h, pl.ds(0, n_keys), :]
            (e1,), l1 = _exp_parts([_dot_nt(q1, k2)])
            (e2,), l2 = _exp_parts([_dot_nt(q2, k2)])
            w = e1 * (1.0 / l1) - e2 * (lam / l2)
            o = _dot(w.astype(BF16), v2)
            o = o * lax.rsqrt(jnp.mean(o * o, axis=-1, keepdims=True) + NORM_EPS) * sg_ref[...]
            o_ref[0, h] = (o * (1.0 - lam_init)).astype(BF16)
            return carry
        lax.fori_loop(0, DA_HEADS, head, 0)

    @pl.when(t < nct)
    def _():
        run(n_ctx)

    @pl.when(t >= nct)
    def _():
        run(n_tok)


def _diff_attention(q, k, v, lam_vecs, subln_g, *, nct, n_ctx, lam_init):
    bsz, _, n_tok, _ = q.shape
    nt = n_tok // TOK_TILE
    tile = pl.BlockSpec((1, N_PAIRS, TOK_TILE, LANES), lambda b, t: (b, 0, t, 0))
    whole = pl.BlockSpec((1, DA_HEADS, n_tok, LANES), lambda b, t: (b, 0, 0, 0))
    return pl.pallas_call(
        functools.partial(_diff_attn_kernel, nct=nct, n_ctx=n_ctx, n_tok=n_tok, lam_init=lam_init),
        grid=(bsz, nt),
        in_specs=[_resident((4, HEAD_DIM)), _resident((1, LANES)), tile, whole, whole],
        out_specs=tile,
        out_shape=jax.ShapeDtypeStruct(q.shape, BF16),
        compiler_params=_params(2),
        name="diff_attention",
    )(lam_vecs, subln_g.reshape(1, LANES), q, k, v)


def _pair_attention(q2, key_parts, value_parts, bias_parts):
    low = lax.broadcasted_iota(jnp.int32, (q2.shape[0], LANES), 1) < HEAD_DIM
    outs = []
    for half, qh in enumerate(_split_pair(q2)):
        scores = []
        for kp, bp in zip(key_parts, bias_parts):
            s = _dot_nt(qh, kp)
            if bp is not None:
                s = s + bp[half].astype(F32)
            scores.append(s)
        es, denom = _exp_parts(scores)
        o = functools.reduce(jnp.add, [_dot(e.astype(BF16), vp) for e, vp in zip(es, value_parts)])
        outs.append(o * (1.0 / denom))
    return jnp.where(low, outs[0], outs[1])


def _gqa_attn_kernel(q_ref, k_ref, v_ref, o_ref, *, n_keys):
    def pair(p, carry):
        g = p // 2
        k2 = k_ref[0, g, pl.ds(0, n_keys), :]
        v2 = v_ref[0, g, pl.ds(0, n_keys), :]
        o_ref[0, p] = _pair_attention(q_ref[0, p], [k2], [v2], [None]).astype(BF16)
        return carry
    lax.fori_loop(0, N_PAIRS, pair, 0)


def _gqa_attention(q, k, v, *, nct):
    bsz, _, n_tok, _ = q.shape
    nt = n_tok // TOK_TILE - nct
    tile = pl.BlockSpec((1, N_PAIRS, TOK_TILE, LANES), lambda b, t: (b, 0, t + nct, 0))
    whole = pl.BlockSpec((1, GQA_KV_HEADS, n_tok, LANES), lambda b, t: (b, 0, 0, 0))
    return pl.pallas_call(
        functools.partial(_gqa_attn_kernel, n_keys=n_tok),
        grid=(bsz, nt),
        in_specs=[tile, whole, whole],
        out_specs=tile,
        out_shape=jax.ShapeDtypeStruct(q.shape, BF16),
        compiler_params=_params(2),
        name="gqa_attention",
    )(q, k, v)


def _na_attn_kernel(bias_ref, q_ref, k_ref, v_ref, o_ref, *, nct, n_ctx, n_rows):
    t = pl.program_id(1)

    @pl.when(t < nct)
    def _():
        def pair(p, carry):
            k2 = k_ref[0, p, pl.ds(0, n_ctx), :]
            v2 = v_ref[0, p, pl.ds(0, n_ctx), :]
            o_ref[0, p] = _pair_attention(q_ref[0, p], [k2], [v2], [None]).astype(BF16)
            return carry
        lax.fori_loop(0, N_PAIRS, pair, 0)

    @pl.when(t >= nct)
    def _():
        first_row = jnp.clip(NA_Q_ROWS * (t - nct) - NA_ROWS_MAX // 2, 0, n_rows - NA_K_ROWS)
        start = pl.multiple_of(n_ctx + first_row * GRID_W, GRID_W)

        def pair(p, carry):
            kc = k_ref[0, p, pl.ds(0, n_ctx), :]
            vc = v_ref[0, p, pl.ds(0, n_ctx), :]
            kb = k_ref[0, p, pl.ds(start, NA_BAND), :]
            vb = v_ref[0, p, pl.ds(start, NA_BAND), :]
            bias = (bias_ref[0, 2 * p], bias_ref[0, 2 * p + 1])
            o_ref[0, p] = _pair_attention(
                q_ref[0, p], [kc, kb], [vc, vb], [None, bias]).astype(BF16)
            return carry
        lax.fori_loop(0, N_PAIRS, pair, 0)


def _na_bias_table(rpb, n_rows):
    heads = rpb.shape[0]
    n_tiles = n_rows // NA_Q_ROWS
    kr_rows = min(NA_ROWS_MAX, n_rows)
    tt = jnp.arange(n_tiles)[:, None, None]
    qr = jnp.arange(NA_Q_ROWS)[None, :, None]
    kr = jnp.arange(NA_K_ROWS)[None, None, :]
    r = NA_Q_ROWS * tt + qr
    rp = jnp.clip(NA_Q_ROWS * tt - NA_ROWS_MAX // 2, 0, n_rows - NA_K_ROWS) + kr
    r0 = jnp.clip(r - kr_rows // 2, 0, n_rows - kr_rows)
    row_ok = (rp >= r0) & (rp < r0 + kr_rows)
    row_idx = jnp.clip(rp - r + NA_ROWS_MAX - 1, 0, 2 * NA_ROWS_MAX - 2)
    c = jnp.arange(GRID_W)[:, None]
    cp = jnp.arange(GRID_W)[None, :]
    cs = jnp.clip(c - NA_COLS // 2, 0, GRID_W - NA_COLS)
    col_ok = (cp >= cs) & (cp < cs + NA_COLS)
    col_idx = jnp.clip(cp - c, -(NA_COLS - 1), NA_COLS - 1) + NA_COLS - 1
    rows_sel = rpb[:, row_idx, :]
    onehot = (col_idx.reshape(-1)[None, :] == jnp.arange(2 * NA_COLS - 1)[:, None]).astype(F32)
    tab = jnp.dot(rows_sel.reshape(-1, 2 * NA_COLS - 1), onehot, precision=lax.Precision.HIGHEST)
    tab = tab.reshape(heads, n_tiles, NA_Q_ROWS, NA_K_ROWS, GRID_W, GRID_W)
    ok = row_ok[None, :, :, :, None, None] & col_ok[None, None, None, None, :, :]
    tab = jnp.where(ok, tab, MASK_VALUE)
    tab = tab.transpose(1, 0, 2, 4, 3, 5)
    return tab.reshape(n_tiles, heads, TOK_TILE, NA_BAND).astype(BF16)


def _na_attention(q, k, v, bias_tab, *, nct, n_ctx):
    bsz, _, n_tok, _ = q.shape
    nt = n_tok // TOK_TILE
    heads = bias_tab.shape[1]
    n_rows = (n_tok - n_ctx) // GRID_W
    tile = pl.BlockSpec((1, N_PAIRS, TOK_TILE, LANES), lambda b, t: (b, 0, t, 0))
    whole = pl.BlockSpec((1, N_PAIRS, n_tok, LANES), lambda b, t: (b, 0, 0, 0))
    bias_spec = pl.BlockSpec((1, heads, TOK_TILE, NA_BAND),
                             lambda b, t: (jnp.maximum(t - nct, 0), 0, 0, 0))
    return pl.pallas_call(
        functools.partial(_na_attn_kernel, nct=nct, n_ctx=n_ctx, n_rows=n_rows),
        grid=(bsz, nt),
        in_specs=[bias_spec, tile, whole, whole],
        out_specs=tile,
        out_shape=jax.ShapeDtypeStruct(q.shape, BF16),
        compiler_params=_params(2),
        name="neighbourhood_attention",
    )(bias_tab, q, k, v)


LRU_CHUNK = 256
LRU_PAD = 8


def _softplus(z):
    return jnp.maximum(z, 0.0) + jnp.log1p(jnp.exp(-jnp.abs(z)))


def _lru_kernel(rec_ref, gate_ref, cw_ref, cb_ref,
                fwa_ref, fwx_ref, bwa_ref, bwx_ref,
                fba_ref, fbx_ref, bba_ref, bbx_ref, flam_ref, blam_ref,
                o_ref, xp_ref, af_ref, bf_ref, ab_ref, bb_ref, *, n_ctx, n_tok):
    width = LRU_BLOCK
    xp_ref[0:LRU_PAD, :] = jnp.zeros((LRU_PAD, width), F32)
    xp_ref[LRU_PAD + n_tok:2 * LRU_PAD + n_tok, :] = jnp.zeros((LRU_PAD, width), F32)
    xp_ref[LRU_PAD:LRU_PAD + n_tok, :] = rec_ref[0]

    directions = (
        (fwa_ref, fwx_ref, fba_ref, fbx_ref, _softplus(-flam_ref[...]), af_ref, bf_ref, False),
        (bwa_ref, bwx_ref, bba_ref, bbx_ref, _softplus(-blam_ref[...]), ab_ref, bb_ref, True),
    )
    n_blk = LRU_CHUNK // SUBLANES
    sub = lax.broadcasted_iota(jnp.int32, (n_blk, SUBLANES, width), 1)

    def chunk(ci, carry):
        r0 = pl.multiple_of(ci * LRU_CHUNK, LRU_CHUNK)
        xw = xp_ref[pl.ds(r0, LRU_CHUNK + 2 * LRU_PAD), :]
        row = r0 + lax.broadcasted_iota(jnp.int32, (LRU_CHUNK, 1), 0)
        seg_lo = jnp.where(row < n_ctx, 0, n_ctx)
        seg_hi = jnp.where(row < n_ctx, n_ctx, n_tok)
        u = cb_ref[...]
        for j in range(LRU_CONV):
            off = j - LRU_CONV // 2
            src = row + off
            tap = xw[LRU_PAD + off:LRU_PAD + off + LRU_CHUNK, :]
            u = u + jnp.where((src >= seg_lo) & (src < seg_hi), tap, 0.0) * cw_ref[j:j + 1, :]
        ub = u.astype(BF16)
        for wa_ref, wx_ref, ba_ref, bx_ref, sp, a_ref, b_ref, reverse in directions:
            r = jax.nn.sigmoid(_dot(ub, wa_ref[0]) + ba_ref[...])
            i = jax.nn.sigmoid(_dot(ub, wx_ref[0]) + bx_ref[...])
            log_a = -LRU_C * r * sp
            a = jnp.exp(log_a)
            b = jnp.sqrt(-jnp.tanh(log_a) * (a * a + 1.0)) * (i * u)
            a3 = a.reshape(n_blk, SUBLANES, width)
            b3 = b.reshape(n_blk, SUBLANES, width)
            for dist in (1, 2, 4):
                if reverse:
                    shift, ok = SUBLANES - dist, sub < SUBLANES - dist
                else:
                    shift, ok = dist, sub >= dist
                a_prev = jnp.where(ok, pltpu.roll(a3, shift, 1), 1.0)
                b_prev = jnp.where(ok, pltpu.roll(b3, shift, 1), 0.0)
                b3 = a3 * b_prev + b3
                a3 = a3 * a_prev
            a_ref[pl.ds(r0, LRU_CHUNK), :] = a3.reshape(LRU_CHUNK, width)
            b_ref[pl.ds(r0, LRU_CHUNK), :] = b3.reshape(LRU_CHUNK, width)
        return carry

    lax.fori_loop(0, n_tok // LRU_CHUNK, chunk, 0)

    n_all = n_tok // SUBLANES
    n_cb = n_ctx // SUBLANES

    def step(i, carry):
        cf, cr = carry
        rf = pl.multiple_of(i * SUBLANES, SUBLANES)
        jb = jnp.where(i < n_cb, n_cb - 1 - i, n_all - 1 - (i - n_cb))
        rb = pl.multiple_of(jb * SUBLANES, SUBLANES)
        hf = af_ref[pl.ds(rf, SUBLANES), :] * cf + bf_ref[pl.ds(rf, SUBLANES), :]
        bf_ref[pl.ds(rf, SUBLANES), :] = hf
        hb = ab_ref[pl.ds(rb, SUBLANES), :] * cr + bb_ref[pl.ds(rb, SUBLANES), :]
        bb_ref[pl.ds(rb, SUBLANES), :] = hb
        return (jnp.broadcast_to(hf[SUBLANES - 1:SUBLANES, :], (SUBLANES, width)),
                jnp.broadcast_to(hb[0:1, :], (SUBLANES, width)))

    zero = jnp.zeros((SUBLANES, width), F32)
    lax.fori_loop(0, n_all, step, (zero, zero))

    y = (bf_ref[...] + bb_ref[...]) * gate_ref[0].astype(F32)
    for p in range(width // LANES):
        o_ref[0, p] = y[:, p * LANES:(p + 1) * LANES].astype(BF16)


def _lru_mixer(rec, gate, p, *, n_ctx):
    (conv_w, conv_b, f_wa, f_ba, f_wx, f_bx, f_lam, b_wa, b_ba, b_wx, b_bx, b_lam) = p
    bsz, n_tok, d = rec.shape
    n_blocks = d // LRU_BLOCK
    seq = pl.BlockSpec((1, n_tok, LRU_BLOCK), lambda b, n: (b, 0, n))
    vec = pl.BlockSpec((1, LRU_BLOCK), lambda b, n: (0, n))
    mat = pl.BlockSpec((1, LRU_BLOCK, LRU_BLOCK), lambda b, n: (n, 0, 0))
    row = lambda a: a.reshape(1, d)
    scratch = [pltpu.VMEM((n_tok + 2 * LRU_PAD, LRU_BLOCK), F32)]
    scratch += [pltpu.VMEM((n_tok, LRU_BLOCK), F32)] * 4
    return pl.pallas_call(
        functools.partial(_lru_kernel, n_ctx=n_ctx, n_tok=n_tok),
        grid=(bsz, n_blocks),
        in_specs=[seq, seq, pl.BlockSpec((LRU_CONV, LRU_BLOCK), lambda b, n: (0, n)), vec,
                  mat, mat, mat, mat, vec, vec, vec, vec, vec, vec],
        out_specs=pl.BlockSpec((1, LRU_BLOCK // LANES, n_tok, LANES), lambda b, n: (b, n, 0, 0)),
        out_shape=jax.ShapeDtypeStruct((bsz, d // LANES, n_tok, LANES), BF16),
        scratch_shapes=scratch,
        compiler_params=_params(2),
        name="rglru_scan",
    )(rec, gate, conv_w, row(conv_b),
      f_wa.astype(BF16), f_wx.astype(BF16), b_wa.astype(BF16), b_wx.astype(BF16),
      row(f_ba), row(f_bx), row(b_ba), row(b_bx), row(f_lam), row(b_lam))


FF_CHUNK = 1024


def _post_kernel(o_ref, x_ref, mod_ref, ng_ref, wo_ref, w1_ref, w2_ref, out_ref):
    m = mod_ref[0]
    o = jnp.concatenate([o_ref[0, p] for p in range(N_PAIRS)], axis=-1)
    x1 = x_ref[0] + m[2:3, :] * _dot(o, wo_ref[...])
    h = _norm_mod(x1, ng_ref[...], m[3:4, :], m[4:5, :]).astype(BF16)
    acc = jnp.zeros_like(x1)
    for c in range(0, D_FF, FF_CHUNK):
        a = jnp.maximum(_dot(h, w1_ref[:, c:c + FF_CHUNK]), 0.0)
        acc = acc + _dot((a * a).astype(BF16), w2_ref[c:c + FF_CHUNK, :])
    out_ref[0] = x1 + m[5:6, :] * acc


def _post(o, xs, modsel, norm_g, w_o, w1, w2, *, nct, skip_ctx):
    bsz, n_tok, d = xs.shape
    first = nct if skip_ctx else 0
    nt = n_tok // TOK_TILE - first
    return pl.pallas_call(
        _post_kernel,
        grid=(bsz, nt),
        in_specs=[pl.BlockSpec((1, N_PAIRS, TOK_TILE, LANES), lambda b, t: (b, 0, t + first, 0)),
                  pl.BlockSpec((1, TOK_TILE, d), lambda b, t: (b, t + first, 0)),
                  pl.BlockSpec((1, N_MOD, d),
                               lambda b, t: (2 * b + jnp.where(t + first >= nct, 1, 0), 0, 0)),
                  _resident((1, d)), _resident((d, d)), _resident((d, D_FF)), _resident((D_FF, d))],
        out_specs=pl.BlockSpec((1, TOK_TILE, d), lambda b, t: (b, t, 0)),
        out_shape=jax.ShapeDtypeStruct((bsz, nt * TOK_TILE, d), F32),
        compiler_params=_params(2),
        name="proj_residual_mlp",
    )(o, xs, modsel, norm_g.reshape(1, d), w_o, w1, w2)


def _rope_tables(n_ctx, n_lat):
    t = jnp.arange(n_lat)
    row = (t // GRID_W).astype(F32)
    col = (t % GRID_W).astype(F32)
    n_freq = HEAD_DIM // 4
    inv = ROPE_THETA ** (-jnp.arange(n_freq, dtype=F32) / n_freq)
    ang = jnp.concatenate([row[:, None] * inv, col[:, None] * inv], axis=-1)
    cos = jnp.repeat(jnp.cos(ang), 2, axis=-1)
    sin = jnp.repeat(jnp.sin(ang), 2, axis=-1) * jnp.tile(jnp.array([-1.0, 1.0], F32), HEAD_DIM // 2)
    cos = jnp.concatenate([jnp.ones((n_ctx, HEAD_DIM), F32), cos], axis=0)
    sin = jnp.concatenate([jnp.zeros((n_ctx, HEAD_DIM), F32), sin], axis=0)
    return jnp.tile(cos, (1, LANES // HEAD_DIM)), jnp.tile(sin, (1, LANES // HEAD_DIM))


def kernel(x, c, ctx, c_ctx, norm1_g, norm2_g, w_mod, b_mod, w_mlp1, w_mlp2, a_w_qkv, a_q_norm_g, a_k_norm_g, a_lambda_q1, a_lambda_k1, a_lambda_q2, a_lambda_k2, a_subln_g, a_w_o, b_w_qkv, b_q_norm_g, b_k_norm_g, b_rpb, b_w_o, c_w_in, c_conv_w, c_conv_b, c_fwd_w_a, c_fwd_b_a, c_fwd_w_x, c_fwd_b_x, c_fwd_lam, c_bwd_w_a, c_bwd_b_a, c_bwd_w_x, c_bwd_b_x, c_bwd_lam, c_w_o, d_w_qkv, d_q_norm_g, d_k_norm_g, d_w_o):
    bsz, n_lat, d = x.shape
    n_ctx = ctx.shape[1]
    depth = w_mod.shape[0]
    assert d == D_MODEL and depth == 4 and n_ctx % TOK_TILE == 0 and n_lat % TOK_TILE == 0
    assert (n_lat // GRID_W) % NA_Q_ROWS == 0 and n_lat // GRID_W >= NA_K_ROWS
    nct = n_ctx // TOK_TILE

    rows = -(-(bsz + 1) // SUBLANES) * SUBLANES
    c_all = jnp.concatenate([c, c_ctx[None, :], jnp.zeros((rows - bsz - 1, d), F32)], axis=0)
    mods = _modulation(c_all, w_mod, b_mod).reshape(depth, rows, N_MOD, d)
    mod_ctx = jnp.broadcast_to(mods[:, bsz][:, None], (depth, bsz, N_MOD, d))
    modsel = jnp.stack([mod_ctx, mods[:, :bsz]], axis=2).reshape(depth, 2 * bsz, N_MOD, d)

    xs = jnp.concatenate([ctx, x], axis=1)
    rope_tabs = _rope_tables(n_ctx, n_lat)
    w1 = w_mlp1.astype(BF16)
    w2 = w_mlp2.astype(BF16)

    q, k, v = _pre_attn(xs, modsel[0], norm1_g[0], a_w_qkv[0].astype(BF16), a_q_norm_g[0],
                        a_k_norm_g[0], rope_tabs, k_width=d, dup_kv=False, nct=nct)
    lam_vecs = jnp.stack([a_lambda_q1[0], a_lambda_k1[0], a_lambda_q2[0], a_lambda_k2[0]])
    lam_init = 0.8 - 0.6 * math.exp(-0.3 * 0)
    o = _diff_attention(q, k, v, lam_vecs, a_subln_g[0], nct=nct, n_ctx=n_ctx, lam_init=lam_init)
    xs = _post(o, xs, modsel[0], norm2_g[0], a_w_o[0].astype(BF16), w1[0], w2[0],
               nct=nct, skip_ctx=False)

    q, k, v = _pre_attn(xs, modsel[1], norm1_g[1], b_w_qkv[0].astype(BF16), b_q_norm_g[0],
                        b_k_norm_g[0], None, k_width=d, dup_kv=False, nct=nct)
    bias_tab = _na_bias_table(b_rpb[0], n_lat // GRID_W)
    o = _na_attention(q, k, v, bias_tab, nct=nct, n_ctx=n_ctx)
    xs = _post(o, xs, modsel[1], norm2_g[1], b_w_o[0].astype(BF16), w1[1], w2[1],
               nct=nct, skip_ctx=False)

    gate, rec = _pre_lru(xs, modsel[2], norm1_g[2], c_w_in[0].astype(BF16), nct=nct)
    lru_p = (c_conv_w[0], c_conv_b[0], c_fwd_w_a[0], c_fwd_b_a[0], c_fwd_w_x[0], c_fwd_b_x[0],
             c_fwd_lam[0], c_bwd_w_a[0], c_bwd_b_a[0], c_bwd_w_x[0], c_bwd_b_x[0], c_bwd_lam[0])
    o = _lru_mixer(rec, gate, lru_p, n_ctx=n_ctx)
    xs = _post(o, xs, modsel[2], norm2_g[2], c_w_o[0].astype(BF16), w1[2], w2[2],
               nct=nct, skip_ctx=False)

    q, k, v = _pre_attn(xs, modsel[3], norm1_g[3], d_w_qkv[0].astype(BF16), d_q_norm_g[0],
                        d_k_norm_g[0], rope_tabs, k_width=GQA_KV_HEADS * HEAD_DIM, dup_kv=True,
                        nct=nct)
    o = _gqa_attention(q, k, v, nct=nct)
    return _post(o, xs, modsel[3], norm2_g[3], d_w_o[0].astype(BF16), w1[3], w2[3],
                 nct=nct, skip_ctx=True)
```

```python
import functools
import math

import jax
import jax.numpy as jnp
from jax import lax
from jax.experimental import pallas as pl
from jax.experimental.pallas import tpu as pltpu

F32 = jnp.float32
BF16 = jnp.bfloat16

D_MODEL = 1024
HEAD_DIM = 64
GRID_W = 64
N_MOD = 6
ATTN_SCALE = HEAD_DIM ** -0.5
LOG2E = math.log2(math.e)
ROPE_THETA = 10000.0
NORM_EPS = 1e-6
DA_HEADS = D_MODEL // (2 * HEAD_DIM)
NA_ROWS_MAX = 8
NA_COLS = 16
LRU_BLOCK = 256
LRU_CONV = 4
LRU_C = 8.0
GQA_KV_HEADS = 4
D_FF = 4 * D_MODEL

LANES = 128
SUBLANES = 8
MXU_DIM = 256
VMEM_LIMIT = 56 * 1024 * 1024

TOK_TILE = 256
N_PAIRS = D_MODEL // LANES
MASK_VALUE = -1e30
NA_Q_ROWS = TOK_TILE // GRID_W
NA_K_ROWS = 12
NA_BAND = NA_K_ROWS * GRID_W


def _params(n_axes):
    return pltpu.CompilerParams(
        dimension_semantics=("arbitrary",) * n_axes, vmem_limit_bytes=VMEM_LIMIT)


def _resident(shape):
    zeros = (0,) * len(shape)
    return pl.BlockSpec(shape, lambda *_: zeros, pipeline_mode=pl.Buffered(1))


def _dot(a, b):
    return jnp.dot(a, b, preferred_element_type=F32)


def _dot_nt(a, b):
    return lax.dot_general(a, b, (((1,), (1,)), ((), ())), preferred_element_type=F32)


def _norm_mod(x, gain, shift, scale):
    y = x * lax.rsqrt(jnp.mean(x * x, axis=-1, keepdims=True) + NORM_EPS) * gain
    return y * (1.0 + scale) + shift


def _group_mean_matrix():
    r = lax.broadcasted_iota(jnp.int32, (MXU_DIM, MXU_DIM), 0) // HEAD_DIM
    c = lax.broadcasted_iota(jnp.int32, (MXU_DIM, MXU_DIM), 1) // HEAD_DIM
    return jnp.where(r == c, 1.0 / HEAD_DIM, 0.0).astype(BF16)


def _head_rms(x, gain, mean_mat):
    sq = (x * x).astype(BF16)
    ms = jnp.concatenate(
        [_dot(sq[:, j:j + MXU_DIM], mean_mat) for j in range(0, x.shape[1], MXU_DIM)], axis=-1)
    return x * lax.rsqrt(ms + NORM_EPS) * gain


def _rope(x, cos, sin):
    width = x.shape[1]
    reps = width // LANES
    c = jnp.concatenate([cos] * reps, axis=-1)
    s = jnp.concatenate([sin] * reps, axis=-1)
    lane = lax.broadcasted_iota(jnp.int32, x.shape, 1)
    partner = jnp.where((lane & 1) == 0, pltpu.roll(x, width - 1, 1), pltpu.roll(x, 1, 1))
    return x * c + partner * s


def _mods_kernel(c_ref, w_ref, b_ref, o_ref):
    c = c_ref[...]
    a = (c * jax.nn.sigmoid(c)).astype(BF16)
    o_ref[0] = _dot(a, w_ref[0].astype(BF16)) + b_ref[0]


def _modulation(c_all, w_mod, b_mod):
    depth, d, n = w_mod.shape
    rows = c_all.shape[0]
    tn = n // 4
    return pl.pallas_call(
        _mods_kernel,
        grid=(depth, n // tn),
        in_specs=[pl.BlockSpec((rows, d), lambda l, j: (0, 0)),
                  pl.BlockSpec((1, d, tn), lambda l, j: (l, 0, j)),
                  pl.BlockSpec((1, 1, tn), lambda l, j: (l, 0, j))],
        out_specs=pl.BlockSpec((1, rows, tn), lambda l, j: (l, 0, j)),
        out_shape=jax.ShapeDtypeStruct((depth, rows, n), F32),
        compiler_params=_params(2),
        name="adaln_modulation",
    )(c_all, w_mod, b_mod.reshape(depth, 1, n))


def _pre_attn_kernel(*refs, k_width, rope, dup_kv, v_ones):
    if rope:
        (x_ref, mod_ref, ng_ref, w_ref, qg_ref, kg_ref, cos_ref, sin_ref,
         q_ref, k_ref, v_ref) = refs
    else:
        x_ref, mod_ref, ng_ref, w_ref, qg_ref, kg_ref, q_ref, k_ref, v_ref = refs
    m = mod_ref[0]
    h = _norm_mod(x_ref[0], ng_ref[...], m[0:1, :], m[1:2, :]).astype(BF16)
    mean_mat = _group_mean_matrix()
    d = D_MODEL
    q = _head_rms(_dot(h, w_ref[:, 0:d]), qg_ref[...], mean_mat)
    k = _head_rms(_dot(h, w_ref[:, d:d + k_width]), kg_ref[...], mean_mat)
    v = _dot(h, w_ref[:, d + k_width:d + 2 * k_width])
    if rope:
        q = _rope(q, cos_ref[...], sin_ref[...])
        k = _rope(k, cos_ref[...], sin_ref[...])
    q = q * (ATTN_SCALE * LOG2E)
    for p in range(N_PAIRS):
        q_ref[0, p] = q[:, p * LANES:(p + 1) * LANES].astype(BF16)
    if dup_kv:
        for g in range(k_width // HEAD_DIM):
            kh = k[:, g * HEAD_DIM:(g + 1) * HEAD_DIM]
            k_ref[0, g] = jnp.concatenate([kh, kh], axis=-1).astype(BF16)
    else:
        for p in range(k_width // LANES):
            k_ref[0, p] = k[:, p * LANES:(p + 1) * LANES].astype(BF16)
    if v_ones:
        ones = jnp.ones((v.shape[0], HEAD_DIM), F32)
        for g in range(k_width // HEAD_DIM):
            vh = v[:, g * HEAD_DIM:(g + 1) * HEAD_DIM]
            v_ref[0, g] = jnp.concatenate([vh, ones], axis=-1).astype(BF16)
    else:
        for p in range(k_width // LANES):
            v_ref[0, p] = v[:, p * LANES:(p + 1) * LANES].astype(BF16)


def _mod_spec(nct):
    return pl.BlockSpec((1, N_MOD, D_MODEL),
                        lambda b, t: (2 * b + jnp.where(t >= nct, 1, 0), 0, 0))


def _pre_attn(xs, modsel, norm_g, w_qkv, q_gain, k_gain, rope_tabs, *, k_width, dup_kv, v_ones,
              nct):
    bsz, n_tok, d = xs.shape
    nt = n_tok // TOK_TILE
    n_w = w_qkv.shape[1]
    k_slots = k_width // HEAD_DIM if dup_kv else k_width // LANES
    v_slots = k_width // HEAD_DIM if v_ones else k_width // LANES
    rope = rope_tabs is not None
    in_specs = [pl.BlockSpec((1, TOK_TILE, d), lambda b, t: (b, t, 0)),
                _mod_spec(nct),
                _resident((1, d)),
                _resident((d, n_w)),
                _resident((1, d)),
                _resident((1, k_width))]
    args = [xs, modsel, norm_g.reshape(1, d), w_qkv,
            jnp.tile(q_gain, d // HEAD_DIM).reshape(1, d),
            jnp.tile(k_gain, k_width // HEAD_DIM).reshape(1, k_width)]
    if rope:
        in_specs += [pl.BlockSpec((TOK_TILE, LANES), lambda b, t: (t, 0))] * 2
        args += list(rope_tabs)
    qo = jax.ShapeDtypeStruct((bsz, N_PAIRS, n_tok, LANES), BF16)
    ko = jax.ShapeDtypeStruct((bsz, k_slots, n_tok, LANES), BF16)
    vo = jax.ShapeDtypeStruct((bsz, v_slots, n_tok, LANES), BF16)
    return pl.pallas_call(
        functools.partial(_pre_attn_kernel, k_width=k_width, rope=rope, dup_kv=dup_kv,
                          v_ones=v_ones),
        grid=(bsz, nt),
        in_specs=in_specs,
        out_specs=[pl.BlockSpec((1, N_PAIRS, TOK_TILE, LANES), lambda b, t: (b, 0, t, 0)),
                   pl.BlockSpec((1, k_slots, TOK_TILE, LANES), lambda b, t: (b, 0, t, 0)),
                   pl.BlockSpec((1, v_slots, TOK_TILE, LANES), lambda b, t: (b, 0, t, 0))],
        out_shape=[qo, ko, vo],
        compiler_params=_params(2),
        name="norm_mod_qkv",
    )(*args)


def _pre_lru_kernel(x_ref, mod_ref, ng_ref, w_ref, gate_ref, rec_ref):
    m = mod_ref[0]
    h = _norm_mod(x_ref[0], ng_ref[...], m[0:1, :], m[1:2, :]).astype(BF16)
    d = D_MODEL
    gate_ref[0] = jax.nn.gelu(_dot(h, w_ref[:, 0:d])).astype(BF16)
    rec_ref[0] = _dot(h, w_ref[:, d:2 * d])


def _pre_lru(xs, modsel, norm_g, w_in, *, nct):
    bsz, n_tok, d = xs.shape
    nt = n_tok // TOK_TILE
    tok_spec = pl.BlockSpec((1, TOK_TILE, d), lambda b, t: (b, t, 0))
    return pl.pallas_call(
        _pre_lru_kernel,
        grid=(bsz, nt),
        in_specs=[tok_spec, _mod_spec(nct), _resident((1, d)), _resident((d, 2 * d))],
        out_specs=[tok_spec, tok_spec],
        out_shape=[jax.ShapeDtypeStruct((bsz, n_tok, d), BF16),
                   jax.ShapeDtypeStruct((bsz, n_tok, d), F32)],
        compiler_params=_params(2),
        name="norm_mod_lru_in",
    )(xs, modsel, norm_g.reshape(1, d), w_in)


def _split_pair(q2):
    low = lax.broadcasted_iota(jnp.int32, q2.shape, 1) < HEAD_DIM
    zero = jnp.zeros_like(q2)
    return jnp.where(low, q2, zero), jnp.where(low, zero, q2)


def _exp2_parts(parts):
    m = functools.reduce(jnp.maximum, [jnp.max(s, axis=-1, keepdims=True) for s in parts])
    return [jnp.exp2(s - m) for s in parts]


def _diff_attn_kernel(lam_ref, sg_ref, q_ref, k_ref, v_ref, o_ref, *, nct, n_ctx, n_tok, lam_init):
    t = pl.program_id(1)
    lv = lam_ref[...]
    lam = (jnp.exp(jnp.sum(lv[0:1, :] * lv[1:2, :], axis=-1, keepdims=True))
           - jnp.exp(jnp.sum(lv[2:3, :] * lv[3:4, :], axis=-1, keepdims=True)) + lam_init)

    def run(n_keys):
        def head(h, carry):
            q1, q2 = _split_pair(q_ref[0, h])
            k2 = k_ref[0, h, pl.ds(0, n_keys), :]
            v2 = v_ref[0, h, pl.ds(0, n_keys), :]
            (e1,) = _exp2_parts([_dot_nt(q1, k2)])
            (e2,) = _exp2_parts([_dot_nt(q2, k2)])
            l1 = jnp.sum(e1, axis=-1, keepdims=True)
            l2 = jnp.sum(e2, axis=-1, keepdims=True)
            w = e1 - e2 * (lam * l1 / l2)
            o = _dot(w.astype(BF16), v2) * (1.0 / l1)
            o = o * lax.rsqrt(jnp.mean(o * o, axis=-1, keepdims=True) + NORM_EPS) * sg_ref[...]
            o_ref[0, h] = (o * (1.0 - lam_init)).astype(BF16)
            return carry
        lax.fori_loop(0, DA_HEADS, head, 0, unroll=4)

    @pl.when(t < nct)
    def _():
        run(n_ctx)

    @pl.when(t >= nct)
    def _():
        run(n_tok)


def _diff_attention(q, k, v, lam_vecs, subln_g, *, nct, n_ctx, lam_init):
    bsz, _, n_tok, _ = q.shape
    nt = n_tok // TOK_TILE
    tile = pl.BlockSpec((1, N_PAIRS, TOK_TILE, LANES), lambda b, t: (b, 0, t, 0))
    whole = pl.BlockSpec((1, DA_HEADS, n_tok, LANES), lambda b, t: (b, 0, 0, 0))
    return pl.pallas_call(
        functools.partial(_diff_attn_kernel, nct=nct, n_ctx=n_ctx, n_tok=n_tok, lam_init=lam_init),
        grid=(bsz, nt),
        in_specs=[_resident((4, HEAD_DIM)), _resident((1, LANES)), tile, whole, whole],
        out_specs=tile,
        out_shape=jax.ShapeDtypeStruct(q.shape, BF16),
        compiler_params=_params(2),
        name="diff_attention",
    )(lam_vecs, subln_g.reshape(1, LANES), q, k, v)


def _pair_attention(q2, key_parts, value_parts, bias_parts):
    low = lax.broadcasted_iota(jnp.int32, (q2.shape[0], LANES), 1) < HEAD_DIM
    outs = []
    for half, qh in enumerate(_split_pair(q2)):
        scores = []
        for kp, bp in zip(key_parts, bias_parts):
            s = _dot_nt(qh, kp)
            if bp is not None:
                s = s + bp[half].astype(F32)
            scores.append(s)
        es = _exp2_parts(scores)
        acc = functools.reduce(
            jnp.add, [_dot(e.astype(BF16), vp) for e, vp in zip(es, value_parts[half])])
        swapped = pltpu.roll(acc, HEAD_DIM, 1)
        outs.append(acc / swapped if half == 0 else swapped / acc)
    return jnp.where(low, outs[0], outs[1])


def _gqa_attn_kernel(q_ref, k_ref, v_ref, o_ref, *, n_keys):
    def pair(p, carry):
        g = p // 2
        k2 = k_ref[0, g, pl.ds(0, n_keys), :]
        v2 = v_ref[0, g, pl.ds(0, n_keys), :]
        o_ref[0, p] = _pair_attention(q_ref[0, p], [k2], [[v2], [v2]], [None]).astype(BF16)
        return carry
    lax.fori_loop(0, N_PAIRS, pair, 0, unroll=4)


def _gqa_attention(q, k, v, *, nct):
    bsz, _, n_tok, _ = q.shape
    nt = n_tok // TOK_TILE - nct
    tile = pl.BlockSpec((1, N_PAIRS, TOK_TILE, LANES), lambda b, t: (b, 0, t + nct, 0))
    whole = pl.BlockSpec((1, GQA_KV_HEADS, n_tok, LANES), lambda b, t: (b, 0, 0, 0))
    return pl.pallas_call(
        functools.partial(_gqa_attn_kernel, n_keys=n_tok),
        grid=(bsz, nt),
        in_specs=[tile, whole, whole],
        out_specs=tile,
        out_shape=jax.ShapeDtypeStruct(q.shape, BF16),
        compiler_params=_params(2),
        name="gqa_attention",
    )(q, k, v)


def _na_attn_kernel(bias_ref, q_ref, k_ref, v_ref, o_ref, *, nct, n_ctx, n_rows):
    t = pl.program_id(1)

    @pl.when(t < nct)
    def _():
        def pair(p, carry):
            k2 = k_ref[0, p, pl.ds(0, n_ctx), :]
            va = v_ref[0, 2 * p, pl.ds(0, n_ctx), :]
            vb = v_ref[0, 2 * p + 1, pl.ds(0, n_ctx), :]
            o_ref[0, p] = _pair_attention(q_ref[0, p], [k2], [[va], [vb]], [None]).astype(BF16)
            return carry
        lax.fori_loop(0, N_PAIRS, pair, 0)

    @pl.when(t >= nct)
    def _():
        first_row = jnp.clip(NA_Q_ROWS * (t - nct) - NA_ROWS_MAX // 2, 0, n_rows - NA_K_ROWS)
        start = pl.multiple_of(n_ctx + first_row * GRID_W, GRID_W)

        def pair(p, carry):
            kc = k_ref[0, p, pl.ds(0, n_ctx), :]
            kb = k_ref[0, p, pl.ds(start, NA_BAND), :]
            values = [[v_ref[0, 2 * p + half, pl.ds(0, n_ctx), :],
                       v_ref[0, 2 * p + half, pl.ds(start, NA_BAND), :]] for half in range(2)]
            bias = (bias_ref[0, 2 * p], bias_ref[0, 2 * p + 1])
            o_ref[0, p] = _pair_attention(
                q_ref[0, p], [kc, kb], values, [None, bias]).astype(BF16)
            return carry
        lax.fori_loop(0, N_PAIRS, pair, 0, unroll=4)


def _na_bias_table(rpb, n_rows):
    heads = rpb.shape[0]
    n_tiles = n_rows // NA_Q_ROWS
    kr_rows = min(NA_ROWS_MAX, n_rows)
    tt = jnp.arange(n_tiles)[:, None, None]
    qr = jnp.arange(NA_Q_ROWS)[None, :, None]
    kr = jnp.arange(NA_K_ROWS)[None, None, :]
    r = NA_Q_ROWS * tt + qr
    rp = jnp.clip(NA_Q_ROWS * tt - NA_ROWS_MAX // 2, 0, n_rows - NA_K_ROWS) + kr
    r0 = jnp.clip(r - kr_rows // 2, 0, n_rows - kr_rows)
    row_ok = (rp >= r0) & (rp < r0 + kr_rows)
    row_idx = jnp.clip(rp - r + NA_ROWS_MAX - 1, 0, 2 * NA_ROWS_MAX - 2)
    c = jnp.arange(GRID_W)[:, None]
    cp = jnp.arange(GRID_W)[None, :]
    cs = jnp.clip(c - NA_COLS // 2, 0, GRID_W - NA_COLS)
    col_ok = (cp >= cs) & (cp < cs + NA_COLS)
    col_idx = jnp.clip(cp - c, -(NA_COLS - 1), NA_COLS - 1) + NA_COLS - 1
    rows_sel = rpb[:, row_idx, :]
    onehot = (col_idx.reshape(-1)[None, :] == jnp.arange(2 * NA_COLS - 1)[:, None]).astype(F32)
    tab = jnp.dot(rows_sel.reshape(-1, 2 * NA_COLS - 1), onehot, precision=lax.Precision.HIGHEST)
    tab = tab.reshape(heads, n_tiles, NA_Q_ROWS, NA_K_ROWS, GRID_W, GRID_W)
    ok = row_ok[None, :, :, :, None, None] & col_ok[None, None, None, None, :, :]
    tab = jnp.where(ok, tab * LOG2E, MASK_VALUE)
    tab = tab.transpose(1, 0, 2, 4, 3, 5)
    return tab.reshape(n_tiles, heads, TOK_TILE, NA_BAND).astype(BF16)


def _na_attention(q, k, v, bias_tab, *, nct, n_ctx):
    bsz, _, n_tok, _ = q.shape
    nt = n_tok // TOK_TILE
    heads = bias_tab.shape[1]
    n_rows = (n_tok - n_ctx) // GRID_W
    tile = pl.BlockSpec((1, N_PAIRS, TOK_TILE, LANES), lambda b, t: (b, 0, t, 0))
    whole_k = pl.BlockSpec((1, N_PAIRS, n_tok, LANES), lambda b, t: (b, 0, 0, 0))
    whole_v = pl.BlockSpec((1, heads, n_tok, LANES), lambda b, t: (b, 0, 0, 0))
    bias_spec = pl.BlockSpec((1, heads, TOK_TILE, NA_BAND),
                             lambda b, t: (jnp.maximum(t - nct, 0), 0, 0, 0))
    return pl.pallas_call(
        functools.partial(_na_attn_kernel, nct=nct, n_ctx=n_ctx, n_rows=n_rows),
        grid=(bsz, nt),
        in_specs=[bias_spec, tile, whole_k, whole_v],
        out_specs=tile,
        out_shape=jax.ShapeDtypeStruct(q.shape, BF16),
        compiler_params=_params(2),
        name="neighbourhood_attention",
    )(bias_tab, q, k, v)


LRU_CHUNK = 256
LRU_PAD = 8


def _softplus(z):
    return jnp.maximum(z, 0.0) + jnp.log1p(jnp.exp(-jnp.abs(z)))


def _lru_kernel(rec_ref, gate_ref, cw_ref, cb_ref,
                fwa_ref, fwx_ref, bwa_ref, bwx_ref,
                fba_ref, fbx_ref, bba_ref, bbx_ref, flam_ref, blam_ref,
                o_ref, xp_ref, af_ref, bf_ref, ab_ref, bb_ref, *, n_ctx, n_tok):
    width = LRU_BLOCK
    xp_ref[0:LRU_PAD, :] = jnp.zeros((LRU_PAD, width), F32)
    xp_ref[LRU_PAD + n_tok:2 * LRU_PAD + n_tok, :] = jnp.zeros((LRU_PAD, width), F32)
    xp_ref[LRU_PAD:LRU_PAD + n_tok, :] = rec_ref[0]

    directions = (
        (fwa_ref, fwx_ref, fba_ref, fbx_ref, _softplus(-flam_ref[...]), af_ref, bf_ref, False),
        (bwa_ref, bwx_ref, bba_ref, bbx_ref, _softplus(-blam_ref[...]), ab_ref, bb_ref, True),
    )
    n_blk = LRU_CHUNK // SUBLANES
    sub = lax.broadcasted_iota(jnp.int32, (n_blk, SUBLANES, width), 1)

    def chunk(ci, carry):
        r0 = pl.multiple_of(ci * LRU_CHUNK, LRU_CHUNK)
        xw = xp_ref[pl.ds(r0, LRU_CHUNK + 2 * LRU_PAD), :]
        row = r0 + lax.broadcasted_iota(jnp.int32, (LRU_CHUNK, 1), 0)
        seg_lo = jnp.where(row < n_ctx, 0, n_ctx)
        seg_hi = jnp.where(row < n_ctx, n_ctx, n_tok)
        u = cb_ref[...]
        for j in range(LRU_CONV):
            off = j - LRU_CONV // 2
            src = row + off
            tap = xw[LRU_PAD + off:LRU_PAD + off + LRU_CHUNK, :]
            u = u + jnp.where((src >= seg_lo) & (src < seg_hi), tap, 0.0) * cw_ref[j:j + 1, :]
        ub = u.astype(BF16)
        for wa_ref, wx_ref, ba_ref, bx_ref, sp, a_ref, b_ref, reverse in directions:
            r = jax.nn.sigmoid(_dot(ub, wa_ref[0]) + ba_ref[...])
            i = jax.nn.sigmoid(_dot(ub, wx_ref[0]) + bx_ref[...])
            log_a = -LRU_C * r * sp
            a = jnp.exp(log_a)
            b = jnp.sqrt(-jnp.tanh(log_a) * (a * a + 1.0)) * (i * u)
            a3 = a.reshape(n_blk, SUBLANES, width)
            b3 = b.reshape(n_blk, SUBLANES, width)
            for dist in (1, 2, 4):
                if reverse:
                    shift, ok = SUBLANES - dist, sub < SUBLANES - dist
                else:
                    shift, ok = dist, sub >= dist
                a_prev = jnp.where(ok, pltpu.roll(a3, shift, 1), 1.0)
                b_prev = jnp.where(ok, pltpu.roll(b3, shift, 1), 0.0)
                b3 = a3 * b_prev + b3
                a3 = a3 * a_prev
            a_ref[pl.ds(r0, LRU_CHUNK), :] = a3.reshape(LRU_CHUNK, width)
            b_ref[pl.ds(r0, LRU_CHUNK), :] = b3.reshape(LRU_CHUNK, width)
        return carry

    lax.fori_loop(0, n_tok // LRU_CHUNK, chunk, 0)

    n_all = n_tok // SUBLANES
    n_cb = n_ctx // SUBLANES

    def step(i, carry):
        cf, cr = carry
        rf = pl.multiple_of(i * SUBLANES, SUBLANES)
        jb = jnp.where(i < n_cb, n_cb - 1 - i, n_all - 1 - (i - n_cb))
        rb = pl.multiple_of(jb * SUBLANES, SUBLANES)
        hf = af_ref[pl.ds(rf, SUBLANES), :] * cf + bf_ref[pl.ds(rf, SUBLANES), :]
        bf_ref[pl.ds(rf, SUBLANES), :] = hf
        hb = ab_ref[pl.ds(rb, SUBLANES), :] * cr + bb_ref[pl.ds(rb, SUBLANES), :]
        bb_ref[pl.ds(rb, SUBLANES), :] = hb
        return (jnp.broadcast_to(hf[SUBLANES - 1:SUBLANES, :], (SUBLANES, width)),
                jnp.broadcast_to(hb[0:1, :], (SUBLANES, width)))

    zero = jnp.zeros((SUBLANES, width), F32)
    lax.fori_loop(0, n_all, step, (zero, zero))

    y = (bf_ref[...] + bb_ref[...]) * gate_ref[0].astype(F32)
    for p in range(width // LANES):
        o_ref[0, p] = y[:, p * LANES:(p + 1) * LANES].astype(BF16)


def _lru_mixer(rec, gate, p, *, n_ctx):
    (conv_w, conv_b, f_wa, f_ba, f_wx, f_bx, f_lam, b_wa, b_ba, b_wx, b_bx, b_lam) = p
    bsz, n_tok, d = rec.shape
    n_blocks = d // LRU_BLOCK
    seq = pl.BlockSpec((1, n_tok, LRU_BLOCK), lambda b, n: (b, 0, n))
    vec = pl.BlockSpec((1, LRU_BLOCK), lambda b, n: (0, n))
    mat = pl.BlockSpec((1, LRU_BLOCK, LRU_BLOCK), lambda b, n: (n, 0, 0))
    row = lambda a: a.reshape(1, d)
    scratch = [pltpu.VMEM((n_tok + 2 * LRU_PAD, LRU_BLOCK), F32)]
    scratch += [pltpu.VMEM((n_tok, LRU_BLOCK), F32)] * 4
    return pl.pallas_call(
        functools.partial(_lru_kernel, n_ctx=n_ctx, n_tok=n_tok),
        grid=(bsz, n_blocks),
        in_specs=[seq, seq, pl.BlockSpec((LRU_CONV, LRU_BLOCK), lambda b, n: (0, n)), vec,
                  mat, mat, mat, mat, vec, vec, vec, vec, vec, vec],
        out_specs=pl.BlockSpec((1, LRU_BLOCK // LANES, n_tok, LANES), lambda b, n: (b, n, 0, 0)),
        out_shape=jax.ShapeDtypeStruct((bsz, d // LANES, n_tok, LANES), BF16),
        scratch_shapes=scratch,
        compiler_params=_params(2),
        name="rglru_scan",
    )(rec, gate, conv_w, row(conv_b),
      f_wa.astype(BF16), f_wx.astype(BF16), b_wa.astype(BF16), b_wx.astype(BF16),
      row(f_ba), row(f_bx), row(b_ba), row(b_bx), row(f_lam), row(b_lam))


FF_CHUNK = 1024


def _post_kernel(o_ref, x_ref, mod_ref, ng_ref, wo_ref, w1_ref, w2_ref, out_ref):
    m = mod_ref[0]
    o = jnp.concatenate([o_ref[0, p] for p in range(N_PAIRS)], axis=-1)
    x1 = x_ref[0] + m[2:3, :] * _dot(o, wo_ref[...])
    h = _norm_mod(x1, ng_ref[...], m[3:4, :], m[4:5, :]).astype(BF16)
    acc = jnp.zeros_like(x1)
    for c in range(0, D_FF, FF_CHUNK):
        a = jnp.maximum(_dot(h, w1_ref[:, c:c + FF_CHUNK]), 0.0)
        acc = acc + _dot((a * a).astype(BF16), w2_ref[c:c + FF_CHUNK, :])
    out_ref[0] = x1 + m[5:6, :] * acc


def _post(o, xs, modsel, norm_g, w_o, w1, w2, *, nct, skip_ctx):
    bsz, n_tok, d = xs.shape
    first = nct if skip_ctx else 0
    nt = n_tok // TOK_TILE - first
    return pl.pallas_call(
        _post_kernel,
        grid=(bsz, nt),
        in_specs=[pl.BlockSpec((1, N_PAIRS, TOK_TILE, LANES), lambda b, t: (b, 0, t + first, 0)),
                  pl.BlockSpec((1, TOK_TILE, d), lambda b, t: (b, t + first, 0)),
                  pl.BlockSpec((1, N_MOD, d),
                               lambda b, t: (2 * b + jnp.where(t + first >= nct, 1, 0), 0, 0)),
                  _resident((1, d)), _resident((d, d)), _resident((d, D_FF)), _resident((D_FF, d))],
        out_specs=pl.BlockSpec((1, TOK_TILE, d), lambda b, t: (b, t, 0)),
        out_shape=jax.ShapeDtypeStruct((bsz, nt * TOK_TILE, d), F32),
        compiler_params=_params(2),
        name="proj_residual_mlp",
    )(o, xs, modsel, norm_g.reshape(1, d), w_o, w1, w2)


def _rope_tables(n_ctx, n_lat):
    t = jnp.arange(n_lat)
    row = (t // GRID_W).astype(F32)
    col = (t % GRID_W).astype(F32)
    n_freq = HEAD_DIM // 4
    inv = ROPE_THETA ** (-jnp.arange(n_freq, dtype=F32) / n_freq)
    ang = jnp.concatenate([row[:, None] * inv, col[:, None] * inv], axis=-1)
    cos = jnp.repeat(jnp.cos(ang), 2, axis=-1)
    sin = jnp.repeat(jnp.sin(ang), 2, axis=-1) * jnp.tile(jnp.array([-1.0, 1.0], F32), HEAD_DIM // 2)
    cos = jnp.concatenate([jnp.ones((n_ctx, HEAD_DIM), F32), cos], axis=0)
    sin = jnp.concatenate([jnp.zeros((n_ctx, HEAD_DIM), F32), sin], axis=0)
    return jnp.tile(cos, (1, LANES // HEAD_DIM)), jnp.tile(sin, (1, LANES // HEAD_DIM))


def kernel(x, c, ctx, c_ctx, norm1_g, norm2_g, w_mod, b_mod, w_mlp1, w_mlp2, a_w_qkv, a_q_norm_g, a_k_norm_g, a_lambda_q1, a_lambda_k1, a_lambda_q2, a_lambda_k2, a_subln_g, a_w_o, b_w_qkv, b_q_norm_g, b_k_norm_g, b_rpb, b_w_o, c_w_in, c_conv_w, c_conv_b, c_fwd_w_a, c_fwd_b_a, c_fwd_w_x, c_fwd_b_x, c_fwd_lam, c_bwd_w_a, c_bwd_b_a, c_bwd_w_x, c_bwd_b_x, c_bwd_lam, c_w_o, d_w_qkv, d_q_norm_g, d_k_norm_g, d_w_o):
    bsz, n_lat, d = x.shape
    n_ctx = ctx.shape[1]
    depth = w_mod.shape[0]
    assert d == D_MODEL and depth == 4 and n_ctx % TOK_TILE == 0 and n_lat % TOK_TILE == 0
    assert (n_lat // GRID_W) % NA_Q_ROWS == 0 and n_lat // GRID_W >= NA_K_ROWS
    nct = n_ctx // TOK_TILE

    rows = -(-(bsz + 1) // SUBLANES) * SUBLANES
    c_all = jnp.concatenate([c, c_ctx[None, :], jnp.zeros((rows - bsz - 1, d), F32)], axis=0)
    mods = _modulation(c_all, w_mod, b_mod).reshape(depth, rows, N_MOD, d)
    mod_ctx = jnp.broadcast_to(mods[:, bsz][:, None], (depth, bsz, N_MOD, d))
    modsel = jnp.stack([mod_ctx, mods[:, :bsz]], axis=2).reshape(depth, 2 * bsz, N_MOD, d)

    xs = jnp.concatenate([ctx, x], axis=1)
    rope_tabs = _rope_tables(n_ctx, n_lat)
    w1 = w_mlp1.astype(BF16)
    w2 = w_mlp2.astype(BF16)

    q, k, v = _pre_attn(xs, modsel[0], norm1_g[0], a_w_qkv[0].astype(BF16), a_q_norm_g[0],
                        a_k_norm_g[0], rope_tabs, k_width=d, dup_kv=False, v_ones=False, nct=nct)
    lam_vecs = jnp.stack([a_lambda_q1[0], a_lambda_k1[0], a_lambda_q2[0], a_lambda_k2[0]])
    lam_init = 0.8 - 0.6 * math.exp(-0.3 * 0)
    o = _diff_attention(q, k, v, lam_vecs, a_subln_g[0], nct=nct, n_ctx=n_ctx, lam_init=lam_init)
    xs = _post(o, xs, modsel[0], norm2_g[0], a_w_o[0].astype(BF16), w1[0], w2[0],
               nct=nct, skip_ctx=False)

    q, k, v = _pre_attn(xs, modsel[1], norm1_g[1], b_w_qkv[0].astype(BF16), b_q_norm_g[0],
                        b_k_norm_g[0], None, k_width=d, dup_kv=False, v_ones=True, nct=nct)
    bias_tab = _na_bias_table(b_rpb[0], n_lat // GRID_W)
    o = _na_attention(q, k, v, bias_tab, nct=nct, n_ctx=n_ctx)
    xs = _post(o, xs, modsel[1], norm2_g[1], b_w_o[0].astype(BF16), w1[1], w2[1],
               nct=nct, skip_ctx=False)

    gate, rec = _pre_lru(xs, modsel[2], norm1_g[2], c_w_in[0].astype(BF16), nct=nct)
    lru_p = (c_conv_w[0], c_conv_b[0], c_fwd_w_a[0], c_fwd_b_a[0], c_fwd_w_x[0], c_fwd_b_x[0],
             c_fwd_lam[0], c_bwd_w_a[0], c_bwd_b_a[0], c_bwd_w_x[0], c_bwd_b_x[0], c_bwd_lam[0])
    o = _lru_mixer(rec, gate, lru_p, n_ctx=n_ctx)
    xs = _post(o, xs, modsel[2], norm2_g[2], c_w_o[0].astype(BF16), w1[2], w2[2],
               nct=nct, skip_ctx=False)

    q, k, v = _pre_attn(xs, modsel[3], norm1_g[3], d_w_qkv[0].astype(BF16), d_q_norm_g[0],
                        d_k_norm_g[0], rope_tabs, k_width=GQA_KV_HEADS * HEAD_DIM, dup_kv=True,
                        v_ones=True, nct=nct)
    o = _gqa_attention(q, k, v, nct=nct)
    return _post(o, xs, modsel[3], norm2_g[3], d_w_o[0].astype(BF16), w1[3], w2[3],
                 nct=nct, skip_ctx=True)
```

```python
import functools
import math

import jax
import jax.numpy as jnp
from jax import lax
from jax.experimental import pallas as pl
from jax.experimental.pallas import tpu as pltpu

F32 = jnp.float32
BF16 = jnp.bfloat16

D_MODEL = 1024
HEAD_DIM = 64
GRID_W = 64
N_MOD = 6
ATTN_SCALE = HEAD_DIM ** -0.5
LOG2E = math.log2(math.e)
ROPE_THETA = 10000.0
NORM_EPS = 1e-6
DA_HEADS = D_MODEL // (2 * HEAD_DIM)
NA_ROWS_MAX = 8
NA_COLS = 16
LRU_BLOCK = 256
LRU_CONV = 4
LRU_C = 8.0
GQA_KV_HEADS = 4
D_FF = 4 * D_MODEL

LANES = 128
SUBLANES = 8
MXU_DIM = 256
VMEM_LIMIT = 56 * 1024 * 1024

TOK_TILE = 256
N_PAIRS = D_MODEL // LANES
MASK_VALUE = -1e30
NA_Q_ROWS = TOK_TILE // GRID_W
NA_K_ROWS = 12
NA_BAND = NA_K_ROWS * GRID_W


def _params(n_axes):
    return pltpu.CompilerParams(
        dimension_semantics=("arbitrary",) * n_axes, vmem_limit_bytes=VMEM_LIMIT)


def _resident(shape):
    zeros = (0,) * len(shape)
    return pl.BlockSpec(shape, lambda *_: zeros, pipeline_mode=pl.Buffered(1))


def _dot(a, b):
    return jnp.dot(a, b, preferred_element_type=F32)


def _dot_nt(a, b):
    return lax.dot_general(a, b, (((1,), (1,)), ((), ())), preferred_element_type=F32)


def _norm_mod(x, gain, shift, scale):
    y = x * lax.rsqrt(jnp.mean(x * x, axis=-1, keepdims=True) + NORM_EPS) * gain
    return y * (1.0 + scale) + shift


def _group_mean_matrix():
    r = lax.broadcasted_iota(jnp.int32, (MXU_DIM, MXU_DIM), 0) // HEAD_DIM
    c = lax.broadcasted_iota(jnp.int32, (MXU_DIM, MXU_DIM), 1) // HEAD_DIM
    return jnp.where(r == c, 1.0 / HEAD_DIM, 0.0).astype(BF16)


def _head_rms(x, gain, mean_mat):
    sq = (x * x).astype(BF16)
    ms = jnp.concatenate(
        [_dot(sq[:, j:j + MXU_DIM], mean_mat) for j in range(0, x.shape[1], MXU_DIM)], axis=-1)
    return x * lax.rsqrt(ms + NORM_EPS) * gain


def _rope(x, cos, sin):
    width = x.shape[1]
    reps = width // LANES
    c = jnp.concatenate([cos] * reps, axis=-1)
    s = jnp.concatenate([sin] * reps, axis=-1)
    lane = lax.broadcasted_iota(jnp.int32, x.shape, 1)
    partner = jnp.where((lane & 1) == 0, pltpu.roll(x, width - 1, 1), pltpu.roll(x, 1, 1))
    return x * c + partner * s


def _mods_kernel(c_ref, w_ref, b_ref, o_ref):
    c = c_ref[...]
    a = (c * jax.nn.sigmoid(c)).astype(BF16)
    o_ref[0] = _dot(a, w_ref[0].astype(BF16)) + b_ref[0]


def _modulation(c_all, w_mod, b_mod):
    depth, d, n = w_mod.shape
    rows = c_all.shape[0]
    tn = n // 4
    return pl.pallas_call(
        _mods_kernel,
        grid=(depth, n // tn),
        in_specs=[pl.BlockSpec((rows, d), lambda l, j: (0, 0)),
                  pl.BlockSpec((1, d, tn), lambda l, j: (l, 0, j)),
                  pl.BlockSpec((1, 1, tn), lambda l, j: (l, 0, j))],
        out_specs=pl.BlockSpec((1, rows, tn), lambda l, j: (l, 0, j)),
        out_shape=jax.ShapeDtypeStruct((depth, rows, n), F32),
        compiler_params=_params(2),
        name="adaln_modulation",
    )(c_all, w_mod, b_mod.reshape(depth, 1, n))


def _stream_tile(refs, nct):
    if len(refs) == 1:
        return refs[0][0]
    return jnp.where(pl.program_id(1) < nct, refs[0][0], refs[1][0])


def _stream_specs(xs, nct):
    d = xs[0].shape[-1]
    if len(xs) == 1:
        return [pl.BlockSpec((1, TOK_TILE, d), lambda b, t: (b, t, 0))]
    return [pl.BlockSpec((1, TOK_TILE, d), lambda b, t: (b, jnp.minimum(t, nct - 1), 0)),
            pl.BlockSpec((1, TOK_TILE, d), lambda b, t: (b, jnp.maximum(t - nct, 0), 0))]


def _pre_attn_kernel(*refs, n_streams, nct, k_width, rope, dup_kv):
    x_refs, refs = refs[:n_streams], refs[n_streams:]
    if rope:
        (mod_ref, ng_ref, w_ref, qg_ref, kg_ref, cos_ref, sin_ref, q_ref, k_ref, v_ref) = refs
    else:
        mod_ref, ng_ref, w_ref, qg_ref, kg_ref, q_ref, k_ref, v_ref = refs
    m = mod_ref[0]
    h = _norm_mod(_stream_tile(x_refs, nct), ng_ref[...], m[0:1, :], m[1:2, :]).astype(BF16)
    mean_mat = _group_mean_matrix()
    d = D_MODEL
    q = _head_rms(_dot(h, w_ref[:, 0:d]), qg_ref[...], mean_mat)
    k = _head_rms(_dot(h, w_ref[:, d:d + k_width]), kg_ref[...], mean_mat)
    v = _dot(h, w_ref[:, d + k_width:d + 2 * k_width])
    if rope:
        q = _rope(q, cos_ref[...], sin_ref[...])
        k = _rope(k, cos_ref[...], sin_ref[...])
    q = q * (ATTN_SCALE * LOG2E)
    for p in range(N_PAIRS):
        q_ref[0, p] = q[:, p * LANES:(p + 1) * LANES].astype(BF16)
    if dup_kv:
        for g in range(k_width // HEAD_DIM):
            kh = k[:, g * HEAD_DIM:(g + 1) * HEAD_DIM]
            k_ref[0, g] = jnp.concatenate([kh, kh], axis=-1).astype(BF16)
    else:
        for p in range(k_width // LANES):
            k_ref[0, p] = k[:, p * LANES:(p + 1) * LANES].astype(BF16)
    fill = jnp.ones((v.shape[0], LANES), F32)
    if dup_kv:
        for g in range(k_width // HEAD_DIM):
            vh = v[:, g * HEAD_DIM:(g + 1) * HEAD_DIM]
            v_ref[0, g] = jnp.concatenate([vh, vh, fill], axis=-1).astype(BF16)
    else:
        for p in range(k_width // LANES):
            v_ref[0, p] = jnp.concatenate(
                [v[:, p * LANES:(p + 1) * LANES], fill], axis=-1).astype(BF16)


def _mod_spec(nct):
    return pl.BlockSpec((1, N_MOD, D_MODEL),
                        lambda b, t: (2 * b + jnp.where(t >= nct, 1, 0), 0, 0))


def _pre_attn(xs, modsel, norm_g, w_qkv, q_gain, k_gain, rope_tabs, *, k_width, dup_kv, nct):
    bsz, _, d = xs[0].shape
    n_tok = sum(a.shape[1] for a in xs)
    nt = n_tok // TOK_TILE
    n_w = w_qkv.shape[1]
    k_slots = k_width // HEAD_DIM if dup_kv else k_width // LANES
    rope = rope_tabs is not None
    in_specs = _stream_specs(xs, nct) + [
        _mod_spec(nct),
        _resident((1, d)),
        _resident((d, n_w)),
        _resident((1, d)),
        _resident((1, k_width))]
    args = list(xs) + [modsel, norm_g.reshape(1, d), w_qkv,
                       jnp.tile(q_gain, d // HEAD_DIM).reshape(1, d),
                       jnp.tile(k_gain, k_width // HEAD_DIM).reshape(1, k_width)]
    if rope:
        in_specs += [pl.BlockSpec((TOK_TILE, LANES), lambda b, t: (t, 0))] * 2
        args += list(rope_tabs)
    qo = jax.ShapeDtypeStruct((bsz, N_PAIRS, n_tok, LANES), BF16)
    ko = jax.ShapeDtypeStruct((bsz, k_slots, n_tok, LANES), BF16)
    vo = jax.ShapeDtypeStruct((bsz, k_slots, n_tok, MXU_DIM), BF16)
    return pl.pallas_call(
        functools.partial(_pre_attn_kernel, n_streams=len(xs), nct=nct, k_width=k_width,
                          rope=rope, dup_kv=dup_kv),
        grid=(bsz, nt),
        in_specs=in_specs,
        out_specs=[pl.BlockSpec((1, N_PAIRS, TOK_TILE, LANES), lambda b, t: (b, 0, t, 0)),
                   pl.BlockSpec((1, k_slots, TOK_TILE, LANES), lambda b, t: (b, 0, t, 0)),
                   pl.BlockSpec((1, k_slots, TOK_TILE, MXU_DIM), lambda b, t: (b, 0, t, 0))],
        out_shape=[qo, ko, vo],
        compiler_params=_params(2),
        name="norm_mod_qkv",
    )(*args)


def _pre_lru_kernel(x_ref, mod_ref, ng_ref, w_ref, gate_ref, rec_ref):
    m = mod_ref[0]
    h = _norm_mod(x_ref[0], ng_ref[...], m[0:1, :], m[1:2, :]).astype(BF16)
    d = D_MODEL
    gate_ref[0] = jax.nn.gelu(_dot(h, w_ref[:, 0:d])).astype(BF16)
    rec_ref[0] = _dot(h, w_ref[:, d:2 * d])


def _pre_lru(xs, modsel, norm_g, w_in, *, nct):
    bsz, n_tok, d = xs.shape
    nt = n_tok // TOK_TILE
    tok_spec = pl.BlockSpec((1, TOK_TILE, d), lambda b, t: (b, t, 0))
    return pl.pallas_call(
        _pre_lru_kernel,
        grid=(bsz, nt),
        in_specs=[tok_spec, _mod_spec(nct), _resident((1, d)), _resident((d, 2 * d))],
        out_specs=[tok_spec, tok_spec],
        out_shape=[jax.ShapeDtypeStruct((bsz, n_tok, d), BF16),
                   jax.ShapeDtypeStruct((bsz, n_tok, d), F32)],
        compiler_params=_params(2),
        name="norm_mod_lru_in",
    )(xs, modsel, norm_g.reshape(1, d), w_in)


def _split_pair(q2):
    low = lax.broadcasted_iota(jnp.int32, q2.shape, 1) < HEAD_DIM
    zero = jnp.zeros_like(q2)
    return jnp.where(low, q2, zero), jnp.where(low, zero, q2)


def _exp2_parts(parts):
    m = functools.reduce(jnp.maximum, [jnp.max(s, axis=-1, keepdims=True) for s in parts])
    return [jnp.exp2(s - m) for s in parts]


def _pipeline3(n_units, scores, softmax, values):
    assert n_units % 2 == 0 and n_units >= 4
    scores(0, 0)
    scores(1, 1)
    softmax(0, 0)
    for u in range(2, n_units, 2):
        scores(u, 0)
        softmax(u - 1, 1)
        values(u - 2, 0)
        scores(u + 1, 1)
        softmax(u, 0)
        values(u - 1, 1)
    softmax(n_units - 1, 1)
    values(n_units - 2, 0)
    values(n_units - 1, 1)


def _sub_layer_norm(o, gain, lam_init):
    o = o * lax.rsqrt(jnp.mean(o * o, axis=-1, keepdims=True) + NORM_EPS) * gain
    return (o * (1.0 - lam_init)).astype(BF16)


def _scores_stage(s_buf, m_buf, q2, key_parts, bias_parts):
    for half, qh in enumerate(_split_pair(q2)):
        m, col = None, 0
        for kp, bp in zip(key_parts, bias_parts):
            s = _dot_nt(qh, kp)
            if bp is not None:
                s = s + bp[half].astype(F32)
            s_buf[half, :, col:col + kp.shape[0]] = s
            part_max = jnp.max(s, axis=-1, keepdims=True)
            m = part_max if m is None else jnp.maximum(m, part_max)
            col += kp.shape[0]
        m_buf[half] = jnp.broadcast_to(m, m_buf.shape[1:])


def _softmax_stage(s_buf, m_buf, p_buf):
    for half in range(2):
        p_buf[half] = jnp.exp2(s_buf[half] - m_buf[half][:, 0:1]).astype(BF16)


def _values_stage(p_buf, value_parts):
    outs = []
    for half in range(2):
        acc, col = None, 0
        for vp in value_parts:
            part = _dot(p_buf[half, :, col:col + vp.shape[0]], vp)
            acc = part if acc is None else acc + part
            col += vp.shape[0]
        outs.append(acc[:, :LANES] / acc[:, LANES:])
    return outs


def _direct_pair(q2, k2, v2):
    outs = []
    for qh in _split_pair(q2):
        (e,) = _exp2_parts([_dot_nt(qh, k2)])
        acc = _dot(e.astype(BF16), v2)
        outs.append(acc[:, :LANES] / acc[:, LANES:])
    return outs


def _attn_scratch(n_cols):
    scratch = [pltpu.VMEM((2, TOK_TILE, n_cols), F32)] * 2
    scratch += [pltpu.VMEM((2, TOK_TILE, LANES), F32)] * 2
    scratch += [pltpu.VMEM((2, TOK_TILE, n_cols), BF16)] * 2
    return scratch


def _diff_attn_kernel(lam_ref, sg_ref, q_ref, k_ref, v_ref, o_ref, s0, s1, m0, m1, p0, p1,
                      *, nct, n_ctx, lam_init):
    t = pl.program_id(1)
    lv = lam_ref[...]
    lam = (jnp.exp(jnp.sum(lv[0:1, :] * lv[1:2, :], axis=-1, keepdims=True))
           - jnp.exp(jnp.sum(lv[2:3, :] * lv[3:4, :], axis=-1, keepdims=True)) + lam_init)

    def finish(h, outs):
        o_ref[0, h] = _sub_layer_norm(outs[0] - lam * outs[1], sg_ref[...], lam_init)

    @pl.when(t < nct)
    def _():
        def head(h, carry):
            finish(h, _direct_pair(q_ref[0, h], k_ref[0, h, pl.ds(0, n_ctx), :],
                                   v_ref[0, h, pl.ds(0, n_ctx), :]))
            return carry
        lax.fori_loop(0, DA_HEADS, head, 0)

    @pl.when(t >= nct)
    def _():
        s_bufs, m_bufs, p_bufs = (s0, s1), (m0, m1), (p0, p1)
        _pipeline3(
            DA_HEADS,
            lambda h, slot: _scores_stage(s_bufs[slot], m_bufs[slot], q_ref[0, h],
                                          [k_ref[0, h]], [None]),
            lambda h, slot: _softmax_stage(s_bufs[slot], m_bufs[slot], p_bufs[slot]),
            lambda h, slot: finish(h, _values_stage(p_bufs[slot], [v_ref[0, h]])))


def _diff_attention(q, k, v, lam_vecs, subln_g, *, nct, n_ctx, lam_init):
    bsz, _, n_tok, _ = q.shape
    nt = n_tok // TOK_TILE
    tile = pl.BlockSpec((1, N_PAIRS, TOK_TILE, LANES), lambda b, t: (b, 0, t, 0))
    whole_k = pl.BlockSpec((1, DA_HEADS, n_tok, LANES), lambda b, t: (b, 0, 0, 0))
    whole_v = pl.BlockSpec((1, DA_HEADS, n_tok, MXU_DIM), lambda b, t: (b, 0, 0, 0))
    scratch = _attn_scratch(n_tok)
    return pl.pallas_call(
        functools.partial(_diff_attn_kernel, nct=nct, n_ctx=n_ctx, lam_init=lam_init),
        grid=(bsz, nt),
        in_specs=[_resident((4, HEAD_DIM)), _resident((1, LANES)), tile, whole_k, whole_v],
        out_specs=tile,
        out_shape=jax.ShapeDtypeStruct(q.shape, BF16),
        scratch_shapes=scratch,
        compiler_params=_params(2),
        name="diff_attention",
    )(lam_vecs, subln_g.reshape(1, LANES), q, k, v)


def _pair_output(outs):
    low = lax.broadcasted_iota(jnp.int32, outs[0].shape, 1) < HEAD_DIM
    return jnp.where(low, outs[0], outs[1]).astype(BF16)


def _gqa_attn_kernel(q_ref, k_ref, v_ref, o_ref, s0, s1, m0, m1, p0, p1):
    s_bufs, m_bufs, p_bufs = (s0, s1), (m0, m1), (p0, p1)

    def values(p, slot):
        o_ref[0, p] = _pair_output(_values_stage(p_bufs[slot], [v_ref[0, p // 2]]))

    _pipeline3(
        N_PAIRS,
        lambda p, slot: _scores_stage(s_bufs[slot], m_bufs[slot], q_ref[0, p],
                                      [k_ref[0, p // 2]], [None]),
        lambda p, slot: _softmax_stage(s_bufs[slot], m_bufs[slot], p_bufs[slot]),
        values)


def _gqa_attention(q, k, v, *, nct):
    bsz, _, n_tok, _ = q.shape
    nt = n_tok // TOK_TILE - nct
    tile = pl.BlockSpec((1, N_PAIRS, TOK_TILE, LANES), lambda b, t: (b, 0, t + nct, 0))
    whole_k = pl.BlockSpec((1, GQA_KV_HEADS, n_tok, LANES), lambda b, t: (b, 0, 0, 0))
    whole_v = pl.BlockSpec((1, GQA_KV_HEADS, n_tok, MXU_DIM), lambda b, t: (b, 0, 0, 0))
    scratch = _attn_scratch(n_tok)
    return pl.pallas_call(
        _gqa_attn_kernel,
        grid=(bsz, nt),
        in_specs=[tile, whole_k, whole_v],
        out_specs=tile,
        out_shape=jax.ShapeDtypeStruct(q.shape, BF16),
        scratch_shapes=scratch,
        compiler_params=_params(2),
        name="gqa_attention",
    )(q, k, v)


def _na_attn_kernel(bias_ref, q_ref, k_ref, v_ref, o_ref, s0, s1, m0, m1, p0, p1,
                    *, nct, n_ctx, n_rows):
    t = pl.program_id(1)

    @pl.when(t < nct)
    def _():
        def pair(p, carry):
            o_ref[0, p] = _pair_output(_direct_pair(
                q_ref[0, p], k_ref[0, p, pl.ds(0, n_ctx), :], v_ref[0, p, pl.ds(0, n_ctx), :]))
            return carry
        lax.fori_loop(0, N_PAIRS, pair, 0)

    @pl.when(t >= nct)
    def _():
        first_row = jnp.clip(NA_Q_ROWS * (t - nct) - NA_ROWS_MAX // 2, 0, n_rows - NA_K_ROWS)
        start = pl.multiple_of(n_ctx + first_row * GRID_W, GRID_W)
        s_bufs, m_bufs, p_bufs = (s0, s1), (m0, m1), (p0, p1)

        def scores(p, slot):
            keys = [k_ref[0, p, pl.ds(0, n_ctx), :], k_ref[0, p, pl.ds(start, NA_BAND), :]]
            bias = (bias_ref[0, 2 * p], bias_ref[0, 2 * p + 1])
            _scores_stage(s_bufs[slot], m_bufs[slot], q_ref[0, p], keys, [None, bias])

        def values(p, slot):
            vals = [v_ref[0, p, pl.ds(0, n_ctx), :], v_ref[0, p, pl.ds(start, NA_BAND), :]]
            o_ref[0, p] = _pair_output(_values_stage(p_bufs[slot], vals))

        _pipeline3(
            N_PAIRS, scores,
            lambda p, slot: _softmax_stage(s_bufs[slot], m_bufs[slot], p_bufs[slot]),
            values)


def _na_bias_table(rpb, n_rows):
    heads = rpb.shape[0]
    kr_rows = min(NA_ROWS_MAX, n_rows)
    variants = (0, 1, n_rows // NA_Q_ROWS - 1)
    n_tiles = len(variants)
    tt = jnp.array(variants)[:, None, None]
    qr = jnp.arange(NA_Q_ROWS)[None, :, None]
    kr = jnp.arange(NA_K_ROWS)[None, None, :]
    r = NA_Q_ROWS * tt + qr
    rp = jnp.clip(NA_Q_ROWS * tt - NA_ROWS_MAX // 2, 0, n_rows - NA_K_ROWS) + kr
    r0 = jnp.clip(r - kr_rows // 2, 0, n_rows - kr_rows)
    row_ok = (rp >= r0) & (rp < r0 + kr_rows)
    row_idx = jnp.clip(rp - r + NA_ROWS_MAX - 1, 0, 2 * NA_ROWS_MAX - 2)
    c = jnp.arange(GRID_W)[:, None]
    cp = jnp.arange(GRID_W)[None, :]
    cs = jnp.clip(c - NA_COLS // 2, 0, GRID_W - NA_COLS)
    col_ok = (cp >= cs) & (cp < cs + NA_COLS)
    col_idx = jnp.clip(cp - c, -(NA_COLS - 1), NA_COLS - 1) + NA_COLS - 1
    rows_sel = rpb[:, row_idx, :]
    onehot = (col_idx.reshape(-1)[None, :] == jnp.arange(2 * NA_COLS - 1)[:, None]).astype(F32)
    tab = jnp.dot(rows_sel.reshape(-1, 2 * NA_COLS - 1), onehot, precision=lax.Precision.HIGHEST)
    tab = tab.reshape(heads, n_tiles, NA_Q_ROWS, NA_K_ROWS, GRID_W, GRID_W)
    ok = row_ok[None, :, :, :, None, None] & col_ok[None, None, None, None, :, :]
    tab = jnp.where(ok, tab * LOG2E, MASK_VALUE)
    tab = tab.transpose(1, 0, 2, 4, 3, 5)
    return tab.reshape(n_tiles, heads, TOK_TILE, NA_BAND).astype(BF16)


def _na_attention(q, k, v, bias_tab, *, nct, n_ctx):
    bsz, _, n_tok, _ = q.shape
    nt = n_tok // TOK_TILE
    heads = bias_tab.shape[1]
    n_rows = (n_tok - n_ctx) // GRID_W
    tile = pl.BlockSpec((1, N_PAIRS, TOK_TILE, LANES), lambda b, t: (b, 0, t, 0))
    whole_k = pl.BlockSpec((1, N_PAIRS, n_tok, LANES), lambda b, t: (b, 0, 0, 0))
    whole_v = pl.BlockSpec((1, N_PAIRS, n_tok, MXU_DIM), lambda b, t: (b, 0, 0, 0))
    last = nt - nct - 1

    def bias_index(b, t):
        tile = t - nct
        return (jnp.where(tile <= 0, 0, jnp.where(tile == last, 2, 1)), 0, 0, 0)

    bias_spec = pl.BlockSpec((1, heads, TOK_TILE, NA_BAND), bias_index)
    scratch = _attn_scratch(n_ctx + NA_BAND)
    return pl.pallas_call(
        functools.partial(_na_attn_kernel, nct=nct, n_ctx=n_ctx, n_rows=n_rows),
        grid=(bsz, nt),
        in_specs=[bias_spec, tile, whole_k, whole_v],
        out_specs=tile,
        out_shape=jax.ShapeDtypeStruct(q.shape, BF16),
        scratch_shapes=scratch,
        compiler_params=_params(2),
        name="neighbourhood_attention",
    )(bias_tab, q, k, v)


LRU_CHUNK = 256
LRU_PAD = 8


def _softplus(z):
    return jnp.maximum(z, 0.0) + jnp.log1p(jnp.exp(-jnp.abs(z)))


def _sigmoid(z):
    return 0.5 * jnp.tanh(0.5 * z) + 0.5


def _sqrt_nonneg(z):
    return z * lax.rsqrt(jnp.maximum(z, float(jnp.finfo(F32).tiny)))


def _lru_kernel(rec_ref, gate_ref, cw_ref, cb_ref,
                fwa_ref, fwx_ref, bwa_ref, bwx_ref,
                fba_ref, fbx_ref, bba_ref, bbx_ref, flam_ref, blam_ref,
                o_ref, xp_ref, af_ref, bf_ref, ab_ref, bb_ref, *, n_ctx, n_tok):
    width = LRU_BLOCK
    xp_ref[0:LRU_PAD, :] = jnp.zeros((LRU_PAD, width), F32)
    xp_ref[LRU_PAD + n_tok:2 * LRU_PAD + n_tok, :] = jnp.zeros((LRU_PAD, width), F32)
    xp_ref[LRU_PAD:LRU_PAD + n_tok, :] = rec_ref[0]

    directions = (
        (fwa_ref, fwx_ref, fba_ref, fbx_ref, _softplus(-flam_ref[...]), af_ref, bf_ref, False),
        (bwa_ref, bwx_ref, bba_ref, bbx_ref, _softplus(-blam_ref[...]), ab_ref, bb_ref, True),
    )
    n_blk = LRU_CHUNK // SUBLANES
    sub = lax.broadcasted_iota(jnp.int32, (n_blk, SUBLANES, width), 1)

    def chunk(ci, carry):
        r0 = pl.multiple_of(ci * LRU_CHUNK, LRU_CHUNK)
        xw = xp_ref[pl.ds(r0, LRU_CHUNK + 2 * LRU_PAD), :]
        wrow = lax.broadcasted_iota(jnp.int32, (LRU_CHUNK + 2 * LRU_PAD, 1), 0)
        crosses = (((wrow < LRU_PAD) & (r0 == n_ctx))
                   | ((wrow >= LRU_PAD + LRU_CHUNK) & (r0 + LRU_CHUNK == n_ctx)))
        xw = jnp.where(crosses, 0.0, xw)
        u = cb_ref[...]
        for j in range(LRU_CONV):
            off = LRU_PAD + j - LRU_CONV // 2
            u = u + xw[off:off + LRU_CHUNK, :] * cw_ref[j:j + 1, :]
        ub = u.astype(BF16)
        for wa_ref, wx_ref, ba_ref, bx_ref, sp, a_ref, b_ref, reverse in directions:
            r = _sigmoid(_dot(ub, wa_ref[0]) + ba_ref[...])
            i = _sigmoid(_dot(ub, wx_ref[0]) + bx_ref[...])
            log_a = -LRU_C * r * sp
            a = jnp.exp(log_a)
            b = _sqrt_nonneg(-jnp.tanh(log_a) * (a * a + 1.0)) * (i * u)
            a3 = a.reshape(n_blk, SUBLANES, width)
            b3 = b.reshape(n_blk, SUBLANES, width)
            for dist in (1, 2, 4):
                if reverse:
                    shift, ok = SUBLANES - dist, sub < SUBLANES - dist
                else:
                    shift, ok = dist, sub >= dist
                a_prev = jnp.where(ok, pltpu.roll(a3, shift, 1), 1.0)
                b_prev = jnp.where(ok, pltpu.roll(b3, shift, 1), 0.0)
                b3 = a3 * b_prev + b3
                a3 = a3 * a_prev
            a_ref[pl.ds(r0, LRU_CHUNK), :] = a3.reshape(LRU_CHUNK, width)
            b_ref[pl.ds(r0, LRU_CHUNK), :] = b3.reshape(LRU_CHUNK, width)
        return carry

    lax.fori_loop(0, n_tok // LRU_CHUNK, chunk, 0)

    n_all = n_tok // SUBLANES
    n_cb = n_ctx // SUBLANES

    def step(i, carry):
        cf, cr = carry
        rf = pl.multiple_of(i * SUBLANES, SUBLANES)
        jb = jnp.where(i < n_cb, n_cb - 1 - i, n_all - 1 - (i - n_cb))
        rb = pl.multiple_of(jb * SUBLANES, SUBLANES)
        hf = af_ref[pl.ds(rf, SUBLANES), :] * cf + bf_ref[pl.ds(rf, SUBLANES), :]
        bf_ref[pl.ds(rf, SUBLANES), :] = hf
        hb = ab_ref[pl.ds(rb, SUBLANES), :] * cr + bb_ref[pl.ds(rb, SUBLANES), :]
        bb_ref[pl.ds(rb, SUBLANES), :] = hb
        return (jnp.broadcast_to(hf[SUBLANES - 1:SUBLANES, :], (SUBLANES, width)),
                jnp.broadcast_to(hb[0:1, :], (SUBLANES, width)))

    zero = jnp.zeros((SUBLANES, width), F32)
    lax.fori_loop(0, n_all, step, (zero, zero))

    y = (bf_ref[...] + bb_ref[...]) * gate_ref[0].astype(F32)
    for p in range(width // LANES):
        o_ref[0, p] = y[:, p * LANES:(p + 1) * LANES].astype(BF16)


def _lru_mixer(rec, gate, p, *, n_ctx):
    (conv_w, conv_b, f_wa, f_ba, f_wx, f_bx, f_lam, b_wa, b_ba, b_wx, b_bx, b_lam) = p
    bsz, n_tok, d = rec.shape
    n_blocks = d // LRU_BLOCK
    seq = pl.BlockSpec((1, n_tok, LRU_BLOCK), lambda b, n: (b, 0, n))
    vec = pl.BlockSpec((1, LRU_BLOCK), lambda b, n: (0, n))
    mat = pl.BlockSpec((1, LRU_BLOCK, LRU_BLOCK), lambda b, n: (n, 0, 0))
    row = lambda a: a.reshape(1, d)
    scratch = [pltpu.VMEM((n_tok + 2 * LRU_PAD, LRU_BLOCK), F32)]
    scratch += [pltpu.VMEM((n_tok, LRU_BLOCK), F32)] * 4
    return pl.pallas_call(
        functools.partial(_lru_kernel, n_ctx=n_ctx, n_tok=n_tok),
        grid=(bsz, n_blocks),
        in_specs=[seq, seq, pl.BlockSpec((LRU_CONV, LRU_BLOCK), lambda b, n: (0, n)), vec,
                  mat, mat, mat, mat, vec, vec, vec, vec, vec, vec],
        out_specs=pl.BlockSpec((1, LRU_BLOCK // LANES, n_tok, LANES), lambda b, n: (b, n, 0, 0)),
        out_shape=jax.ShapeDtypeStruct((bsz, d // LANES, n_tok, LANES), BF16),
        scratch_shapes=scratch,
        compiler_params=_params(2),
        name="rglru_scan",
    )(rec, gate, conv_w, row(conv_b),
      f_wa.astype(BF16), f_wx.astype(BF16), b_wa.astype(BF16), b_wx.astype(BF16),
      row(f_ba), row(f_bx), row(b_ba), row(b_bx), row(f_lam), row(b_lam))


FF_CHUNK = 1024


def _post_kernel(o_ref, *refs, n_streams, nct):
    x_refs = refs[:n_streams]
    mod_ref, ng_ref, wo_ref, w1_ref, w2_ref, out_ref = refs[n_streams:]
    m = mod_ref[0]
    o = jnp.concatenate([o_ref[0, p] for p in range(N_PAIRS)], axis=-1)
    x1 = _stream_tile(x_refs, nct) + m[2:3, :] * _dot(o, wo_ref[...])
    h = _norm_mod(x1, ng_ref[...], m[3:4, :], m[4:5, :]).astype(BF16)
    acc = jnp.zeros_like(x1)
    for c in range(0, D_FF, FF_CHUNK):
        a = jnp.maximum(_dot(h, w1_ref[:, c:c + FF_CHUNK]), 0.0)
        acc = acc + _dot((a * a).astype(BF16), w2_ref[c:c + FF_CHUNK, :])
    out_ref[0] = x1 + m[5:6, :] * acc


def _post(o, xs, modsel, norm_g, w_o, w1, w2, *, nct, skip_ctx):
    bsz, _, d = xs[0].shape
    n_tok = sum(a.shape[1] for a in xs)
    first = nct if skip_ctx else 0
    nt = n_tok // TOK_TILE - first
    if len(xs) == 1:
        x_specs = [pl.BlockSpec((1, TOK_TILE, d), lambda b, t: (b, t + first, 0))]
    else:
        assert not skip_ctx
        x_specs = _stream_specs(xs, nct)
    return pl.pallas_call(
        functools.partial(_post_kernel, n_streams=len(xs), nct=nct),
        grid=(bsz, nt),
        in_specs=[pl.BlockSpec((1, N_PAIRS, TOK_TILE, LANES), lambda b, t: (b, 0, t + first, 0))]
        + x_specs
        + [pl.BlockSpec((1, N_MOD, d),
                        lambda b, t: (2 * b + jnp.where(t + first >= nct, 1, 0), 0, 0)),
           _resident((1, d)), _resident((d, d)), _resident((d, D_FF)), _resident((D_FF, d))],
        out_specs=pl.BlockSpec((1, TOK_TILE, d), lambda b, t: (b, t, 0)),
        out_shape=jax.ShapeDtypeStruct((bsz, nt * TOK_TILE, d), F32),
        compiler_params=_params(2),
        name="proj_residual_mlp",
    )(o, *xs, modsel, norm_g.reshape(1, d), w_o, w1, w2)


def _rope_tables(n_ctx, n_lat):
    t = jnp.arange(n_lat)
    row = (t // GRID_W).astype(F32)
    col = (t % GRID_W).astype(F32)
    n_freq = HEAD_DIM // 4
    inv = ROPE_THETA ** (-jnp.arange(n_freq, dtype=F32) / n_freq)
    ang = jnp.concatenate([row[:, None] * inv, col[:, None] * inv], axis=-1)
    cos = jnp.repeat(jnp.cos(ang), 2, axis=-1)
    sin = jnp.repeat(jnp.sin(ang), 2, axis=-1) * jnp.tile(jnp.array([-1.0, 1.0], F32), HEAD_DIM // 2)
    cos = jnp.concatenate([jnp.ones((n_ctx, HEAD_DIM), F32), cos], axis=0)
    sin = jnp.concatenate([jnp.zeros((n_ctx, HEAD_DIM), F32), sin], axis=0)
    return jnp.tile(cos, (1, LANES // HEAD_DIM)), jnp.tile(sin, (1, LANES // HEAD_DIM))


def kernel(x, c, ctx, c_ctx, norm1_g, norm2_g, w_mod, b_mod, w_mlp1, w_mlp2, a_w_qkv, a_q_norm_g, a_k_norm_g, a_lambda_q1, a_lambda_k1, a_lambda_q2, a_lambda_k2, a_subln_g, a_w_o, b_w_qkv, b_q_norm_g, b_k_norm_g, b_rpb, b_w_o, c_w_in, c_conv_w, c_conv_b, c_fwd_w_a, c_fwd_b_a, c_fwd_w_x, c_fwd_b_x, c_fwd_lam, c_bwd_w_a, c_bwd_b_a, c_bwd_w_x, c_bwd_b_x, c_bwd_lam, c_w_o, d_w_qkv, d_q_norm_g, d_k_norm_g, d_w_o):
    bsz, n_lat, d = x.shape
    n_ctx = ctx.shape[1]
    depth = w_mod.shape[0]
    assert d == D_MODEL and depth == 4 and n_ctx % TOK_TILE == 0 and n_lat % TOK_TILE == 0
    assert (n_lat // GRID_W) % NA_Q_ROWS == 0 and n_lat // GRID_W >= NA_K_ROWS
    nct = n_ctx // TOK_TILE

    rows = -(-(bsz + 1) // SUBLANES) * SUBLANES
    c_all = jnp.concatenate([c, c_ctx[None, :], jnp.zeros((rows - bsz - 1, d), F32)], axis=0)
    mods = _modulation(c_all, w_mod, b_mod).reshape(depth, rows, N_MOD, d)
    mod_ctx = jnp.broadcast_to(mods[:, bsz][:, None], (depth, bsz, N_MOD, d))
    modsel = jnp.stack([mod_ctx, mods[:, :bsz]], axis=2).reshape(depth, 2 * bsz, N_MOD, d)

    rope_tabs = _rope_tables(n_ctx, n_lat)
    w1 = w_mlp1.astype(BF16)
    w2 = w_mlp2.astype(BF16)

    xs = (ctx, x)
    q, k, v = _pre_attn(xs, modsel[0], norm1_g[0], a_w_qkv[0].astype(BF16), a_q_norm_g[0],
                        a_k_norm_g[0], rope_tabs, k_width=d, dup_kv=False, nct=nct)
    lam_vecs = jnp.stack([a_lambda_q1[0], a_lambda_k1[0], a_lambda_q2[0], a_lambda_k2[0]])
    lam_init = 0.8 - 0.6 * math.exp(-0.3 * 0)
    o = _diff_attention(q, k, v, lam_vecs, a_subln_g[0], nct=nct, n_ctx=n_ctx, lam_init=lam_init)
    xs = (_post(o, xs, modsel[0], norm2_g[0], a_w_o[0].astype(BF16), w1[0], w2[0],
                nct=nct, skip_ctx=False),)

    q, k, v = _pre_attn(xs, modsel[1], norm1_g[1], b_w_qkv[0].astype(BF16), b_q_norm_g[0],
                        b_k_norm_g[0], None, k_width=d, dup_kv=False, nct=nct)
    bias_tab = _na_bias_table(b_rpb[0], n_lat // GRID_W)
    o = _na_attention(q, k, v, bias_tab, nct=nct, n_ctx=n_ctx)
    xs = (_post(o, xs, modsel[1], norm2_g[1], b_w_o[0].astype(BF16), w1[1], w2[1],
                nct=nct, skip_ctx=False),)

    gate, rec = _pre_lru(xs[0], modsel[2], norm1_g[2], c_w_in[0].astype(BF16), nct=nct)
    lru_p = (c_conv_w[0], c_conv_b[0], c_fwd_w_a[0], c_fwd_b_a[0], c_fwd_w_x[0], c_fwd_b_x[0],
             c_fwd_lam[0], c_bwd_w_a[0], c_bwd_b_a[0], c_bwd_w_x[0], c_bwd_b_x[0], c_bwd_lam[0])
    o = _lru_mixer(rec, gate, lru_p, n_ctx=n_ctx)
    xs = (_post(o, xs, modsel[2], norm2_g[2], c_w_o[0].astype(BF16), w1[2], w2[2],
                nct=nct, skip_ctx=False),)

    q, k, v = _pre_attn(xs, modsel[3], norm1_g[3], d_w_qkv[0].astype(BF16), d_q_norm_g[0],
                        d_k_norm_g[0], rope_tabs, k_width=GQA_KV_HEADS * HEAD_DIM, dup_kv=True,
                        nct=nct)
    o = _gqa_attention(q, k, v, nct=nct)
    return _post(o, xs, modsel[3], norm2_g[3], d_w_o[0].astype(BF16), w1[3], w2[3],
                 nct=nct, skip_ctx=True)
```

```python
import functools
import math

import jax
import jax.numpy as jnp
from jax import lax
from jax.experimental import pallas as pl
from jax.experimental.pallas import tpu as pltpu

F32 = jnp.float32
BF16 = jnp.bfloat16

D_MODEL = 1024
HEAD_DIM = 64
GRID_W = 64
N_MOD = 6
ATTN_SCALE = HEAD_DIM ** -0.5
LOG2E = math.log2(math.e)
ROPE_THETA = 10000.0
NORM_EPS = 1e-6
DA_HEADS = D_MODEL // (2 * HEAD_DIM)
NA_ROWS_MAX = 8
NA_COLS = 16
LRU_BLOCK = 256
LRU_CONV = 4
LRU_C = 8.0
GQA_KV_HEADS = 4
D_FF = 4 * D_MODEL

LANES = 128
SUBLANES = 8
MXU_DIM = 256
VMEM_LIMIT = 56 * 1024 * 1024

TOK_TILE = 256
N_PAIRS = D_MODEL // LANES
MASK_VALUE = -1e30
NA_Q_ROWS = TOK_TILE // GRID_W
NA_K_ROWS = 12
NA_BAND = NA_K_ROWS * GRID_W


def _params(n_axes):
    return pltpu.CompilerParams(
        dimension_semantics=("arbitrary",) * n_axes, vmem_limit_bytes=VMEM_LIMIT)


def _resident(shape):
    zeros = (0,) * len(shape)
    return pl.BlockSpec(shape, lambda *_: zeros, pipeline_mode=pl.Buffered(1))


def _dot(a, b):
    return jnp.dot(a, b, preferred_element_type=F32)


def _dot_nt(a, b):
    return lax.dot_general(a, b, (((1,), (1,)), ((), ())), preferred_element_type=F32)


def _norm_mod(x, gain, shift, scale):
    y = x * lax.rsqrt(jnp.mean(x * x, axis=-1, keepdims=True) + NORM_EPS) * gain
    return y * (1.0 + scale) + shift


def _group_mean_matrix():
    r = lax.broadcasted_iota(jnp.int32, (MXU_DIM, MXU_DIM), 0) // HEAD_DIM
    c = lax.broadcasted_iota(jnp.int32, (MXU_DIM, MXU_DIM), 1) // HEAD_DIM
    return jnp.where(r == c, 1.0 / HEAD_DIM, 0.0).astype(BF16)


def _head_rms(x, gain, mean_mat):
    sq = (x * x).astype(BF16)
    ms = jnp.concatenate(
        [_dot(sq[:, j:j + MXU_DIM], mean_mat) for j in range(0, x.shape[1], MXU_DIM)], axis=-1)
    return x * lax.rsqrt(ms + NORM_EPS) * gain


def _rope(x, cos, sin):
    width = x.shape[1]
    reps = width // LANES
    c = jnp.concatenate([cos] * reps, axis=-1)
    s = jnp.concatenate([sin] * reps, axis=-1)
    lane = lax.broadcasted_iota(jnp.int32, x.shape, 1)
    partner = jnp.where((lane & 1) == 0, pltpu.roll(x, width - 1, 1), pltpu.roll(x, 1, 1))
    return x * c + partner * s


def _mods_kernel(c_ref, w_ref, b_ref, o_ref):
    c = c_ref[...]
    a = (c * jax.nn.sigmoid(c)).astype(BF16)
    o_ref[0] = _dot(a, w_ref[0].astype(BF16)) + b_ref[0]


def _modulation(c_all, w_mod, b_mod):
    depth, d, n = w_mod.shape
    rows = c_all.shape[0]
    tn = n // 4
    return pl.pallas_call(
        _mods_kernel,
        grid=(depth, n // tn),
        in_specs=[pl.BlockSpec((rows, d), lambda l, j: (0, 0)),
                  pl.BlockSpec((1, d, tn), lambda l, j: (l, 0, j)),
                  pl.BlockSpec((1, 1, tn), lambda l, j: (l, 0, j))],
        out_specs=pl.BlockSpec((1, rows, tn), lambda l, j: (l, 0, j)),
        out_shape=jax.ShapeDtypeStruct((depth, rows, n), F32),
        compiler_params=_params(2),
        name="adaln_modulation",
    )(c_all, w_mod, b_mod.reshape(depth, 1, n))


def _stream_tile(refs, nct):
    if len(refs) == 1:
        return refs[0][0]
    return jnp.where(pl.program_id(1) < nct, refs[0][0], refs[1][0])


def _stream_specs(xs, nct):
    d = xs[0].shape[-1]
    if len(xs) == 1:
        return [pl.BlockSpec((1, TOK_TILE, d), lambda b, t: (b, t, 0))]
    return [pl.BlockSpec((1, TOK_TILE, d), lambda b, t: (b, jnp.minimum(t, nct - 1), 0)),
            pl.BlockSpec((1, TOK_TILE, d), lambda b, t: (b, jnp.maximum(t - nct, 0), 0))]


def _pre_attn_kernel(*refs, n_streams, nct, k_width, rope, dup_kv):
    x_refs, refs = refs[:n_streams], refs[n_streams:]
    if rope:
        (mod_ref, ng_ref, w_ref, qg_ref, kg_ref, cos_ref, sin_ref, q_ref, k_ref, v_ref) = refs
    else:
        mod_ref, ng_ref, w_ref, qg_ref, kg_ref, q_ref, k_ref, v_ref = refs
    m = mod_ref[0]
    h = _norm_mod(_stream_tile(x_refs, nct), ng_ref[...], m[0:1, :], m[1:2, :]).astype(BF16)
    mean_mat = _group_mean_matrix()
    d = D_MODEL
    q = _head_rms(_dot(h, w_ref[:, 0:d]), qg_ref[...], mean_mat)
    k = _head_rms(_dot(h, w_ref[:, d:d + k_width]), kg_ref[...], mean_mat)
    v = _dot(h, w_ref[:, d + k_width:d + 2 * k_width])
    if rope:
        q = _rope(q, cos_ref[...], sin_ref[...])
        k = _rope(k, cos_ref[...], sin_ref[...])
    q = q * (ATTN_SCALE * LOG2E)
    for p in range(N_PAIRS):
        q_ref[0, p] = q[:, p * LANES:(p + 1) * LANES].astype(BF16)
    if dup_kv:
        for g in range(k_width // HEAD_DIM):
            kh = k[:, g * HEAD_DIM:(g + 1) * HEAD_DIM]
            k_ref[0, g] = jnp.concatenate([kh, kh], axis=-1).astype(BF16)
    else:
        for p in range(k_width // LANES):
            k_ref[0, p] = k[:, p * LANES:(p + 1) * LANES].astype(BF16)
    fill = jnp.ones((v.shape[0], LANES), F32)
    if dup_kv:
        for g in range(k_width // HEAD_DIM):
            vh = v[:, g * HEAD_DIM:(g + 1) * HEAD_DIM]
            v_ref[0, g] = jnp.concatenate([vh, vh, fill], axis=-1).astype(BF16)
    else:
        for p in range(k_width // LANES):
            v_ref[0, p] = jnp.concatenate(
                [v[:, p * LANES:(p + 1) * LANES], fill], axis=-1).astype(BF16)


def _mod_spec(nct):
    return pl.BlockSpec((1, N_MOD, D_MODEL),
                        lambda b, t: (2 * b + jnp.where(t >= nct, 1, 0), 0, 0))


def _pre_attn(xs, modsel, norm_g, w_qkv, q_gain, k_gain, rope_tabs, *, k_width, dup_kv, nct):
    bsz, _, d = xs[0].shape
    n_tok = sum(a.shape[1] for a in xs)
    nt = n_tok // TOK_TILE
    n_w = w_qkv.shape[1]
    k_slots = k_width // HEAD_DIM if dup_kv else k_width // LANES
    rope = rope_tabs is not None
    in_specs = _stream_specs(xs, nct) + [
        _mod_spec(nct),
        _resident((1, d)),
        _resident((d, n_w)),
        _resident((1, d)),
        _resident((1, k_width))]
    args = list(xs) + [modsel, norm_g.reshape(1, d), w_qkv,
                       jnp.tile(q_gain, d // HEAD_DIM).reshape(1, d),
                       jnp.tile(k_gain, k_width // HEAD_DIM).reshape(1, k_width)]
    if rope:
        in_specs += [pl.BlockSpec((TOK_TILE, LANES), lambda b, t: (t, 0))] * 2
        args += list(rope_tabs)
    qo = jax.ShapeDtypeStruct((bsz, N_PAIRS, n_tok, LANES), BF16)
    ko = jax.ShapeDtypeStruct((bsz, k_slots, n_tok, LANES), BF16)
    vo = jax.ShapeDtypeStruct((bsz, k_slots, n_tok, MXU_DIM), BF16)
    return pl.pallas_call(
        functools.partial(_pre_attn_kernel, n_streams=len(xs), nct=nct, k_width=k_width,
                          rope=rope, dup_kv=dup_kv),
        grid=(bsz, nt),
        in_specs=in_specs,
        out_specs=[pl.BlockSpec((1, N_PAIRS, TOK_TILE, LANES), lambda b, t: (b, 0, t, 0)),
                   pl.BlockSpec((1, k_slots, TOK_TILE, LANES), lambda b, t: (b, 0, t, 0)),
                   pl.BlockSpec((1, k_slots, TOK_TILE, MXU_DIM), lambda b, t: (b, 0, t, 0))],
        out_shape=[qo, ko, vo],
        compiler_params=_params(2),
        name="norm_mod_qkv",
    )(*args)


def _pre_lru_kernel(x_ref, mod_ref, ng_ref, w_ref, gate_ref, rec_ref):
    m = mod_ref[0]
    h = _norm_mod(x_ref[0], ng_ref[...], m[0:1, :], m[1:2, :]).astype(BF16)
    d = D_MODEL
    gate_ref[0] = jax.nn.gelu(_dot(h, w_ref[:, 0:d])).astype(BF16)
    rec_ref[0] = _dot(h, w_ref[:, d:2 * d])


def _pre_lru(xs, modsel, norm_g, w_in, *, nct):
    bsz, n_tok, d = xs.shape
    nt = n_tok // TOK_TILE
    tok_spec = pl.BlockSpec((1, TOK_TILE, d), lambda b, t: (b, t, 0))
    return pl.pallas_call(
        _pre_lru_kernel,
        grid=(bsz, nt),
        in_specs=[tok_spec, _mod_spec(nct), _resident((1, d)), _resident((d, 2 * d))],
        out_specs=[tok_spec, tok_spec],
        out_shape=[jax.ShapeDtypeStruct((bsz, n_tok, d), BF16),
                   jax.ShapeDtypeStruct((bsz, n_tok, d), F32)],
        compiler_params=_params(2),
        name="norm_mod_lru_in",
    )(xs, modsel, norm_g.reshape(1, d), w_in)


def _split_pair(q2):
    low = lax.broadcasted_iota(jnp.int32, q2.shape, 1) < HEAD_DIM
    zero = jnp.zeros_like(q2)
    return jnp.where(low, q2, zero), jnp.where(low, zero, q2)


def _exp2_parts(parts):
    m = functools.reduce(jnp.maximum, [jnp.max(s, axis=-1, keepdims=True) for s in parts])
    return [jnp.exp2(s - m) for s in parts]


def _pipeline3(n_units, scores, softmax, values):
    assert n_units % 2 == 0 and n_units >= 4
    scores(0, 0)
    scores(1, 1)
    softmax(0, 0)
    for u in range(2, n_units, 2):
        scores(u, 0)
        softmax(u - 1, 1)
        values(u - 2, 0)
        scores(u + 1, 1)
        softmax(u, 0)
        values(u - 1, 1)
    softmax(n_units - 1, 1)
    values(n_units - 2, 0)
    values(n_units - 1, 1)


def _sub_layer_norm(o, gain, lam_init):
    o = o * lax.rsqrt(jnp.mean(o * o, axis=-1, keepdims=True) + NORM_EPS) * gain
    return (o * (1.0 - lam_init)).astype(BF16)


def _scores_stage(s_buf, m_buf, q2, key_parts, bias_parts):
    for half, qh in enumerate(_split_pair(q2)):
        m, col = None, 0
        for kp, bp in zip(key_parts, bias_parts):
            s = _dot_nt(qh, kp)
            if bp is not None:
                s = s + bp[half].astype(F32)
            s_buf[half, :, col:col + kp.shape[0]] = s
            part_max = jnp.max(s, axis=-1, keepdims=True)
            m = part_max if m is None else jnp.maximum(m, part_max)
            col += kp.shape[0]
        m_buf[half] = jnp.broadcast_to(m, m_buf.shape[1:])


def _softmax_stage(s_buf, m_buf, p_buf):
    for half in range(2):
        p_buf[half] = jnp.exp2(s_buf[half] - m_buf[half][:, 0:1]).astype(BF16)


def _values_stage(p_buf, value_parts):
    outs = []
    for half in range(2):
        acc, col = None, 0
        for vp in value_parts:
            part = _dot(p_buf[half, :, col:col + vp.shape[0]], vp)
            acc = part if acc is None else acc + part
            col += vp.shape[0]
        outs.append(acc[:, :LANES] / acc[:, LANES:])
    return outs


def _direct_pair(q2, k2, v2):
    outs = []
    for qh in _split_pair(q2):
        (e,) = _exp2_parts([_dot_nt(qh, k2)])
        acc = _dot(e.astype(BF16), v2)
        outs.append(acc[:, :LANES] / acc[:, LANES:])
    return outs


def _attn_scratch(n_cols):
    scratch = [pltpu.VMEM((2, TOK_TILE, n_cols), F32)] * 2
    scratch += [pltpu.VMEM((2, TOK_TILE, LANES), F32)] * 2
    scratch += [pltpu.VMEM((2, TOK_TILE, n_cols), BF16)] * 2
    return scratch


def _diff_attn_kernel(lam_ref, sg_ref, q_ref, k_ref, v_ref, o_ref, s0, s1, m0, m1, p0, p1,
                      *, nct, n_ctx, lam_init):
    t = pl.program_id(1)
    lv = lam_ref[...]
    lam = (jnp.exp(jnp.sum(lv[0:1, :] * lv[1:2, :], axis=-1, keepdims=True))
           - jnp.exp(jnp.sum(lv[2:3, :] * lv[3:4, :], axis=-1, keepdims=True)) + lam_init)

    def finish(h, outs):
        o_ref[0, h] = _sub_layer_norm(outs[0] - lam * outs[1], sg_ref[...], lam_init)

    @pl.when(t < nct)
    def _():
        def head(h, carry):
            finish(h, _direct_pair(q_ref[0, h], k_ref[0, h, pl.ds(0, n_ctx), :],
                                   v_ref[0, h, pl.ds(0, n_ctx), :]))
            return carry
        lax.fori_loop(0, DA_HEADS, head, 0)

    @pl.when(t >= nct)
    def _():
        s_bufs, m_bufs, p_bufs = (s0, s1), (m0, m1), (p0, p1)
        _pipeline3(
            DA_HEADS,
            lambda h, slot: _scores_stage(s_bufs[slot], m_bufs[slot], q_ref[0, h],
                                          [k_ref[0, h]], [None]),
            lambda h, slot: _softmax_stage(s_bufs[slot], m_bufs[slot], p_bufs[slot]),
            lambda h, slot: finish(h, _values_stage(p_bufs[slot], [v_ref[0, h]])))


def _diff_attention(q, k, v, lam_vecs, subln_g, *, nct, n_ctx, lam_init):
    bsz, _, n_tok, _ = q.shape
    nt = n_tok // TOK_TILE
    tile = pl.BlockSpec((1, N_PAIRS, TOK_TILE, LANES), lambda b, t: (b, 0, t, 0))
    whole_k = pl.BlockSpec((1, DA_HEADS, n_tok, LANES), lambda b, t: (b, 0, 0, 0))
    whole_v = pl.BlockSpec((1, DA_HEADS, n_tok, MXU_DIM), lambda b, t: (b, 0, 0, 0))
    scratch = _attn_scratch(n_tok)
    return pl.pallas_call(
        functools.partial(_diff_attn_kernel, nct=nct, n_ctx=n_ctx, lam_init=lam_init),
        grid=(bsz, nt),
        in_specs=[_resident((4, HEAD_DIM)), _resident((1, LANES)), tile, whole_k, whole_v],
        out_specs=tile,
        out_shape=jax.ShapeDtypeStruct(q.shape, BF16),
        scratch_shapes=scratch,
        compiler_params=_params(2),
        name="diff_attention",
    )(lam_vecs, subln_g.reshape(1, LANES), q, k, v)


def _pair_output(outs):
    low = lax.broadcasted_iota(jnp.int32, outs[0].shape, 1) < HEAD_DIM
    return jnp.where(low, outs[0], outs[1]).astype(BF16)


def _gqa_attn_kernel(q_ref, k_ref, v_ref, o_ref, s0, s1, m0, m1, p0, p1):
    s_bufs, m_bufs, p_bufs = (s0, s1), (m0, m1), (p0, p1)

    def values(p, slot):
        o_ref[0, p] = _pair_output(_values_stage(p_bufs[slot], [v_ref[0, p // 2]]))

    _pipeline3(
        N_PAIRS,
        lambda p, slot: _scores_stage(s_bufs[slot], m_bufs[slot], q_ref[0, p],
                                      [k_ref[0, p // 2]], [None]),
        lambda p, slot: _softmax_stage(s_bufs[slot], m_bufs[slot], p_bufs[slot]),
        values)


def _gqa_attention(q, k, v, *, nct):
    bsz, _, n_tok, _ = q.shape
    nt = n_tok // TOK_TILE - nct
    tile = pl.BlockSpec((1, N_PAIRS, TOK_TILE, LANES), lambda b, t: (b, 0, t + nct, 0))
    whole_k = pl.BlockSpec((1, GQA_KV_HEADS, n_tok, LANES), lambda b, t: (b, 0, 0, 0))
    whole_v = pl.BlockSpec((1, GQA_KV_HEADS, n_tok, MXU_DIM), lambda b, t: (b, 0, 0, 0))
    scratch = _attn_scratch(n_tok)
    return pl.pallas_call(
        _gqa_attn_kernel,
        grid=(bsz, nt),
        in_specs=[tile, whole_k, whole_v],
        out_specs=tile,
        out_shape=jax.ShapeDtypeStruct(q.shape, BF16),
        scratch_shapes=scratch,
        compiler_params=_params(2),
        name="gqa_attention",
    )(q, k, v)


def _na_attn_kernel(bias_ref, q_ref, k_ref, v_ref, o_ref, s0, s1, m0, m1, p0, p1,
                    *, nct, n_ctx, n_rows):
    t = pl.program_id(1)

    @pl.when(t < nct)
    def _():
        def pair(p, carry):
            o_ref[0, p] = _pair_output(_direct_pair(
                q_ref[0, p], k_ref[0, p, pl.ds(0, n_ctx), :], v_ref[0, p, pl.ds(0, n_ctx), :]))
            return carry
        lax.fori_loop(0, N_PAIRS, pair, 0)

    @pl.when(t >= nct)
    def _():
        first_row = jnp.clip(NA_Q_ROWS * (t - nct) - NA_ROWS_MAX // 2, 0, n_rows - NA_K_ROWS)
        start = pl.multiple_of(n_ctx + first_row * GRID_W, GRID_W)
        s_bufs, m_bufs, p_bufs = (s0, s1), (m0, m1), (p0, p1)

        def scores(p, slot):
            keys = [k_ref[0, p, pl.ds(0, n_ctx), :], k_ref[0, p, pl.ds(start, NA_BAND), :]]
            bias = (bias_ref[0, 2 * p], bias_ref[0, 2 * p + 1])
            _scores_stage(s_bufs[slot], m_bufs[slot], q_ref[0, p], keys, [None, bias])

        def values(p, slot):
            vals = [v_ref[0, p, pl.ds(0, n_ctx), :], v_ref[0, p, pl.ds(start, NA_BAND), :]]
            o_ref[0, p] = _pair_output(_values_stage(p_bufs[slot], vals))

        _pipeline3(
            N_PAIRS, scores,
            lambda p, slot: _softmax_stage(s_bufs[slot], m_bufs[slot], p_bufs[slot]),
            values)


def _na_bias_table(rpb, n_rows):
    heads = rpb.shape[0]
    kr_rows = min(NA_ROWS_MAX, n_rows)
    variants = (0, 1, n_rows // NA_Q_ROWS - 1)
    n_tiles = len(variants)
    tt = jnp.array(variants)[:, None, None]
    qr = jnp.arange(NA_Q_ROWS)[None, :, None]
    kr = jnp.arange(NA_K_ROWS)[None, None, :]
    r = NA_Q_ROWS * tt + qr
    rp = jnp.clip(NA_Q_ROWS * tt - NA_ROWS_MAX // 2, 0, n_rows - NA_K_ROWS) + kr
    r0 = jnp.clip(r - kr_rows // 2, 0, n_rows - kr_rows)
    row_ok = (rp >= r0) & (rp < r0 + kr_rows)
    row_idx = jnp.clip(rp - r + NA_ROWS_MAX - 1, 0, 2 * NA_ROWS_MAX - 2)
    c = jnp.arange(GRID_W)[:, None]
    cp = jnp.arange(GRID_W)[None, :]
    cs = jnp.clip(c - NA_COLS // 2, 0, GRID_W - NA_COLS)
    col_ok = (cp >= cs) & (cp < cs + NA_COLS)
    col_idx = jnp.clip(cp - c, -(NA_COLS - 1), NA_COLS - 1) + NA_COLS - 1
    rows_sel = rpb[:, row_idx, :]
    onehot = (col_idx.reshape(-1)[None, :] == jnp.arange(2 * NA_COLS - 1)[:, None]).astype(F32)
    tab = jnp.dot(rows_sel.reshape(-1, 2 * NA_COLS - 1), onehot, precision=lax.Precision.HIGHEST)
    tab = tab.reshape(heads, n_tiles, NA_Q_ROWS, NA_K_ROWS, GRID_W, GRID_W)
    ok = row_ok[None, :, :, :, None, None] & col_ok[None, None, None, None, :, :]
    tab = jnp.where(ok, tab * LOG2E, MASK_VALUE)
    tab = tab.transpose(1, 0, 2, 4, 3, 5)
    return tab.reshape(n_tiles, heads, TOK_TILE, NA_BAND).astype(BF16)


def _na_attention(q, k, v, bias_tab, *, nct, n_ctx):
    bsz, _, n_tok, _ = q.shape
    nt = n_tok // TOK_TILE
    heads = bias_tab.shape[1]
    n_rows = (n_tok - n_ctx) // GRID_W
    tile = pl.BlockSpec((1, N_PAIRS, TOK_TILE, LANES), lambda b, t: (b, 0, t, 0))
    whole_k = pl.BlockSpec((1, N_PAIRS, n_tok, LANES), lambda b, t: (b, 0, 0, 0))
    whole_v = pl.BlockSpec((1, N_PAIRS, n_tok, MXU_DIM), lambda b, t: (b, 0, 0, 0))
    last = nt - nct - 1

    def bias_index(b, t):
        tile = t - nct
        return (jnp.where(tile <= 0, 0, jnp.where(tile == last, 2, 1)), 0, 0, 0)

    bias_spec = pl.BlockSpec((1, heads, TOK_TILE, NA_BAND), bias_index)
    scratch = _attn_scratch(n_ctx + NA_BAND)
    return pl.pallas_call(
        functools.partial(_na_attn_kernel, nct=nct, n_ctx=n_ctx, n_rows=n_rows),
        grid=(bsz, nt),
        in_specs=[bias_spec, tile, whole_k, whole_v],
        out_specs=tile,
        out_shape=jax.ShapeDtypeStruct(q.shape, BF16),
        scratch_shapes=scratch,
        compiler_params=_params(2),
        name="neighbourhood_attention",
    )(bias_tab, q, k, v)


LRU_CHUNK = 256
LRU_PAD = 8


def _softplus(z):
    return jnp.maximum(z, 0.0) + jnp.log1p(jnp.exp(-jnp.abs(z)))


def _sigmoid(z):
    return 0.5 * jnp.tanh(0.5 * z) + 0.5


def _sqrt_nonneg(z):
    return z * lax.rsqrt(jnp.maximum(z, float(jnp.finfo(F32).tiny)))


def _block_scan(a_ref, b_ref, slab, r0, reverse):
    n_blk = LRU_CHUNK // SUBLANES
    order = range(SUBLANES - 1, -1, -1) if reverse else range(SUBLANES)
    acc_a = acc_b = None
    for r in order:
        rows = pl.ds(r0 + r, n_blk, stride=SUBLANES)
        a_r = a_ref[slab, rows, :]
        b_r = b_ref[slab, rows, :]
        if acc_a is None:
            acc_a, acc_b = a_r, b_r
            continue
        acc_b = a_r * acc_b + b_r
        acc_a = a_r * acc_a
        a_ref[slab, rows, :] = acc_a
        b_ref[slab, rows, :] = acc_b


def _lru_kernel(rec_ref, gate_ref, cw_ref, cb_ref,
                fwa_ref, fwx_ref, bwa_ref, bwx_ref,
                fba_ref, fbx_ref, bba_ref, bbx_ref, flam_ref, blam_ref,
                o_ref, xp_ref, af_ref, bf_ref, ab_ref, bb_ref, *, n_ctx, n_tok):
    width = LRU_BLOCK
    xp_ref[0:LRU_PAD, :] = jnp.zeros((LRU_PAD, width), F32)
    xp_ref[LRU_PAD + n_tok:2 * LRU_PAD + n_tok, :] = jnp.zeros((LRU_PAD, width), F32)
    xp_ref[LRU_PAD:LRU_PAD + n_tok, :] = rec_ref[0]

    directions = (
        (fwa_ref, fwx_ref, fba_ref, fbx_ref, (-0.5 * LRU_C) * _softplus(-flam_ref[...]),
         af_ref, bf_ref, False),
        (bwa_ref, bwx_ref, bba_ref, bbx_ref, (-0.5 * LRU_C) * _softplus(-blam_ref[...]),
         ab_ref, bb_ref, True),
    )
    n_slabs = width // LANES
    wrow = lax.broadcasted_iota(jnp.int32, (LRU_CHUNK + 2 * LRU_PAD, 1), 0)

    for r0 in range(0, n_tok, LRU_CHUNK):
        xw = xp_ref[r0:r0 + LRU_CHUNK + 2 * LRU_PAD, :]
        if r0 == n_ctx:
            xw = jnp.where(wrow < LRU_PAD, 0.0, xw)
        if r0 + LRU_CHUNK == n_ctx:
            xw = jnp.where(wrow >= LRU_PAD + LRU_CHUNK, 0.0, xw)
        u = cb_ref[...]
        for j in range(LRU_CONV):
            off = LRU_PAD + j - LRU_CONV // 2
            u = u + xw[off:off + LRU_CHUNK, :] * cw_ref[j:j + 1, :]
        ub = u.astype(BF16)
        for wa_ref, wx_ref, ba_ref, bx_ref, decay, a_ref, b_ref, reverse in directions:
            log_a = decay * jnp.tanh(_dot(ub, wa_ref[0]) + ba_ref[...]) + decay
            i = 0.5 * jnp.tanh(_dot(ub, wx_ref[0]) + bx_ref[...]) + 0.5
            a = jnp.exp(log_a)
            b = _sqrt_nonneg(-jnp.tanh(log_a) * (a * a + 1.0)) * (i * u)
            for s in range(n_slabs):
                a_ref[s, r0:r0 + LRU_CHUNK, :] = a[:, s * LANES:(s + 1) * LANES]
                b_ref[s, r0:r0 + LRU_CHUNK, :] = b[:, s * LANES:(s + 1) * LANES]
                _block_scan(a_ref, b_ref, s, r0, reverse)

    n_all = n_tok // SUBLANES
    n_cb = n_ctx // SUBLANES

    def step(i, carry):
        cf, cr = carry
        rf = pl.multiple_of(i * SUBLANES, SUBLANES)
        jb = jnp.where(i < n_cb, n_cb - 1 - i, n_all - 1 - (i - n_cb))
        rb = pl.multiple_of(jb * SUBLANES, SUBLANES)
        hf = af_ref[:, pl.ds(rf, SUBLANES), :] * cf + bf_ref[:, pl.ds(rf, SUBLANES), :]
        bf_ref[:, pl.ds(rf, SUBLANES), :] = hf
        hb = ab_ref[:, pl.ds(rb, SUBLANES), :] * cr + bb_ref[:, pl.ds(rb, SUBLANES), :]
        bb_ref[:, pl.ds(rb, SUBLANES), :] = hb
        return (jnp.broadcast_to(hf[:, SUBLANES - 1:SUBLANES, :], hf.shape),
                jnp.broadcast_to(hb[:, 0:1, :], hb.shape))

    zero = jnp.zeros((n_slabs, SUBLANES, LANES), F32)
    lax.fori_loop(0, n_all, step, (zero, zero))

    gate = gate_ref[0]
    for s in range(n_slabs):
        y = (bf_ref[s] + bb_ref[s]) * gate[:, s * LANES:(s + 1) * LANES].astype(F32)
        o_ref[0, s] = y.astype(BF16)


def _lru_mixer(rec, gate, p, *, n_ctx):
    (conv_w, conv_b, f_wa, f_ba, f_wx, f_bx, f_lam, b_wa, b_ba, b_wx, b_bx, b_lam) = p
    bsz, n_tok, d = rec.shape
    n_blocks = d // LRU_BLOCK
    seq = pl.BlockSpec((1, n_tok, LRU_BLOCK), lambda b, n: (b, 0, n))
    vec = pl.BlockSpec((1, LRU_BLOCK), lambda b, n: (0, n))
    mat = pl.BlockSpec((1, LRU_BLOCK, LRU_BLOCK), lambda b, n: (n, 0, 0))
    row = lambda a: a.reshape(1, d)
    half = lambda a: 0.5 * a
    assert n_ctx % LRU_CHUNK == 0 and n_tok % LRU_CHUNK == 0
    scratch = [pltpu.VMEM((n_tok + 2 * LRU_PAD, LRU_BLOCK), F32)]
    scratch += [pltpu.VMEM((LRU_BLOCK // LANES, n_tok, LANES), F32)] * 4
    return pl.pallas_call(
        functools.partial(_lru_kernel, n_ctx=n_ctx, n_tok=n_tok),
        grid=(bsz, n_blocks),
        in_specs=[seq, seq, pl.BlockSpec((LRU_CONV, LRU_BLOCK), lambda b, n: (0, n)), vec,
                  mat, mat, mat, mat, vec, vec, vec, vec, vec, vec],
        out_specs=pl.BlockSpec((1, LRU_BLOCK // LANES, n_tok, LANES), lambda b, n: (b, n, 0, 0)),
        out_shape=jax.ShapeDtypeStruct((bsz, d // LANES, n_tok, LANES), BF16),
        scratch_shapes=scratch,
        compiler_params=_params(2),
        name="rglru_scan",
    )(rec, gate, conv_w, row(conv_b),
      half(f_wa).astype(BF16), half(f_wx).astype(BF16),
      half(b_wa).astype(BF16), half(b_wx).astype(BF16),
      row(half(f_ba)), row(half(f_bx)), row(half(b_ba)), row(half(b_bx)), row(f_lam), row(b_lam))


FF_CHUNK = 1024


def _post_kernel(o_ref, *refs, n_streams, nct):
    x_refs = refs[:n_streams]
    mod_ref, ng_ref, wo_ref, w1_ref, w2_ref, out_ref = refs[n_streams:]
    m = mod_ref[0]
    o = jnp.concatenate([o_ref[0, p] for p in range(N_PAIRS)], axis=-1)
    x1 = _stream_tile(x_refs, nct) + m[2:3, :] * _dot(o, wo_ref[...])
    h = _norm_mod(x1, ng_ref[...], m[3:4, :], m[4:5, :]).astype(BF16)
    acc = jnp.zeros_like(x1)
    for c in range(0, D_FF, FF_CHUNK):
        a = jnp.maximum(_dot(h, w1_ref[:, c:c + FF_CHUNK]), 0.0)
        acc = acc + _dot((a * a).astype(BF16), w2_ref[c:c + FF_CHUNK, :])
    out_ref[0] = x1 + m[5:6, :] * acc


def _post(o, xs, modsel, norm_g, w_o, w1, w2, *, nct, skip_ctx):
    bsz, _, d = xs[0].shape
    n_tok = sum(a.shape[1] for a in xs)
    first = nct if skip_ctx else 0
    nt = n_tok // TOK_TILE - first
    if len(xs) == 1:
        x_specs = [pl.BlockSpec((1, TOK_TILE, d), lambda b, t: (b, t + first, 0))]
    else:
        assert not skip_ctx
        x_specs = _stream_specs(xs, nct)
    return pl.pallas_call(
        functools.partial(_post_kernel, n_streams=len(xs), nct=nct),
        grid=(bsz, nt),
        in_specs=[pl.BlockSpec((1, N_PAIRS, TOK_TILE, LANES), lambda b, t: (b, 0, t + first, 0))]
        + x_specs
        + [pl.BlockSpec((1, N_MOD, d),
                        lambda b, t: (2 * b + jnp.where(t + first >= nct, 1, 0), 0, 0)),
           _resident((1, d)), _resident((d, d)), _resident((d, D_FF)), _resident((D_FF, d))],
        out_specs=pl.BlockSpec((1, TOK_TILE, d), lambda b, t: (b, t, 0)),
        out_shape=jax.ShapeDtypeStruct((bsz, nt * TOK_TILE, d), F32),
        compiler_params=_params(2),
        name="proj_residual_mlp",
    )(o, *xs, modsel, norm_g.reshape(1, d), w_o, w1, w2)


def _rope_tables(n_ctx, n_lat):
    t = jnp.arange(n_lat)
    row = (t // GRID_W).astype(F32)
    col = (t % GRID_W).astype(F32)
    n_freq = HEAD_DIM // 4
    inv = ROPE_THETA ** (-jnp.arange(n_freq, dtype=F32) / n_freq)
    ang = jnp.concatenate([row[:, None] * inv, col[:, None] * inv], axis=-1)
    cos = jnp.repeat(jnp.cos(ang), 2, axis=-1)
    sin = jnp.repeat(jnp.sin(ang), 2, axis=-1) * jnp.tile(jnp.array([-1.0, 1.0], F32), HEAD_DIM // 2)
    cos = jnp.concatenate([jnp.ones((n_ctx, HEAD_DIM), F32), cos], axis=0)
    sin = jnp.concatenate([jnp.zeros((n_ctx, HEAD_DIM), F32), sin], axis=0)
    return jnp.tile(cos, (1, LANES // HEAD_DIM)), jnp.tile(sin, (1, LANES // HEAD_DIM))


def kernel(x, c, ctx, c_ctx, norm1_g, norm2_g, w_mod, b_mod, w_mlp1, w_mlp2, a_w_qkv, a_q_norm_g, a_k_norm_g, a_lambda_q1, a_lambda_k1, a_lambda_q2, a_lambda_k2, a_subln_g, a_w_o, b_w_qkv, b_q_norm_g, b_k_norm_g, b_rpb, b_w_o, c_w_in, c_conv_w, c_conv_b, c_fwd_w_a, c_fwd_b_a, c_fwd_w_x, c_fwd_b_x, c_fwd_lam, c_bwd_w_a, c_bwd_b_a, c_bwd_w_x, c_bwd_b_x, c_bwd_lam, c_w_o, d_w_qkv, d_q_norm_g, d_k_norm_g, d_w_o):
    bsz, n_lat, d = x.shape
    n_ctx = ctx.shape[1]
    depth = w_mod.shape[0]
    assert d == D_MODEL and depth == 4 and n_ctx % TOK_TILE == 0 and n_lat % TOK_TILE == 0
    assert (n_lat // GRID_W) % NA_Q_ROWS == 0 and n_lat // GRID_W >= NA_K_ROWS
    nct = n_ctx // TOK_TILE

    rows = -(-(bsz + 1) // SUBLANES) * SUBLANES
    c_all = jnp.concatenate([c, c_ctx[None, :], jnp.zeros((rows - bsz - 1, d), F32)], axis=0)
    mods = _modulation(c_all, w_mod, b_mod).reshape(depth, rows, N_MOD, d)
    mod_ctx = jnp.broadcast_to(mods[:, bsz][:, None], (depth, bsz, N_MOD, d))
    modsel = jnp.stack([mod_ctx, mods[:, :bsz]], axis=2).reshape(depth, 2 * bsz, N_MOD, d)

    rope_tabs = _rope_tables(n_ctx, n_lat)
    w1 = w_mlp1.astype(BF16)
    w2 = w_mlp2.astype(BF16)

    xs = (ctx, x)
    q, k, v = _pre_attn(xs, modsel[0], norm1_g[0], a_w_qkv[0].astype(BF16), a_q_norm_g[0],
                        a_k_norm_g[0], rope_tabs, k_width=d, dup_kv=False, nct=nct)
    lam_vecs = jnp.stack([a_lambda_q1[0], a_lambda_k1[0], a_lambda_q2[0], a_lambda_k2[0]])
    lam_init = 0.8 - 0.6 * math.exp(-0.3 * 0)
    o = _diff_attention(q, k, v, lam_vecs, a_subln_g[0], nct=nct, n_ctx=n_ctx, lam_init=lam_init)
    xs = (_post(o, xs, modsel[0], norm2_g[0], a_w_o[0].astype(BF16), w1[0], w2[0],
                nct=nct, skip_ctx=False),)

    q, k, v = _pre_attn(xs, modsel[1], norm1_g[1], b_w_qkv[0].astype(BF16), b_q_norm_g[0],
                        b_k_norm_g[0], None, k_width=d, dup_kv=False, nct=nct)
    bias_tab = _na_bias_table(b_rpb[0], n_lat // GRID_W)
    o = _na_attention(q, k, v, bias_tab, nct=nct, n_ctx=n_ctx)
    xs = (_post(o, xs, modsel[1], norm2_g[1], b_w_o[0].astype(BF16), w1[1], w2[1],
                nct=nct, skip_ctx=False),)

    gate, rec = _pre_lru(xs[0], modsel[2], norm1_g[2], c_w_in[0].astype(BF16), nct=nct)
    lru_p = (c_conv_w[0], c_conv_b[0], c_fwd_w_a[0], c_fwd_b_a[0], c_fwd_w_x[0], c_fwd_b_x[0],
             c_fwd_lam[0], c_bwd_w_a[0], c_bwd_b_a[0], c_bwd_w_x[0], c_bwd_b_x[0], c_bwd_lam[0])
    o = _lru_mixer(rec, gate, lru_p, n_ctx=n_ctx)
    xs = (_post(o, xs, modsel[2], norm2_g[2], c_w_o[0].astype(BF16), w1[2], w2[2],
                nct=nct, skip_ctx=False),)

    q, k, v = _pre_attn(xs, modsel[3], norm1_g[3], d_w_qkv[0].astype(BF16), d_q_norm_g[0],
                        d_k_norm_g[0], rope_tabs, k_width=GQA_KV_HEADS * HEAD_DIM, dup_kv=True,
                        nct=nct)
    o = _gqa_attention(q, k, v, nct=nct)
    return _post(o, xs, modsel[3], norm2_g[3], d_w_o[0].astype(BF16), w1[3], w2[3],
                 nct=nct, skip_ctx=True)
```

```python
import functools
import math

import jax
import jax.numpy as jnp
from jax import lax
from jax.experimental import pallas as pl
from jax.experimental.pallas import tpu as pltpu

F32 = jnp.float32
BF16 = jnp.bfloat16

D_MODEL = 1024
HEAD_DIM = 64
GRID_W = 64
N_MOD = 6
ATTN_SCALE = HEAD_DIM ** -0.5
LOG2E = math.log2(math.e)
ROPE_THETA = 10000.0
NORM_EPS = 1e-6
DA_HEADS = D_MODEL // (2 * HEAD_DIM)
NA_ROWS_MAX = 8
NA_COLS = 16
LRU_BLOCK = 256
LRU_CONV = 4
LRU_C = 8.0
GQA_KV_HEADS = 4
D_FF = 4 * D_MODEL

LANES = 128
SUBLANES = 8
MXU_DIM = 256
VMEM_LIMIT = 56 * 1024 * 1024

TOK_TILE = 256
BATCH_BLOCK = 2
N_PAIRS = D_MODEL // LANES
MASK_VALUE = -1e30
NA_Q_ROWS = TOK_TILE // GRID_W
NA_K_ROWS = 12
NA_BAND = NA_K_ROWS * GRID_W


def _params(n_axes):
    return pltpu.CompilerParams(
        dimension_semantics=("arbitrary",) * n_axes, vmem_limit_bytes=VMEM_LIMIT)


def _resident(shape):
    zeros = (0,) * len(shape)
    return pl.BlockSpec(shape, lambda *_: zeros, pipeline_mode=pl.Buffered(1))


def _dot(a, b):
    return jnp.dot(a, b, preferred_element_type=F32)


def _dot_nt(a, b):
    return lax.dot_general(a, b, (((1,), (1,)), ((), ())), preferred_element_type=F32)


def _norm_mod(x, gain, shift, scale):
    y = x * lax.rsqrt(jnp.mean(x * x, axis=-1, keepdims=True) + NORM_EPS) * gain
    return y * (1.0 + scale) + shift


def _group_mean_matrix():
    r = lax.broadcasted_iota(jnp.int32, (MXU_DIM, MXU_DIM), 0) // HEAD_DIM
    c = lax.broadcasted_iota(jnp.int32, (MXU_DIM, MXU_DIM), 1) // HEAD_DIM
    return jnp.where(r == c, 1.0 / HEAD_DIM, 0.0).astype(BF16)


def _head_rms(x, gain, mean_mat):
    sq = (x * x).astype(BF16)
    ms = jnp.concatenate(
        [_dot(sq[:, j:j + MXU_DIM], mean_mat) for j in range(0, x.shape[1], MXU_DIM)], axis=-1)
    return x * lax.rsqrt(ms + NORM_EPS) * gain


def _rope(x, cos, sin):
    width = x.shape[1]
    reps = width // LANES
    c = jnp.concatenate([cos] * reps, axis=-1)
    s = jnp.concatenate([sin] * reps, axis=-1)
    lane = lax.broadcasted_iota(jnp.int32, x.shape, 1)
    partner = jnp.where((lane & 1) == 0, pltpu.roll(x, width - 1, 1), pltpu.roll(x, 1, 1))
    return x * c + partner * s


def _mods_kernel(c_ref, w_ref, b_ref, o_ref):
    c = c_ref[...]
    a = (c * jax.nn.sigmoid(c)).astype(BF16)
    o_ref[0] = _dot(a, w_ref[0].astype(BF16)) + b_ref[0]


def _modulation(c_all, w_mod, b_mod):
    depth, d, n = w_mod.shape
    rows = c_all.shape[0]
    tn = n // 4
    return pl.pallas_call(
        _mods_kernel,
        grid=(depth, n // tn),
        in_specs=[pl.BlockSpec((rows, d), lambda l, j: (0, 0)),
                  pl.BlockSpec((1, d, tn), lambda l, j: (l, 0, j)),
                  pl.BlockSpec((1, 1, tn), lambda l, j: (l, 0, j))],
        out_specs=pl.BlockSpec((1, rows, tn), lambda l, j: (l, 0, j)),
        out_shape=jax.ShapeDtypeStruct((depth, rows, n), F32),
        compiler_params=_params(2),
        name="adaln_modulation",
    )(c_all, w_mod, b_mod.reshape(depth, 1, n))


def _stream_tile(refs, i, nct):
    if len(refs) == 1:
        return refs[0][i]
    return jnp.where(pl.program_id(1) < nct, refs[0][i], refs[1][i])


def _stream_specs(xs, nct):
    d = xs[0].shape[-1]
    if len(xs) == 1:
        return [pl.BlockSpec((BATCH_BLOCK, TOK_TILE, d), lambda b, t: (b, t, 0))]
    return [pl.BlockSpec((BATCH_BLOCK, TOK_TILE, d), lambda b, t: (b, jnp.minimum(t, nct - 1), 0)),
            pl.BlockSpec((BATCH_BLOCK, TOK_TILE, d), lambda b, t: (b, jnp.maximum(t - nct, 0), 0))]


def _pre_attn_kernel(*refs, **static):
    for i in range(BATCH_BLOCK):
        _pre_attn_sample(i, *refs, **static)


def _pre_attn_sample(i, *refs, n_streams, nct, k_width, rope, dup_kv):
    x_refs, refs = refs[:n_streams], refs[n_streams:]
    if rope:
        (mod_ref, ng_ref, w_ref, qg_ref, kg_ref, cos_ref, sin_ref, q_ref, k_ref, v_ref) = refs
    else:
        mod_ref, ng_ref, w_ref, qg_ref, kg_ref, q_ref, k_ref, v_ref = refs
    q_ref, k_ref, v_ref = q_ref.at[i], k_ref.at[i], v_ref.at[i]
    m = mod_ref[i, 0]
    h = _norm_mod(_stream_tile(x_refs, i, nct), ng_ref[...], m[0:1, :], m[1:2, :]).astype(BF16)
    mean_mat = _group_mean_matrix()
    d = D_MODEL
    q = _head_rms(_dot(h, w_ref[:, 0:d]), qg_ref[...], mean_mat)
    k = _head_rms(_dot(h, w_ref[:, d:d + k_width]), kg_ref[...], mean_mat)
    v = _dot(h, w_ref[:, d + k_width:d + 2 * k_width])
    if rope:
        q = _rope(q, cos_ref[...], sin_ref[...])
        k = _rope(k, cos_ref[...], sin_ref[...])
    q = q * (ATTN_SCALE * LOG2E)
    for p in range(N_PAIRS):
        q_ref[p] = q[:, p * LANES:(p + 1) * LANES].astype(BF16)
    if dup_kv:
        for g in range(k_width // HEAD_DIM):
            kh = k[:, g * HEAD_DIM:(g + 1) * HEAD_DIM]
            k_ref[g] = jnp.concatenate([kh, kh], axis=-1).astype(BF16)
    else:
        for p in range(k_width // LANES):
            k_ref[p] = k[:, p * LANES:(p + 1) * LANES].astype(BF16)
    fill = jnp.ones((v.shape[0], LANES), F32)
    if dup_kv:
        for g in range(k_width // HEAD_DIM):
            vh = v[:, g * HEAD_DIM:(g + 1) * HEAD_DIM]
            v_ref[g] = jnp.concatenate([vh, vh, fill], axis=-1).astype(BF16)
    else:
        for p in range(k_width // LANES):
            v_ref[p] = jnp.concatenate(
                [v[:, p * LANES:(p + 1) * LANES], fill], axis=-1).astype(BF16)


def _mod_spec(nct, first=0):
    return pl.BlockSpec((BATCH_BLOCK, 1, N_MOD, D_MODEL),
                        lambda b, t: (b, jnp.where(t + first >= nct, 1, 0), 0, 0))


def _pre_attn(xs, modsel, norm_g, w_qkv, q_gain, k_gain, rope_tabs, *, k_width, dup_kv, nct):
    bsz, _, d = xs[0].shape
    n_tok = sum(a.shape[1] for a in xs)
    nt = n_tok // TOK_TILE
    n_w = w_qkv.shape[1]
    k_slots = k_width // HEAD_DIM if dup_kv else k_width // LANES
    rope = rope_tabs is not None
    in_specs = _stream_specs(xs, nct) + [
        _mod_spec(nct),
        _resident((1, d)),
        _resident((d, n_w)),
        _resident((1, d)),
        _resident((1, k_width))]
    args = list(xs) + [modsel, norm_g.reshape(1, d), w_qkv,
                       jnp.tile(q_gain, d // HEAD_DIM).reshape(1, d),
                       jnp.tile(k_gain, k_width // HEAD_DIM).reshape(1, k_width)]
    if rope:
        in_specs += [pl.BlockSpec((TOK_TILE, LANES), lambda b, t: (t, 0))] * 2
        args += list(rope_tabs)
    qo = jax.ShapeDtypeStruct((bsz, N_PAIRS, n_tok, LANES), BF16)
    ko = jax.ShapeDtypeStruct((bsz, k_slots, n_tok, LANES), BF16)
    vo = jax.ShapeDtypeStruct((bsz, k_slots, n_tok, MXU_DIM), BF16)
    return pl.pallas_call(
        functools.partial(_pre_attn_kernel, n_streams=len(xs), nct=nct, k_width=k_width,
                          rope=rope, dup_kv=dup_kv),
        grid=(bsz // BATCH_BLOCK, nt),
        in_specs=in_specs,
        out_specs=[
            pl.BlockSpec((BATCH_BLOCK, N_PAIRS, TOK_TILE, LANES), lambda b, t: (b, 0, t, 0)),
            pl.BlockSpec((BATCH_BLOCK, k_slots, TOK_TILE, LANES), lambda b, t: (b, 0, t, 0)),
            pl.BlockSpec((BATCH_BLOCK, k_slots, TOK_TILE, MXU_DIM), lambda b, t: (b, 0, t, 0))],
        out_shape=[qo, ko, vo],
        compiler_params=_params(2),
        name="norm_mod_qkv",
    )(*args)


def _pre_lru_kernel(x_ref, mod_ref, ng_ref, w_ref, gate_ref, rec_ref):
    d = D_MODEL
    for i in range(BATCH_BLOCK):
        m = mod_ref[i, 0]
        h = _norm_mod(x_ref[i], ng_ref[...], m[0:1, :], m[1:2, :]).astype(BF16)
        gate_ref[i] = jax.nn.gelu(_dot(h, w_ref[:, 0:d])).astype(BF16)
        rec_ref[i] = _dot(h, w_ref[:, d:2 * d])


def _pre_lru(xs, modsel, norm_g, w_in, *, nct):
    bsz, n_tok, d = xs.shape
    nt = n_tok // TOK_TILE
    tok_spec = pl.BlockSpec((BATCH_BLOCK, TOK_TILE, d), lambda b, t: (b, t, 0))
    return pl.pallas_call(
        _pre_lru_kernel,
        grid=(bsz // BATCH_BLOCK, nt),
        in_specs=[tok_spec, _mod_spec(nct), _resident((1, d)), _resident((d, 2 * d))],
        out_specs=[tok_spec, tok_spec],
        out_shape=[jax.ShapeDtypeStruct((bsz, n_tok, d), BF16),
                   jax.ShapeDtypeStruct((bsz, n_tok, d), F32)],
        compiler_params=_params(2),
        name="norm_mod_lru_in",
    )(xs, modsel, norm_g.reshape(1, d), w_in)


def _split_pair(q2):
    low = lax.broadcasted_iota(jnp.int32, q2.shape, 1) < HEAD_DIM
    zero = jnp.zeros_like(q2)
    return jnp.where(low, q2, zero), jnp.where(low, zero, q2)


def _exp2_parts(parts):
    m = functools.reduce(jnp.maximum, [jnp.max(s, axis=-1, keepdims=True) for s in parts])
    return [jnp.exp2(s - m) for s in parts]


def _pipeline3(n_units, scores, softmax, values):
    assert n_units % 2 == 0 and n_units >= 4
    scores(0, 0)
    scores(1, 1)
    softmax(0, 0)
    for u in range(2, n_units, 2):
        scores(u, 0)
        softmax(u - 1, 1)
        values(u - 2, 0)
        scores(u + 1, 1)
        softmax(u, 0)
        values(u - 1, 1)
    softmax(n_units - 1, 1)
    values(n_units - 2, 0)
    values(n_units - 1, 1)


def _sub_layer_norm(o, gain, lam_init):
    o = o * lax.rsqrt(jnp.mean(o * o, axis=-1, keepdims=True) + NORM_EPS) * gain
    return (o * (1.0 - lam_init)).astype(BF16)


def _scores_stage(s_buf, m_buf, q2, key_parts, bias_parts):
    for half, qh in enumerate(_split_pair(q2)):
        m, col = None, 0
        for kp, bp in zip(key_parts, bias_parts):
            s = _dot_nt(qh, kp)
            if bp is not None:
                s = s + bp[half].astype(F32)
            s_buf[half, :, col:col + kp.shape[0]] = s
            part_max = jnp.max(s, axis=-1, keepdims=True)
            m = part_max if m is None else jnp.maximum(m, part_max)
            col += kp.shape[0]
        m_buf[half] = jnp.broadcast_to(m, m_buf.shape[1:])


def _softmax_stage(s_buf, m_buf, p_buf):
    for half in range(2):
        p_buf[half] = jnp.exp2(s_buf[half] - m_buf[half][:, 0:1]).astype(BF16)


def _values_stage(p_buf, value_parts):
    outs = []
    for half in range(2):
        acc, col = None, 0
        for vp in value_parts:
            part = _dot(p_buf[half, :, col:col + vp.shape[0]], vp)
            acc = part if acc is None else acc + part
            col += vp.shape[0]
        outs.append(acc[:, :LANES] / acc[:, LANES:])
    return outs


def _direct_pair(q2, k2, v2):
    outs = []
    for qh in _split_pair(q2):
        (e,) = _exp2_parts([_dot_nt(qh, k2)])
        acc = _dot(e.astype(BF16), v2)
        outs.append(acc[:, :LANES] / acc[:, LANES:])
    return outs


def _attn_scratch(n_cols):
    scratch = [pltpu.VMEM((2, TOK_TILE, n_cols), F32)] * 2
    scratch += [pltpu.VMEM((2, TOK_TILE, LANES), F32)] * 2
    scratch += [pltpu.VMEM((2, TOK_TILE, n_cols), BF16)] * 2
    return scratch


def _diff_attn_kernel(lam_ref, sg_ref, q_ref, k_ref, v_ref, o_ref, s0, s1, m0, m1, p0, p1,
                      *, nct, n_ctx, lam_init):
    t = pl.program_id(1)
    lv = lam_ref[...]
    lam = (jnp.exp(jnp.sum(lv[0:1, :] * lv[1:2, :], axis=-1, keepdims=True))
           - jnp.exp(jnp.sum(lv[2:3, :] * lv[3:4, :], axis=-1, keepdims=True)) + lam_init)

    def finish(h, outs):
        o_ref[0, h] = _sub_layer_norm(outs[0] - lam * outs[1], sg_ref[...], lam_init)

    @pl.when(t < nct)
    def _():
        def head(h, carry):
            finish(h, _direct_pair(q_ref[0, h], k_ref[0, h, pl.ds(0, n_ctx), :],
                                   v_ref[0, h, pl.ds(0, n_ctx), :]))
            return carry
        lax.fori_loop(0, DA_HEADS, head, 0)

    @pl.when(t >= nct)
    def _():
        s_bufs, m_bufs, p_bufs = (s0, s1), (m0, m1), (p0, p1)
        _pipeline3(
            DA_HEADS,
            lambda h, slot: _scores_stage(s_bufs[slot], m_bufs[slot], q_ref[0, h],
                                          [k_ref[0, h]], [None]),
            lambda h, slot: _softmax_stage(s_bufs[slot], m_bufs[slot], p_bufs[slot]),
            lambda h, slot: finish(h, _values_stage(p_bufs[slot], [v_ref[0, h]])))


def _diff_attention(q, k, v, lam_vecs, subln_g, *, nct, n_ctx, lam_init):
    bsz, _, n_tok, _ = q.shape
    nt = n_tok // TOK_TILE
    tile = pl.BlockSpec((1, N_PAIRS, TOK_TILE, LANES), lambda b, t: (b, 0, t, 0))
    whole_k = pl.BlockSpec((1, DA_HEADS, n_tok, LANES), lambda b, t: (b, 0, 0, 0))
    whole_v = pl.BlockSpec((1, DA_HEADS, n_tok, MXU_DIM), lambda b, t: (b, 0, 0, 0))
    scratch = _attn_scratch(n_tok)
    return pl.pallas_call(
        functools.partial(_diff_attn_kernel, nct=nct, n_ctx=n_ctx, lam_init=lam_init),
        grid=(bsz, nt),
        in_specs=[_resident((4, HEAD_DIM)), _resident((1, LANES)), tile, whole_k, whole_v],
        out_specs=tile,
        out_shape=jax.ShapeDtypeStruct(q.shape, BF16),
        scratch_shapes=scratch,
        compiler_params=_params(2),
        name="diff_attention",
    )(lam_vecs, subln_g.reshape(1, LANES), q, k, v)


def _pair_output(outs):
    low = lax.broadcasted_iota(jnp.int32, outs[0].shape, 1) < HEAD_DIM
    return jnp.where(low, outs[0], outs[1]).astype(BF16)


def _gqa_attn_kernel(q_ref, k_ref, v_ref, o_ref, s0, s1, m0, m1, p0, p1):
    s_bufs, m_bufs, p_bufs = (s0, s1), (m0, m1), (p0, p1)

    def values(p, slot):
        o_ref[0, p] = _pair_output(_values_stage(p_bufs[slot], [v_ref[0, p // 2]]))

    _pipeline3(
        N_PAIRS,
        lambda p, slot: _scores_stage(s_bufs[slot], m_bufs[slot], q_ref[0, p],
                                      [k_ref[0, p // 2]], [None]),
        lambda p, slot: _softmax_stage(s_bufs[slot], m_bufs[slot], p_bufs[slot]),
        values)


def _gqa_attention(q, k, v, *, nct):
    bsz, _, n_tok, _ = q.shape
    nt = n_tok // TOK_TILE - nct
    tile = pl.BlockSpec((1, N_PAIRS, TOK_TILE, LANES), lambda b, t: (b, 0, t + nct, 0))
    whole_k = pl.BlockSpec((1, GQA_KV_HEADS, n_tok, LANES), lambda b, t: (b, 0, 0, 0))
    whole_v = pl.BlockSpec((1, GQA_KV_HEADS, n_tok, MXU_DIM), lambda b, t: (b, 0, 0, 0))
    scratch = _attn_scratch(n_tok)
    return pl.pallas_call(
        _gqa_attn_kernel,
        grid=(bsz, nt),
        in_specs=[tile, whole_k, whole_v],
        out_specs=tile,
        out_shape=jax.ShapeDtypeStruct(q.shape, BF16),
        scratch_shapes=scratch,
        compiler_params=_params(2),
        name="gqa_attention",
    )(q, k, v)


def _na_attn_kernel(bias_ref, q_ref, k_ref, v_ref, o_ref, s0, s1, m0, m1, p0, p1,
                    *, nct, n_ctx, n_rows):
    t = pl.program_id(1)

    @pl.when(t < nct)
    def _():
        def pair(p, carry):
            o_ref[0, p] = _pair_output(_direct_pair(
                q_ref[0, p], k_ref[0, p, pl.ds(0, n_ctx), :], v_ref[0, p, pl.ds(0, n_ctx), :]))
            return carry
        lax.fori_loop(0, N_PAIRS, pair, 0)

    @pl.when(t >= nct)
    def _():
        first_row = jnp.clip(NA_Q_ROWS * (t - nct) - NA_ROWS_MAX // 2, 0, n_rows - NA_K_ROWS)
        start = pl.multiple_of(n_ctx + first_row * GRID_W, GRID_W)
        s_bufs, m_bufs, p_bufs = (s0, s1), (m0, m1), (p0, p1)

        def scores(p, slot):
            keys = [k_ref[0, p, pl.ds(0, n_ctx), :], k_ref[0, p, pl.ds(start, NA_BAND), :]]
            bias = (bias_ref[0, 2 * p], bias_ref[0, 2 * p + 1])
            _scores_stage(s_bufs[slot], m_bufs[slot], q_ref[0, p], keys, [None, bias])

        def values(p, slot):
            vals = [v_ref[0, p, pl.ds(0, n_ctx), :], v_ref[0, p, pl.ds(start, NA_BAND), :]]
            o_ref[0, p] = _pair_output(_values_stage(p_bufs[slot], vals))

        _pipeline3(
            N_PAIRS, scores,
            lambda p, slot: _softmax_stage(s_bufs[slot], m_bufs[slot], p_bufs[slot]),
            values)


def _na_bias_table(rpb, n_rows):
    heads = rpb.shape[0]
    kr_rows = min(NA_ROWS_MAX, n_rows)
    variants = (0, 1, n_rows // NA_Q_ROWS - 1)
    n_tiles = len(variants)
    tt = jnp.array(variants)[:, None, None]
    qr = jnp.arange(NA_Q_ROWS)[None, :, None]
    kr = jnp.arange(NA_K_ROWS)[None, None, :]
    r = NA_Q_ROWS * tt + qr
    rp = jnp.clip(NA_Q_ROWS * tt - NA_ROWS_MAX // 2, 0, n_rows - NA_K_ROWS) + kr
    r0 = jnp.clip(r - kr_rows // 2, 0, n_rows - kr_rows)
    row_ok = (rp >= r0) & (rp < r0 + kr_rows)
    row_idx = jnp.clip(rp - r + NA_ROWS_MAX - 1, 0, 2 * NA_ROWS_MAX - 2)
    c = jnp.arange(GRID_W)[:, None]
    cp = jnp.arange(GRID_W)[None, :]
    cs = jnp.clip(c - NA_COLS // 2, 0, GRID_W - NA_COLS)
    col_ok = (cp >= cs) & (cp < cs + NA_COLS)
    col_idx = jnp.clip(cp - c, -(NA_COLS - 1), NA_COLS - 1) + NA_COLS - 1
    rows_sel = rpb[:, row_idx, :]
    onehot = (col_idx.reshape(-1)[None, :] == jnp.arange(2 * NA_COLS - 1)[:, None]).astype(F32)
    tab = jnp.dot(rows_sel.reshape(-1, 2 * NA_COLS - 1), onehot, precision=lax.Precision.HIGHEST)
    tab = tab.reshape(heads, n_tiles, NA_Q_ROWS, NA_K_ROWS, GRID_W, GRID_W)
    ok = row_ok[None, :, :, :, None, None] & col_ok[None, None, None, None, :, :]
    tab = jnp.where(ok, tab * LOG2E, MASK_VALUE)
    tab = tab.transpose(1, 0, 2, 4, 3, 5)
    return tab.reshape(n_tiles, heads, TOK_TILE, NA_BAND).astype(BF16)


def _na_attention(q, k, v, bias_tab, *, nct, n_ctx):
    bsz, _, n_tok, _ = q.shape
    nt = n_tok // TOK_TILE
    heads = bias_tab.shape[1]
    n_rows = (n_tok - n_ctx) // GRID_W
    tile = pl.BlockSpec((1, N_PAIRS, TOK_TILE, LANES), lambda b, t: (b, 0, t, 0))
    whole_k = pl.BlockSpec((1, N_PAIRS, n_tok, LANES), lambda b, t: (b, 0, 0, 0))
    whole_v = pl.BlockSpec((1, N_PAIRS, n_tok, MXU_DIM), lambda b, t: (b, 0, 0, 0))
    last = nt - nct - 1

    def bias_index(b, t):
        tile = t - nct
        return (jnp.where(tile <= 0, 0, jnp.where(tile == last, 2, 1)), 0, 0, 0)

    bias_spec = pl.BlockSpec((1, heads, TOK_TILE, NA_BAND), bias_index)
    scratch = _attn_scratch(n_ctx + NA_BAND)
    return pl.pallas_call(
        functools.partial(_na_attn_kernel, nct=nct, n_ctx=n_ctx, n_rows=n_rows),
        grid=(bsz, nt),
        in_specs=[bias_spec, tile, whole_k, whole_v],
        out_specs=tile,
        out_shape=jax.ShapeDtypeStruct(q.shape, BF16),
        scratch_shapes=scratch,
        compiler_params=_params(2),
        name="neighbourhood_attention",
    )(bias_tab, q, k, v)


LRU_CHUNK = 256
LRU_PAD = 8


def _softplus(z):
    return jnp.maximum(z, 0.0) + jnp.log1p(jnp.exp(-jnp.abs(z)))


def _sqrt_nonneg(z):
    return z * lax.rsqrt(jnp.maximum(z, float(jnp.finfo(F32).tiny)))


def _block_scan(a_ref, b_ref, slab, r0, reverse):
    n_blk = LRU_CHUNK // SUBLANES
    order = range(SUBLANES - 1, -1, -1) if reverse else range(SUBLANES)
    acc_a = acc_b = None
    for r in order:
        rows = pl.ds(r0 + r, n_blk, stride=SUBLANES)
        a_r = a_ref[slab, rows, :]
        b_r = b_ref[slab, rows, :]
        if acc_a is None:
            acc_a, acc_b = a_r, b_r
            continue
        acc_b = a_r * acc_b + b_r
        acc_a = a_r * acc_a
        a_ref[slab, rows, :] = acc_a
        b_ref[slab, rows, :] = acc_b


def _lru_kernel(rec_ref, gate_ref, cw_ref, cb_ref,
                fwa_ref, fwx_ref, bwa_ref, bwx_ref,
                fba_ref, fbx_ref, bba_ref, bbx_ref, flam_ref, blam_ref,
                o_ref, xp_ref, af_ref, bf_ref, ab_ref, bb_ref, *, n_ctx, n_tok):
    width = LRU_BLOCK
    xp_ref[0:LRU_PAD, :] = jnp.zeros((LRU_PAD, width), F32)
    xp_ref[LRU_PAD + n_tok:2 * LRU_PAD + n_tok, :] = jnp.zeros((LRU_PAD, width), F32)
    xp_ref[LRU_PAD:LRU_PAD + n_tok, :] = rec_ref[0]

    directions = (
        (fwa_ref, fwx_ref, fba_ref, fbx_ref, (-0.5 * LRU_C) * _softplus(-flam_ref[...]),
         af_ref, bf_ref, False),
        (bwa_ref, bwx_ref, bba_ref, bbx_ref, (-0.5 * LRU_C) * _softplus(-blam_ref[...]),
         ab_ref, bb_ref, True),
    )
    n_slabs = width // LANES
    wrow = lax.broadcasted_iota(jnp.int32, (LRU_CHUNK + 2 * LRU_PAD, 1), 0)

    for r0 in range(0, n_tok, LRU_CHUNK):
        xw = xp_ref[r0:r0 + LRU_CHUNK + 2 * LRU_PAD, :]
        if r0 == n_ctx:
            xw = jnp.where(wrow < LRU_PAD, 0.0, xw)
        if r0 + LRU_CHUNK == n_ctx:
            xw = jnp.where(wrow >= LRU_PAD + LRU_CHUNK, 0.0, xw)
        u = cb_ref[...]
        for j in range(LRU_CONV):
            off = LRU_PAD + j - LRU_CONV // 2
            u = u + xw[off:off + LRU_CHUNK, :] * cw_ref[j:j + 1, :]
        ub = u.astype(BF16)
        for wa_ref, wx_ref, ba_ref, bx_ref, decay, a_ref, b_ref, reverse in directions:
            log_a = decay * jnp.tanh(_dot(ub, wa_ref[0]) + ba_ref[...]) + decay
            i = 0.5 * jnp.tanh(_dot(ub, wx_ref[0]) + bx_ref[...]) + 0.5
            a = jnp.exp(log_a)
            b = _sqrt_nonneg(-jnp.tanh(log_a) * (a * a + 1.0)) * (i * u)
            for s in range(n_slabs):
                a_ref[s, r0:r0 + LRU_CHUNK, :] = a[:, s * LANES:(s + 1) * LANES]
                b_ref[s, r0:r0 + LRU_CHUNK, :] = b[:, s * LANES:(s + 1) * LANES]
                _block_scan(a_ref, b_ref, s, r0, reverse)

    n_all = n_tok // SUBLANES
    n_cb = n_ctx // SUBLANES

    def step(i, carry):
        cf, cr = carry
        rf = pl.multiple_of(i * SUBLANES, SUBLANES)
        jb = jnp.where(i < n_cb, n_cb - 1 - i, n_all - 1 - (i - n_cb))
        rb = pl.multiple_of(jb * SUBLANES, SUBLANES)
        hf = af_ref[:, pl.ds(rf, SUBLANES), :] * cf + bf_ref[:, pl.ds(rf, SUBLANES), :]
        bf_ref[:, pl.ds(rf, SUBLANES), :] = hf
        hb = ab_ref[:, pl.ds(rb, SUBLANES), :] * cr + bb_ref[:, pl.ds(rb, SUBLANES), :]
        bb_ref[:, pl.ds(rb, SUBLANES), :] = hb
        return (jnp.broadcast_to(hf[:, SUBLANES - 1:SUBLANES, :], hf.shape),
                jnp.broadcast_to(hb[:, 0:1, :], hb.shape))

    zero = jnp.zeros((n_slabs, SUBLANES, LANES), F32)
    lax.fori_loop(0, n_all, step, (zero, zero))

    gate = gate_ref[0]
    for s in range(n_slabs):
        y = (bf_ref[s] + bb_ref[s]) * gate[:, s * LANES:(s + 1) * LANES].astype(F32)
        o_ref[0, s] = y.astype(BF16)


def _lru_mixer(rec, gate, p, *, n_ctx):
    (conv_w, conv_b, f_wa, f_ba, f_wx, f_bx, f_lam, b_wa, b_ba, b_wx, b_bx, b_lam) = p
    bsz, n_tok, d = rec.shape
    n_blocks = d // LRU_BLOCK
    seq = pl.BlockSpec((1, n_tok, LRU_BLOCK), lambda b, n: (b, 0, n))
    vec = pl.BlockSpec((1, LRU_BLOCK), lambda b, n: (0, n))
    mat = pl.BlockSpec((1, LRU_BLOCK, LRU_BLOCK), lambda b, n: (n, 0, 0))
    row = lambda a: a.reshape(1, d)
    half = lambda a: 0.5 * a
    assert n_ctx % LRU_CHUNK == 0 and n_tok % LRU_CHUNK == 0
    scratch = [pltpu.VMEM((n_tok + 2 * LRU_PAD, LRU_BLOCK), F32)]
    scratch += [pltpu.VMEM((LRU_BLOCK // LANES, n_tok, LANES), F32)] * 4
    return pl.pallas_call(
        functools.partial(_lru_kernel, n_ctx=n_ctx, n_tok=n_tok),
        grid=(bsz, n_blocks),
        in_specs=[seq, seq, pl.BlockSpec((LRU_CONV, LRU_BLOCK), lambda b, n: (0, n)), vec,
                  mat, mat, mat, mat, vec, vec, vec, vec, vec, vec],
        out_specs=pl.BlockSpec((1, LRU_BLOCK // LANES, n_tok, LANES), lambda b, n: (b, n, 0, 0)),
        out_shape=jax.ShapeDtypeStruct((bsz, d // LANES, n_tok, LANES), BF16),
        scratch_shapes=scratch,
        compiler_params=_params(2),
        name="rglru_scan",
    )(rec, gate, conv_w, row(conv_b),
      half(f_wa).astype(BF16), half(f_wx).astype(BF16),
      half(b_wa).astype(BF16), half(b_wx).astype(BF16),
      row(half(f_ba)), row(half(f_bx)), row(half(b_ba)), row(half(b_bx)), row(f_lam), row(b_lam))


FF_CHUNK = 1024


def _post_kernel(o_ref, *refs, n_streams, nct):
    x_refs = refs[:n_streams]
    mod_ref, ng_ref, wo_ref, w1_ref, w2_ref, out_ref = refs[n_streams:]
    for i in range(BATCH_BLOCK):
        m = mod_ref[i, 0]
        o = jnp.concatenate([o_ref[i, p] for p in range(N_PAIRS)], axis=-1)
        x1 = _stream_tile(x_refs, i, nct) + m[2:3, :] * _dot(o, wo_ref[...])
        h = _norm_mod(x1, ng_ref[...], m[3:4, :], m[4:5, :]).astype(BF16)
        acc = jnp.zeros_like(x1)
        for c in range(0, D_FF, FF_CHUNK):
            a = jnp.maximum(_dot(h, w1_ref[:, c:c + FF_CHUNK]), 0.0)
            acc = acc + _dot((a * a).astype(BF16), w2_ref[c:c + FF_CHUNK, :])
        out_ref[i] = x1 + m[5:6, :] * acc


def _post(o, xs, modsel, norm_g, w_o, w1, w2, *, nct, skip_ctx):
    bsz, _, d = xs[0].shape
    n_tok = sum(a.shape[1] for a in xs)
    first = nct if skip_ctx else 0
    nt = n_tok // TOK_TILE - first
    if len(xs) == 1:
        x_specs = [pl.BlockSpec((BATCH_BLOCK, TOK_TILE, d), lambda b, t: (b, t + first, 0))]
    else:
        assert not skip_ctx
        x_specs = _stream_specs(xs, nct)
    return pl.pallas_call(
        functools.partial(_post_kernel, n_streams=len(xs), nct=nct),
        grid=(bsz // BATCH_BLOCK, nt),
        in_specs=[pl.BlockSpec((BATCH_BLOCK, N_PAIRS, TOK_TILE, LANES),
                               lambda b, t: (b, 0, t + first, 0))]
        + x_specs
        + [_mod_spec(nct, first),
           _resident((1, d)), _resident((d, d)), _resident((d, D_FF)), _resident((D_FF, d))],
        out_specs=pl.BlockSpec((BATCH_BLOCK, TOK_TILE, d), lambda b, t: (b, t, 0)),
        out_shape=jax.ShapeDtypeStruct((bsz, nt * TOK_TILE, d), F32),
        compiler_params=_params(2),
        name="proj_residual_mlp",
    )(o, *xs, modsel, norm_g.reshape(1, d), w_o, w1, w2)


def _rope_tables(n_ctx, n_lat):
    t = jnp.arange(n_lat)
    row = (t // GRID_W).astype(F32)
    col = (t % GRID_W).astype(F32)
    n_freq = HEAD_DIM // 4
    inv = ROPE_THETA ** (-jnp.arange(n_freq, dtype=F32) / n_freq)
    ang = jnp.concatenate([row[:, None] * inv, col[:, None] * inv], axis=-1)
    cos = jnp.repeat(jnp.cos(ang), 2, axis=-1)
    sin = jnp.repeat(jnp.sin(ang), 2, axis=-1) * jnp.tile(jnp.array([-1.0, 1.0], F32), HEAD_DIM // 2)
    cos = jnp.concatenate([jnp.ones((n_ctx, HEAD_DIM), F32), cos], axis=0)
    sin = jnp.concatenate([jnp.zeros((n_ctx, HEAD_DIM), F32), sin], axis=0)
    return jnp.tile(cos, (1, LANES // HEAD_DIM)), jnp.tile(sin, (1, LANES // HEAD_DIM))


def kernel(x, c, ctx, c_ctx, norm1_g, norm2_g, w_mod, b_mod, w_mlp1, w_mlp2, a_w_qkv, a_q_norm_g, a_k_norm_g, a_lambda_q1, a_lambda_k1, a_lambda_q2, a_lambda_k2, a_subln_g, a_w_o, b_w_qkv, b_q_norm_g, b_k_norm_g, b_rpb, b_w_o, c_w_in, c_conv_w, c_conv_b, c_fwd_w_a, c_fwd_b_a, c_fwd_w_x, c_fwd_b_x, c_fwd_lam, c_bwd_w_a, c_bwd_b_a, c_bwd_w_x, c_bwd_b_x, c_bwd_lam, c_w_o, d_w_qkv, d_q_norm_g, d_k_norm_g, d_w_o):
    bsz, n_lat, d = x.shape
    n_ctx = ctx.shape[1]
    depth = w_mod.shape[0]
    assert d == D_MODEL and depth == 4 and n_ctx % TOK_TILE == 0 and n_lat % TOK_TILE == 0
    assert bsz % BATCH_BLOCK == 0
    assert (n_lat // GRID_W) % NA_Q_ROWS == 0 and n_lat // GRID_W >= NA_K_ROWS
    nct = n_ctx // TOK_TILE

    rows = -(-(bsz + 1) // SUBLANES) * SUBLANES
    c_all = jnp.concatenate([c, c_ctx[None, :], jnp.zeros((rows - bsz - 1, d), F32)], axis=0)
    mods = _modulation(c_all, w_mod, b_mod).reshape(depth, rows, N_MOD, d)
    mod_ctx = jnp.broadcast_to(mods[:, bsz][:, None], (depth, bsz, N_MOD, d))
    modsel = jnp.stack([mod_ctx, mods[:, :bsz]], axis=2)

    rope_tabs = _rope_tables(n_ctx, n_lat)
    w1 = w_mlp1.astype(BF16)
    w2 = w_mlp2.astype(BF16)

    xs = (ctx, x)
    q, k, v = _pre_attn(xs, modsel[0], norm1_g[0], a_w_qkv[0].astype(BF16), a_q_norm_g[0],
                        a_k_norm_g[0], rope_tabs, k_width=d, dup_kv=False, nct=nct)
    lam_vecs = jnp.stack([a_lambda_q1[0], a_lambda_k1[0], a_lambda_q2[0], a_lambda_k2[0]])
    lam_init = 0.8 - 0.6 * math.exp(-0.3 * 0)
    o = _diff_attention(q, k, v, lam_vecs, a_subln_g[0], nct=nct, n_ctx=n_ctx, lam_init=lam_init)
    xs = (_post(o, xs, modsel[0], norm2_g[0], a_w_o[0].astype(BF16), w1[0], w2[0],
                nct=nct, skip_ctx=False),)

    q, k, v = _pre_attn(xs, modsel[1], norm1_g[1], b_w_qkv[0].astype(BF16), b_q_norm_g[0],
                        b_k_norm_g[0], None, k_width=d, dup_kv=False, nct=nct)
    bias_tab = _na_bias_table(b_rpb[0], n_lat // GRID_W)
    o = _na_attention(q, k, v, bias_tab, nct=nct, n_ctx=n_ctx)
    xs = (_post(o, xs, modsel[1], norm2_g[1], b_w_o[0].astype(BF16), w1[1], w2[1],
                nct=nct, skip_ctx=False),)

    gate, rec = _pre_lru(xs[0], modsel[2], norm1_g[2], c_w_in[0].astype(BF16), nct=nct)
    lru_p = (c_conv_w[0], c_conv_b[0], c_fwd_w_a[0], c_fwd_b_a[0], c_fwd_w_x[0], c_fwd_b_x[0],
             c_fwd_lam[0], c_bwd_w_a[0], c_bwd_b_a[0], c_bwd_w_x[0], c_bwd_b_x[0], c_bwd_lam[0])
    o = _lru_mixer(rec, gate, lru_p, n_ctx=n_ctx)
    xs = (_post(o, xs, modsel[2], norm2_g[2], c_w_o[0].astype(BF16), w1[2], w2[2],
                nct=nct, skip_ctx=False),)

    q, k, v = _pre_attn(xs, modsel[3], norm1_g[3], d_w_qkv[0].astype(BF16), d_q_norm_g[0],
                        d_k_norm_g[0], rope_tabs, k_width=GQA_KV_HEADS * HEAD_DIM, dup_kv=True,
                        nct=nct)
    o = _gqa_attention(q, k, v, nct=nct)
    return _post(o, xs, modsel[3], norm2_g[3], d_w_o[0].astype(BF16), w1[3], w2[3],
                 nct=nct, skip_ctx=True)
```

```python
import functools
import math

import jax
import jax.numpy as jnp
from jax import lax
from jax.experimental import pallas as pl
from jax.experimental.pallas import tpu as pltpu

F32 = jnp.float32
BF16 = jnp.bfloat16

D_MODEL = 1024
HEAD_DIM = 64
GRID_W = 64
N_MOD = 6
ATTN_SCALE = HEAD_DIM ** -0.5
LOG2E = math.log2(math.e)
ROPE_THETA = 10000.0
NORM_EPS = 1e-6
DA_HEADS = D_MODEL // (2 * HEAD_DIM)
NA_ROWS_MAX = 8
NA_COLS = 16
LRU_BLOCK = 256
LRU_CONV = 4
LRU_C = 8.0
GQA_KV_HEADS = 4
D_FF = 4 * D_MODEL

LANES = 128
SUBLANES = 8
MXU_DIM = 256
VMEM_LIMIT = 56 * 1024 * 1024

TOK_TILE = 256
BATCH_BLOCK = 2
N_PAIRS = D_MODEL // LANES
MASK_VALUE = -1e30
NA_Q_ROWS = TOK_TILE // GRID_W
NA_K_ROWS = 12
NA_BAND = NA_K_ROWS * GRID_W


def _params(n_axes):
    return pltpu.CompilerParams(
        dimension_semantics=("arbitrary",) * n_axes, vmem_limit_bytes=VMEM_LIMIT)


def _resident(shape):
    zeros = (0,) * len(shape)
    return pl.BlockSpec(shape, lambda *_: zeros, pipeline_mode=pl.Buffered(1))


def _dot(a, b):
    return jnp.dot(a, b, preferred_element_type=F32)


def _dot_nt(a, b):
    return lax.dot_general(a, b, (((1,), (1,)), ((), ())), preferred_element_type=F32)


def _norm_mod(x, gain, shift, scale):
    y = x * lax.rsqrt(jnp.mean(x * x, axis=-1, keepdims=True) + NORM_EPS) * gain
    return y * (1.0 + scale) + shift


def _group_mean_matrix():
    r = lax.broadcasted_iota(jnp.int32, (MXU_DIM, MXU_DIM), 0) // HEAD_DIM
    c = lax.broadcasted_iota(jnp.int32, (MXU_DIM, MXU_DIM), 1) // HEAD_DIM
    return jnp.where(r == c, 1.0 / HEAD_DIM, 0.0).astype(BF16)


def _head_rms(x, gain, mean_mat):
    sq = (x * x).astype(BF16)
    ms = jnp.concatenate(
        [_dot(sq[:, j:j + MXU_DIM], mean_mat) for j in range(0, x.shape[1], MXU_DIM)], axis=-1)
    return x * lax.rsqrt(ms + NORM_EPS) * gain


def _rope(x, cos, sin):
    width = x.shape[1]
    reps = width // LANES
    c = jnp.concatenate([cos] * reps, axis=-1)
    s = jnp.concatenate([sin] * reps, axis=-1)
    lane = lax.broadcasted_iota(jnp.int32, x.shape, 1)
    partner = jnp.where((lane & 1) == 0, pltpu.roll(x, width - 1, 1), pltpu.roll(x, 1, 1))
    return x * c + partner * s


def _mods_kernel(c_ref, w_ref, b_ref, o_ref):
    c = c_ref[...]
    a = (c * jax.nn.sigmoid(c)).astype(BF16)
    o_ref[0] = _dot(a, w_ref[0].astype(BF16)) + b_ref[0]


def _modulation(c_all, w_mod, b_mod):
    depth, d, n = w_mod.shape
    rows = c_all.shape[0]
    tn = n // 4
    return pl.pallas_call(
        _mods_kernel,
        grid=(depth, n // tn),
        in_specs=[pl.BlockSpec((rows, d), lambda l, j: (0, 0)),
                  pl.BlockSpec((1, d, tn), lambda l, j: (l, 0, j)),
                  pl.BlockSpec((1, 1, tn), lambda l, j: (l, 0, j))],
        out_specs=pl.BlockSpec((1, rows, tn), lambda l, j: (l, 0, j)),
        out_shape=jax.ShapeDtypeStruct((depth, rows, n), F32),
        compiler_params=_params(2),
        name="adaln_modulation",
    )(c_all, w_mod, b_mod.reshape(depth, 1, n))


def _stream_tile(refs, i, nct):
    if len(refs) == 1:
        return refs[0][i]
    return jnp.where(pl.program_id(1) < nct, refs[0][i], refs[1][i])


def _stream_specs(xs, nct):
    d = xs[0].shape[-1]
    if len(xs) == 1:
        return [pl.BlockSpec((BATCH_BLOCK, TOK_TILE, d), lambda b, t: (b, t, 0))]
    return [pl.BlockSpec((BATCH_BLOCK, TOK_TILE, d), lambda b, t: (b, jnp.minimum(t, nct - 1), 0)),
            pl.BlockSpec((BATCH_BLOCK, TOK_TILE, d), lambda b, t: (b, jnp.maximum(t - nct, 0), 0))]


def _pre_attn_kernel(*refs, **static):
    for i in range(BATCH_BLOCK):
        _pre_attn_sample(i, *refs, **static)


def _pre_attn_sample(i, *refs, n_streams, nct, k_width, rope, dup_kv):
    x_refs, refs = refs[:n_streams], refs[n_streams:]
    if rope:
        (mod_ref, ng_ref, w_ref, qg_ref, kg_ref, cos_ref, sin_ref, q_ref, k_ref, v_ref) = refs
    else:
        mod_ref, ng_ref, w_ref, qg_ref, kg_ref, q_ref, k_ref, v_ref = refs
    q_ref, k_ref, v_ref = q_ref.at[i], k_ref.at[i], v_ref.at[i]
    m = mod_ref[i, 0]
    h = _norm_mod(_stream_tile(x_refs, i, nct), ng_ref[...], m[0:1, :], m[1:2, :]).astype(BF16)
    mean_mat = _group_mean_matrix()
    d = D_MODEL
    q = _head_rms(_dot(h, w_ref[:, 0:d]), qg_ref[...], mean_mat)
    k = _head_rms(_dot(h, w_ref[:, d:d + k_width]), kg_ref[...], mean_mat)
    v = _dot(h, w_ref[:, d + k_width:d + 2 * k_width])
    if rope:
        q = _rope(q, cos_ref[...], sin_ref[...])
        k = _rope(k, cos_ref[...], sin_ref[...])
    q = q * (ATTN_SCALE * LOG2E)
    for p in range(N_PAIRS):
        q_ref[p] = q[:, p * LANES:(p + 1) * LANES].astype(BF16)
    if dup_kv:
        for g in range(k_width // HEAD_DIM):
            kh = k[:, g * HEAD_DIM:(g + 1) * HEAD_DIM]
            k_ref[g] = jnp.concatenate([kh, kh], axis=-1).astype(BF16)
    else:
        for p in range(k_width // LANES):
            k_ref[p] = k[:, p * LANES:(p + 1) * LANES].astype(BF16)
    fill = jnp.ones((v.shape[0], LANES), F32)
    if dup_kv:
        for g in range(k_width // HEAD_DIM):
            vh = v[:, g * HEAD_DIM:(g + 1) * HEAD_DIM]
            v_ref[g] = jnp.concatenate([vh, vh, fill], axis=-1).astype(BF16)
    else:
        for p in range(k_width // LANES):
            v_ref[p] = jnp.concatenate(
                [v[:, p * LANES:(p + 1) * LANES], fill], axis=-1).astype(BF16)


def _mod_spec(nct, first=0):
    return pl.BlockSpec((BATCH_BLOCK, 1, N_MOD, D_MODEL),
                        lambda b, t: (b, jnp.where(t + first >= nct, 1, 0), 0, 0))


def _pre_attn(xs, modsel, norm_g, w_qkv, q_gain, k_gain, rope_tabs, *, k_width, dup_kv, nct):
    bsz, _, d = xs[0].shape
    n_tok = sum(a.shape[1] for a in xs)
    nt = n_tok // TOK_TILE
    n_w = w_qkv.shape[1]
    k_slots = k_width // HEAD_DIM if dup_kv else k_width // LANES
    rope = rope_tabs is not None
    in_specs = _stream_specs(xs, nct) + [
        _mod_spec(nct),
        _resident((1, d)),
        _resident((d, n_w)),
        _resident((1, d)),
        _resident((1, k_width))]
    args = list(xs) + [modsel, norm_g.reshape(1, d), w_qkv,
                       jnp.tile(q_gain, d // HEAD_DIM).reshape(1, d),
                       jnp.tile(k_gain, k_width // HEAD_DIM).reshape(1, k_width)]
    if rope:
        in_specs += [pl.BlockSpec((TOK_TILE, LANES), lambda b, t: (t, 0))] * 2
        args += list(rope_tabs)
    qo = jax.ShapeDtypeStruct((bsz, N_PAIRS, n_tok, LANES), BF16)
    ko = jax.ShapeDtypeStruct((bsz, k_slots, n_tok, LANES), BF16)
    vo = jax.ShapeDtypeStruct((bsz, k_slots, n_tok, MXU_DIM), BF16)
    return pl.pallas_call(
        functools.partial(_pre_attn_kernel, n_streams=len(xs), nct=nct, k_width=k_width,
                          rope=rope, dup_kv=dup_kv),
        grid=(bsz // BATCH_BLOCK, nt),
        in_specs=in_specs,
        out_specs=[
            pl.BlockSpec((BATCH_BLOCK, N_PAIRS, TOK_TILE, LANES), lambda b, t: (b, 0, t, 0)),
            pl.BlockSpec((BATCH_BLOCK, k_slots, TOK_TILE, LANES), lambda b, t: (b, 0, t, 0)),
            pl.BlockSpec((BATCH_BLOCK, k_slots, TOK_TILE, MXU_DIM), lambda b, t: (b, 0, t, 0))],
        out_shape=[qo, ko, vo],
        compiler_params=_params(2),
        name="norm_mod_qkv",
    )(*args)


def _pre_lru_kernel(x_ref, mod_ref, ng_ref, w_ref, gate_ref, rec_ref):
    d = D_MODEL
    for i in range(BATCH_BLOCK):
        m = mod_ref[i, 0]
        h = _norm_mod(x_ref[i], ng_ref[...], m[0:1, :], m[1:2, :]).astype(BF16)
        gate_ref[i] = jax.nn.gelu(_dot(h, w_ref[:, 0:d])).astype(BF16)
        rec_ref[i] = _dot(h, w_ref[:, d:2 * d])


def _pre_lru(xs, modsel, norm_g, w_in, *, nct):
    bsz, n_tok, d = xs.shape
    nt = n_tok // TOK_TILE
    tok_spec = pl.BlockSpec((BATCH_BLOCK, TOK_TILE, d), lambda b, t: (b, t, 0))
    return pl.pallas_call(
        _pre_lru_kernel,
        grid=(bsz // BATCH_BLOCK, nt),
        in_specs=[tok_spec, _mod_spec(nct), _resident((1, d)), _resident((d, 2 * d))],
        out_specs=[tok_spec, tok_spec],
        out_shape=[jax.ShapeDtypeStruct((bsz, n_tok, d), BF16),
                   jax.ShapeDtypeStruct((bsz, n_tok, d), F32)],
        compiler_params=_params(2),
        name="norm_mod_lru_in",
    )(xs, modsel, norm_g.reshape(1, d), w_in)


def _split_pair(q2):
    low = lax.broadcasted_iota(jnp.int32, q2.shape, 1) < HEAD_DIM
    zero = jnp.zeros_like(q2)
    return jnp.where(low, q2, zero), jnp.where(low, zero, q2)


def _exp2_parts(parts):
    m = functools.reduce(jnp.maximum, [jnp.max(s, axis=-1, keepdims=True) for s in parts])
    return [jnp.exp2(s - m) for s in parts]


def _pipeline3(n_units, scores, softmax, values):
    assert n_units % 2 == 0 and n_units >= 4
    scores(0, 0)
    scores(1, 1)
    softmax(0, 0)
    for u in range(2, n_units, 2):
        scores(u, 0)
        softmax(u - 1, 1)
        values(u - 2, 0)
        scores(u + 1, 1)
        softmax(u, 0)
        values(u - 1, 1)
    softmax(n_units - 1, 1)
    values(n_units - 2, 0)
    values(n_units - 1, 1)


def _sub_layer_norm(o, gain, lam_init):
    o = o * lax.rsqrt(jnp.mean(o * o, axis=-1, keepdims=True) + NORM_EPS) * gain
    return (o * (1.0 - lam_init)).astype(BF16)


def _scores_stage(s_buf, m_buf, q2, key_parts, bias_parts):
    for half, qh in enumerate(_split_pair(q2)):
        m, col = None, 0
        for kp, bp in zip(key_parts, bias_parts):
            s = _dot_nt(qh, kp)
            if bp is not None:
                s = s + bp[half].astype(F32)
            s_buf[half, :, col:col + kp.shape[0]] = s
            part_max = jnp.max(s, axis=-1, keepdims=True)
            m = part_max if m is None else jnp.maximum(m, part_max)
            col += kp.shape[0]
        m_buf[half] = jnp.broadcast_to(m, m_buf.shape[1:])


def _softmax_stage(s_buf, m_buf, p_buf):
    for half in range(2):
        p_buf[half] = jnp.exp2(s_buf[half] - m_buf[half][:, 0:1]).astype(BF16)


def _values_stage(p_buf, value_parts):
    outs = []
    for half in range(2):
        acc, col = None, 0
        for vp in value_parts:
            part = _dot(p_buf[half, :, col:col + vp.shape[0]], vp)
            acc = part if acc is None else acc + part
            col += vp.shape[0]
        outs.append(acc[:, :LANES] / acc[:, LANES:])
    return outs


def _direct_pair(q2, k2, v2):
    outs = []
    for qh in _split_pair(q2):
        (e,) = _exp2_parts([_dot_nt(qh, k2)])
        acc = _dot(e.astype(BF16), v2)
        outs.append(acc[:, :LANES] / acc[:, LANES:])
    return outs


def _attn_scratch(n_cols):
    scratch = [pltpu.VMEM((2, TOK_TILE, n_cols), F32)] * 2
    scratch += [pltpu.VMEM((2, TOK_TILE, LANES), F32)] * 2
    scratch += [pltpu.VMEM((2, TOK_TILE, n_cols), BF16)] * 2
    return scratch


def _diff_attn_kernel(lam_ref, sg_ref, q_ref, k_ref, v_ref, o_ref, s0, s1, m0, m1, p0, p1,
                      *, nct, n_ctx, lam_init):
    t = pl.program_id(1)
    lv = lam_ref[...]
    lam = (jnp.exp(jnp.sum(lv[0:1, :] * lv[1:2, :], axis=-1, keepdims=True))
           - jnp.exp(jnp.sum(lv[2:3, :] * lv[3:4, :], axis=-1, keepdims=True)) + lam_init)

    def finish(h, outs):
        o_ref[0, h] = _sub_layer_norm(outs[0] - lam * outs[1], sg_ref[...], lam_init)

    @pl.when(t < nct)
    def _():
        for h in range(DA_HEADS):
            finish(h, _direct_pair(q_ref[0, h], k_ref[0, h, 0:n_ctx, :], v_ref[0, h, 0:n_ctx, :]))

    @pl.when(t >= nct)
    def _():
        s_bufs, m_bufs, p_bufs = (s0, s1), (m0, m1), (p0, p1)
        _pipeline3(
            DA_HEADS,
            lambda h, slot: _scores_stage(s_bufs[slot], m_bufs[slot], q_ref[0, h],
                                          [k_ref[0, h]], [None]),
            lambda h, slot: _softmax_stage(s_bufs[slot], m_bufs[slot], p_bufs[slot]),
            lambda h, slot: finish(h, _values_stage(p_bufs[slot], [v_ref[0, h]])))


def _diff_attention(q, k, v, lam_vecs, subln_g, *, nct, n_ctx, lam_init):
    bsz, _, n_tok, _ = q.shape
    nt = n_tok // TOK_TILE
    tile = pl.BlockSpec((1, N_PAIRS, TOK_TILE, LANES), lambda b, t: (b, 0, t, 0))
    whole_k = pl.BlockSpec((1, DA_HEADS, n_tok, LANES), lambda b, t: (b, 0, 0, 0))
    whole_v = pl.BlockSpec((1, DA_HEADS, n_tok, MXU_DIM), lambda b, t: (b, 0, 0, 0))
    scratch = _attn_scratch(n_tok)
    return pl.pallas_call(
        functools.partial(_diff_attn_kernel, nct=nct, n_ctx=n_ctx, lam_init=lam_init),
        grid=(bsz, nt),
        in_specs=[_resident((4, HEAD_DIM)), _resident((1, LANES)), tile, whole_k, whole_v],
        out_specs=tile,
        out_shape=jax.ShapeDtypeStruct(q.shape, BF16),
        scratch_shapes=scratch,
        compiler_params=_params(2),
        name="diff_attention",
    )(lam_vecs, subln_g.reshape(1, LANES), q, k, v)


def _pair_output(outs):
    low = lax.broadcasted_iota(jnp.int32, outs[0].shape, 1) < HEAD_DIM
    return jnp.where(low, outs[0], outs[1]).astype(BF16)


def _gqa_attn_kernel(q_ref, k_ref, v_ref, o_ref, s0, s1, m0, m1, p0, p1):
    s_bufs, m_bufs, p_bufs = (s0, s1), (m0, m1), (p0, p1)

    def values(p, slot):
        o_ref[0, p] = _pair_output(_values_stage(p_bufs[slot], [v_ref[0, p // 2]]))

    _pipeline3(
        N_PAIRS,
        lambda p, slot: _scores_stage(s_bufs[slot], m_bufs[slot], q_ref[0, p],
                                      [k_ref[0, p // 2]], [None]),
        lambda p, slot: _softmax_stage(s_bufs[slot], m_bufs[slot], p_bufs[slot]),
        values)


def _gqa_attention(q, k, v, *, nct):
    bsz, _, n_tok, _ = q.shape
    nt = n_tok // TOK_TILE - nct
    tile = pl.BlockSpec((1, N_PAIRS, TOK_TILE, LANES), lambda b, t: (b, 0, t + nct, 0))
    whole_k = pl.BlockSpec((1, GQA_KV_HEADS, n_tok, LANES), lambda b, t: (b, 0, 0, 0))
    whole_v = pl.BlockSpec((1, GQA_KV_HEADS, n_tok, MXU_DIM), lambda b, t: (b, 0, 0, 0))
    scratch = _attn_scratch(n_tok)
    return pl.pallas_call(
        _gqa_attn_kernel,
        grid=(bsz, nt),
        in_specs=[tile, whole_k, whole_v],
        out_specs=tile,
        out_shape=jax.ShapeDtypeStruct(q.shape, BF16),
        scratch_shapes=scratch,
        compiler_params=_params(2),
        name="gqa_attention",
    )(q, k, v)


def _na_attn_kernel(bias_ref, q_ref, k_ref, v_ref, o_ref, s0, s1, m0, m1, p0, p1,
                    *, nct, n_ctx, n_rows):
    t = pl.program_id(1)

    @pl.when(t < nct)
    def _():
        for p in range(N_PAIRS):
            o_ref[0, p] = _pair_output(_direct_pair(
                q_ref[0, p], k_ref[0, p, 0:n_ctx, :], v_ref[0, p, 0:n_ctx, :]))

    @pl.when(t >= nct)
    def _():
        first_row = jnp.clip(NA_Q_ROWS * (t - nct) - NA_ROWS_MAX // 2, 0, n_rows - NA_K_ROWS)
        start = pl.multiple_of(n_ctx + first_row * GRID_W, GRID_W)
        s_bufs, m_bufs, p_bufs = (s0, s1), (m0, m1), (p0, p1)

        def scores(p, slot):
            keys = [k_ref[0, p, pl.ds(0, n_ctx), :], k_ref[0, p, pl.ds(start, NA_BAND), :]]
            bias = (bias_ref[0, 2 * p], bias_ref[0, 2 * p + 1])
            _scores_stage(s_bufs[slot], m_bufs[slot], q_ref[0, p], keys, [None, bias])

        def values(p, slot):
            vals = [v_ref[0, p, pl.ds(0, n_ctx), :], v_ref[0, p, pl.ds(start, NA_BAND), :]]
            o_ref[0, p] = _pair_output(_values_stage(p_bufs[slot], vals))

        _pipeline3(
            N_PAIRS, scores,
            lambda p, slot: _softmax_stage(s_bufs[slot], m_bufs[slot], p_bufs[slot]),
            values)


def _na_bias_table(rpb, n_rows):
    heads = rpb.shape[0]
    kr_rows = min(NA_ROWS_MAX, n_rows)
    variants = (0, 1, n_rows // NA_Q_ROWS - 1)
    n_tiles = len(variants)
    tt = jnp.array(variants)[:, None, None]
    qr = jnp.arange(NA_Q_ROWS)[None, :, None]
    kr = jnp.arange(NA_K_ROWS)[None, None, :]
    r = NA_Q_ROWS * tt + qr
    rp = jnp.clip(NA_Q_ROWS * tt - NA_ROWS_MAX // 2, 0, n_rows - NA_K_ROWS) + kr
    r0 = jnp.clip(r - kr_rows // 2, 0, n_rows - kr_rows)
    row_ok = (rp >= r0) & (rp < r0 + kr_rows)
    row_idx = jnp.clip(rp - r + NA_ROWS_MAX - 1, 0, 2 * NA_ROWS_MAX - 2)
    c = jnp.arange(GRID_W)[:, None]
    cp = jnp.arange(GRID_W)[None, :]
    cs = jnp.clip(c - NA_COLS // 2, 0, GRID_W - NA_COLS)
    col_ok = (cp >= cs) & (cp < cs + NA_COLS)
    col_idx = jnp.clip(cp - c, -(NA_COLS - 1), NA_COLS - 1) + NA_COLS - 1
    rows_sel = rpb[:, row_idx, :]
    onehot = (col_idx.reshape(-1)[None, :] == jnp.arange(2 * NA_COLS - 1)[:, None]).astype(F32)
    tab = jnp.dot(rows_sel.reshape(-1, 2 * NA_COLS - 1), onehot, precision=lax.Precision.HIGHEST)
    tab = tab.reshape(heads, n_tiles, NA_Q_ROWS, NA_K_ROWS, GRID_W, GRID_W)
    ok = row_ok[None, :, :, :, None, None] & col_ok[None, None, None, None, :, :]
    tab = jnp.where(ok, tab * LOG2E, MASK_VALUE)
    tab = tab.transpose(1, 0, 2, 4, 3, 5)
    return tab.reshape(n_tiles, heads, TOK_TILE, NA_BAND).astype(BF16)


def _na_attention(q, k, v, bias_tab, *, nct, n_ctx):
    bsz, _, n_tok, _ = q.shape
    nt = n_tok // TOK_TILE
    heads = bias_tab.shape[1]
    n_rows = (n_tok - n_ctx) // GRID_W
    tile = pl.BlockSpec((1, N_PAIRS, TOK_TILE, LANES), lambda b, t: (b, 0, t, 0))
    whole_k = pl.BlockSpec((1, N_PAIRS, n_tok, LANES), lambda b, t: (b, 0, 0, 0))
    whole_v = pl.BlockSpec((1, N_PAIRS, n_tok, MXU_DIM), lambda b, t: (b, 0, 0, 0))
    last = nt - nct - 1

    def bias_index(b, t):
        tile = t - nct
        return (jnp.where(tile <= 0, 0, jnp.where(tile == last, 2, 1)), 0, 0, 0)

    bias_spec = pl.BlockSpec((1, heads, TOK_TILE, NA_BAND), bias_index)
    scratch = _attn_scratch(n_ctx + NA_BAND)
    return pl.pallas_call(
        functools.partial(_na_attn_kernel, nct=nct, n_ctx=n_ctx, n_rows=n_rows),
        grid=(bsz, nt),
        in_specs=[bias_spec, tile, whole_k, whole_v],
        out_specs=tile,
        out_shape=jax.ShapeDtypeStruct(q.shape, BF16),
        scratch_shapes=scratch,
        compiler_params=_params(2),
        name="neighbourhood_attention",
    )(bias_tab, q, k, v)


LRU_CHUNK = 256
LRU_PAD = 8


def _softplus(z):
    return jnp.maximum(z, 0.0) + jnp.log1p(jnp.exp(-jnp.abs(z)))


def _sqrt_nonneg(z):
    return z * lax.rsqrt(jnp.maximum(z, float(jnp.finfo(F32).tiny)))


def _block_scan(a_ref, b_ref, slab, r0, reverse):
    n_blk = LRU_CHUNK // SUBLANES
    order = range(SUBLANES - 1, -1, -1) if reverse else range(SUBLANES)
    acc_a = acc_b = None
    for r in order:
        rows = pl.ds(r0 + r, n_blk, stride=SUBLANES)
        a_r = a_ref[slab, rows, :]
        b_r = b_ref[slab, rows, :]
        if acc_a is None:
            acc_a, acc_b = a_r, b_r
            continue
        acc_b = a_r * acc_b + b_r
        acc_a = a_r * acc_a
        a_ref[slab, rows, :] = acc_a
        b_ref[slab, rows, :] = acc_b


def _lru_kernel(rec_ref, gate_ref, cw_ref, cb_ref,
                fwa_ref, fwx_ref, bwa_ref, bwx_ref,
                fba_ref, fbx_ref, bba_ref, bbx_ref, flam_ref, blam_ref,
                o_ref, xp_ref, af_ref, bf_ref, ab_ref, bb_ref, *, n_ctx, n_tok):
    width = LRU_BLOCK
    xp_ref[0:LRU_PAD, :] = jnp.zeros((LRU_PAD, width), F32)
    xp_ref[LRU_PAD + n_tok:2 * LRU_PAD + n_tok, :] = jnp.zeros((LRU_PAD, width), F32)
    xp_ref[LRU_PAD:LRU_PAD + n_tok, :] = rec_ref[0]

    directions = (
        (fwa_ref, fwx_ref, fba_ref, fbx_ref, (-0.5 * LRU_C) * _softplus(-flam_ref[...]),
         af_ref, bf_ref, False),
        (bwa_ref, bwx_ref, bba_ref, bbx_ref, (-0.5 * LRU_C) * _softplus(-blam_ref[...]),
         ab_ref, bb_ref, True),
    )
    n_slabs = width // LANES
    wrow = lax.broadcasted_iota(jnp.int32, (LRU_CHUNK + 2 * LRU_PAD, 1), 0)

    for r0 in range(0, n_tok, LRU_CHUNK):
        xw = xp_ref[r0:r0 + LRU_CHUNK + 2 * LRU_PAD, :]
        if r0 == n_ctx:
            xw = jnp.where(wrow < LRU_PAD, 0.0, xw)
        if r0 + LRU_CHUNK == n_ctx:
            xw = jnp.where(wrow >= LRU_PAD + LRU_CHUNK, 0.0, xw)
        u = cb_ref[...]
        for j in range(LRU_CONV):
            off = LRU_PAD + j - LRU_CONV // 2
            u = u + xw[off:off + LRU_CHUNK, :] * cw_ref[j:j + 1, :]
        ub = u.astype(BF16)
        for wa_ref, wx_ref, ba_ref, bx_ref, decay, a_ref, b_ref, reverse in directions:
            log_a = decay * jnp.tanh(_dot(ub, wa_ref[0]) + ba_ref[...]) + decay
            i = 0.5 * jnp.tanh(_dot(ub, wx_ref[0]) + bx_ref[...]) + 0.5
            a = jnp.exp(log_a)
            b = _sqrt_nonneg(-jnp.tanh(log_a) * (a * a + 1.0)) * (i * u)
            for s in range(n_slabs):
                a_ref[s, r0:r0 + LRU_CHUNK, :] = a[:, s * LANES:(s + 1) * LANES]
                b_ref[s, r0:r0 + LRU_CHUNK, :] = b[:, s * LANES:(s + 1) * LANES]
                _block_scan(a_ref, b_ref, s, r0, reverse)

    n_all = n_tok // SUBLANES
    n_cb = n_ctx // SUBLANES

    def step(i, carry):
        cf, cr = carry
        rf = pl.multiple_of(i * SUBLANES, SUBLANES)
        jb = jnp.where(i < n_cb, n_cb - 1 - i, n_all - 1 - (i - n_cb))
        rb = pl.multiple_of(jb * SUBLANES, SUBLANES)
        hf = af_ref[:, pl.ds(rf, SUBLANES), :] * cf + bf_ref[:, pl.ds(rf, SUBLANES), :]
        bf_ref[:, pl.ds(rf, SUBLANES), :] = hf
        hb = ab_ref[:, pl.ds(rb, SUBLANES), :] * cr + bb_ref[:, pl.ds(rb, SUBLANES), :]
        bb_ref[:, pl.ds(rb, SUBLANES), :] = hb
        return (jnp.broadcast_to(hf[:, SUBLANES - 1:SUBLANES, :], hf.shape),
                jnp.broadcast_to(hb[:, 0:1, :], hb.shape))

    zero = jnp.zeros((n_slabs, SUBLANES, LANES), F32)
    lax.fori_loop(0, n_all, step, (zero, zero))

    gate = gate_ref[0]
    for s in range(n_slabs):
        y = (bf_ref[s] + bb_ref[s]) * gate[:, s * LANES:(s + 1) * LANES].astype(F32)
        o_ref[0, s] = y.astype(BF16)


def _lru_mixer(rec, gate, p, *, n_ctx):
    (conv_w, conv_b, f_wa, f_ba, f_wx, f_bx, f_lam, b_wa, b_ba, b_wx, b_bx, b_lam) = p
    bsz, n_tok, d = rec.shape
    n_blocks = d // LRU_BLOCK
    seq = pl.BlockSpec((1, n_tok, LRU_BLOCK), lambda b, n: (b, 0, n))
    vec = pl.BlockSpec((1, LRU_BLOCK), lambda b, n: (0, n))
    mat = pl.BlockSpec((1, LRU_BLOCK, LRU_BLOCK), lambda b, n: (n, 0, 0))
    row = lambda a: a.reshape(1, d)
    half = lambda a: 0.5 * a
    assert n_ctx % LRU_CHUNK == 0 and n_tok % LRU_CHUNK == 0
    scratch = [pltpu.VMEM((n_tok + 2 * LRU_PAD, LRU_BLOCK), F32)]
    scratch += [pltpu.VMEM((LRU_BLOCK // LANES, n_tok, LANES), F32)] * 4
    return pl.pallas_call(
        functools.partial(_lru_kernel, n_ctx=n_ctx, n_tok=n_tok),
        grid=(bsz, n_blocks),
        in_specs=[seq, seq, pl.BlockSpec((LRU_CONV, LRU_BLOCK), lambda b, n: (0, n)), vec,
                  mat, mat, mat, mat, vec, vec, vec, vec, vec, vec],
        out_specs=pl.BlockSpec((1, LRU_BLOCK // LANES, n_tok, LANES), lambda b, n: (b, n, 0, 0)),
        out_shape=jax.ShapeDtypeStruct((bsz, d // LANES, n_tok, LANES), BF16),
        scratch_shapes=scratch,
        compiler_params=_params(2),
        name="rglru_scan",
    )(rec, gate, conv_w, row(conv_b),
      half(f_wa).astype(BF16), half(f_wx).astype(BF16),
      half(b_wa).astype(BF16), half(b_wx).astype(BF16),
      row(half(f_ba)), row(half(f_bx)), row(half(b_ba)), row(half(b_bx)), row(f_lam), row(b_lam))


FF_CHUNK = 1024


def _post_kernel(o_ref, *refs, n_streams, nct):
    x_refs = refs[:n_streams]
    mod_ref, ng_ref, wo_ref, w1_ref, w2_ref, out_ref = refs[n_streams:]
    for i in range(BATCH_BLOCK):
        m = mod_ref[i, 0]
        o = jnp.concatenate([o_ref[i, p] for p in range(N_PAIRS)], axis=-1)
        x1 = _stream_tile(x_refs, i, nct) + m[2:3, :] * _dot(o, wo_ref[...])
        h = _norm_mod(x1, ng_ref[...], m[3:4, :], m[4:5, :]).astype(BF16)
        acc = jnp.zeros_like(x1)
        for c in range(0, D_FF, FF_CHUNK):
            a = jnp.maximum(_dot(h, w1_ref[:, c:c + FF_CHUNK]), 0.0)
            acc = acc + _dot((a * a).astype(BF16), w2_ref[c:c + FF_CHUNK, :])
        out_ref[i] = x1 + m[5:6, :] * acc


def _post(o, xs, modsel, norm_g, w_o, w1, w2, *, nct, skip_ctx):
    bsz, _, d = xs[0].shape
    n_tok = sum(a.shape[1] for a in xs)
    first = nct if skip_ctx else 0
    nt = n_tok // TOK_TILE - first
    if len(xs) == 1:
        x_specs = [pl.BlockSpec((BATCH_BLOCK, TOK_TILE, d), lambda b, t: (b, t + first, 0))]
    else:
        assert not skip_ctx
        x_specs = _stream_specs(xs, nct)
    return pl.pallas_call(
        functools.partial(_post_kernel, n_streams=len(xs), nct=nct),
        grid=(bsz // BATCH_BLOCK, nt),
        in_specs=[pl.BlockSpec((BATCH_BLOCK, N_PAIRS, TOK_TILE, LANES),
                               lambda b, t: (b, 0, t + first, 0))]
        + x_specs
        + [_mod_spec(nct, first),
           _resident((1, d)), _resident((d, d)), _resident((d, D_FF)), _resident((D_FF, d))],
        out_specs=pl.BlockSpec((BATCH_BLOCK, TOK_TILE, d), lambda b, t: (b, t, 0)),
        out_shape=jax.ShapeDtypeStruct((bsz, nt * TOK_TILE, d), F32),
        compiler_params=_params(2),
        name="proj_residual_mlp",
    )(o, *xs, modsel, norm_g.reshape(1, d), w_o, w1, w2)


def _rope_tables(n_ctx, n_lat):
    t = jnp.arange(n_lat)
    row = (t // GRID_W).astype(F32)
    col = (t % GRID_W).astype(F32)
    n_freq = HEAD_DIM // 4
    inv = ROPE_THETA ** (-jnp.arange(n_freq, dtype=F32) / n_freq)
    ang = jnp.concatenate([row[:, None] * inv, col[:, None] * inv], axis=-1)
    cos = jnp.repeat(jnp.cos(ang), 2, axis=-1)
    sin = jnp.repeat(jnp.sin(ang), 2, axis=-1) * jnp.tile(jnp.array([-1.0, 1.0], F32), HEAD_DIM // 2)
    cos = jnp.concatenate([jnp.ones((n_ctx, HEAD_DIM), F32), cos], axis=0)
    sin = jnp.concatenate([jnp.zeros((n_ctx, HEAD_DIM), F32), sin], axis=0)
    return jnp.tile(cos, (1, LANES // HEAD_DIM)), jnp.tile(sin, (1, LANES // HEAD_DIM))


def kernel(x, c, ctx, c_ctx, norm1_g, norm2_g, w_mod, b_mod, w_mlp1, w_mlp2, a_w_qkv, a_q_norm_g, a_k_norm_g, a_lambda_q1, a_lambda_k1, a_lambda_q2, a_lambda_k2, a_subln_g, a_w_o, b_w_qkv, b_q_norm_g, b_k_norm_g, b_rpb, b_w_o, c_w_in, c_conv_w, c_conv_b, c_fwd_w_a, c_fwd_b_a, c_fwd_w_x, c_fwd_b_x, c_fwd_lam, c_bwd_w_a, c_bwd_b_a, c_bwd_w_x, c_bwd_b_x, c_bwd_lam, c_w_o, d_w_qkv, d_q_norm_g, d_k_norm_g, d_w_o):
    bsz, n_lat, d = x.shape
    n_ctx = ctx.shape[1]
    depth = w_mod.shape[0]
    assert d == D_MODEL and depth == 4 and n_ctx % TOK_TILE == 0 and n_lat % TOK_TILE == 0
    assert bsz % BATCH_BLOCK == 0
    assert (n_lat // GRID_W) % NA_Q_ROWS == 0 and n_lat // GRID_W >= NA_K_ROWS
    nct = n_ctx // TOK_TILE

    rows = -(-(bsz + 1) // SUBLANES) * SUBLANES
    c_all = jnp.concatenate([c, c_ctx[None, :], jnp.zeros((rows - bsz - 1, d), F32)], axis=0)
    mods = _modulation(c_all, w_mod, b_mod).reshape(depth, rows, N_MOD, d)
    mod_ctx = jnp.broadcast_to(mods[:, bsz][:, None], (depth, bsz, N_MOD, d))
    modsel = jnp.stack([mod_ctx, mods[:, :bsz]], axis=2)

    rope_tabs = _rope_tables(n_ctx, n_lat)
    w1 = w_mlp1.astype(BF16)
    w2 = w_mlp2.astype(BF16)

    xs = (ctx, x)
    q, k, v = _pre_attn(xs, modsel[0], norm1_g[0], a_w_qkv[0].astype(BF16), a_q_norm_g[0],
                        a_k_norm_g[0], rope_tabs, k_width=d, dup_kv=False, nct=nct)
    lam_vecs = jnp.stack([a_lambda_q1[0], a_lambda_k1[0], a_lambda_q2[0], a_lambda_k2[0]])
    lam_init = 0.8 - 0.6 * math.exp(-0.3 * 0)
    o = _diff_attention(q, k, v, lam_vecs, a_subln_g[0], nct=nct, n_ctx=n_ctx, lam_init=lam_init)
    xs = (_post(o, xs, modsel[0], norm2_g[0], a_w_o[0].astype(BF16), w1[0], w2[0],
                nct=nct, skip_ctx=False),)

    q, k, v = _pre_attn(xs, modsel[1], norm1_g[1], b_w_qkv[0].astype(BF16), b_q_norm_g[0],
                        b_k_norm_g[0], None, k_width=d, dup_kv=False, nct=nct)
    bias_tab = _na_bias_table(b_rpb[0], n_lat // GRID_W)
    o = _na_attention(q, k, v, bias_tab, nct=nct, n_ctx=n_ctx)
    xs = (_post(o, xs, modsel[1], norm2_g[1], b_w_o[0].astype(BF16), w1[1], w2[1],
                nct=nct, skip_ctx=False),)

    gate, rec = _pre_lru(xs[0], modsel[2], norm1_g[2], c_w_in[0].astype(BF16), nct=nct)
    lru_p = (c_conv_w[0], c_conv_b[0], c_fwd_w_a[0], c_fwd_b_a[0], c_fwd_w_x[0], c_fwd_b_x[0],
             c_fwd_lam[0], c_bwd_w_a[0], c_bwd_b_a[0], c_bwd_w_x[0], c_bwd_b_x[0], c_bwd_lam[0])
    o = _lru_mixer(rec, gate, lru_p, n_ctx=n_ctx)
    xs = (_post(o, xs, modsel[2], norm2_g[2], c_w_o[0].astype(BF16), w1[2], w2[2],
                nct=nct, skip_ctx=False),)

    q, k, v = _pre_attn(xs, modsel[3], norm1_g[3], d_w_qkv[0].astype(BF16), d_q_norm_g[0],
                        d_k_norm_g[0], rope_tabs, k_width=GQA_KV_HEADS * HEAD_DIM, dup_kv=True,
                        nct=nct)
    o = _gqa_attention(q, k, v, nct=nct)
    return _post(o, xs, modsel[3], norm2_g[3], d_w_o[0].astype(BF16), w1[3], w2[3],
                 nct=nct, skip_ctx=True)
```

```python
import functools
import math

import jax
import jax.numpy as jnp
from jax import lax
from jax.experimental import pallas as pl
from jax.experimental.pallas import tpu as pltpu

F32 = jnp.float32
BF16 = jnp.bfloat16

D_MODEL = 1024
HEAD_DIM = 64
GRID_W = 64
N_MOD = 6
ATTN_SCALE = HEAD_DIM ** -0.5
LOG2E = math.log2(math.e)
ROPE_THETA = 10000.0
NORM_EPS = 1e-6
DA_HEADS = D_MODEL // (2 * HEAD_DIM)
NA_ROWS_MAX = 8
NA_COLS = 16
LRU_BLOCK = 256
LRU_CONV = 4
LRU_C = 8.0
GQA_KV_HEADS = 4
D_FF = 4 * D_MODEL

LANES = 128
SUBLANES = 8
MXU_DIM = 256
VMEM_LIMIT = 56 * 1024 * 1024

TOK_TILE = 256
BATCH_BLOCK = 4
N_PAIRS = D_MODEL // LANES
MASK_VALUE = -1e30
NA_Q_ROWS = TOK_TILE // GRID_W
NA_K_ROWS = 12
NA_BAND = NA_K_ROWS * GRID_W


def _params(n_axes):
    return pltpu.CompilerParams(
        dimension_semantics=("arbitrary",) * n_axes, vmem_limit_bytes=VMEM_LIMIT)


def _resident(shape):
    zeros = (0,) * len(shape)
    return pl.BlockSpec(shape, lambda *_: zeros, pipeline_mode=pl.Buffered(1))


def _dot(a, b):
    return jnp.dot(a, b, preferred_element_type=F32)


def _dot_nt(a, b):
    return lax.dot_general(a, b, (((1,), (1,)), ((), ())), preferred_element_type=F32)


def _norm_mod(x, gain, shift, scale):
    y = x * lax.rsqrt(jnp.mean(x * x, axis=-1, keepdims=True) + NORM_EPS) * gain
    return y * (1.0 + scale) + shift


def _group_mean_matrix():
    r = lax.broadcasted_iota(jnp.int32, (MXU_DIM, MXU_DIM), 0) // HEAD_DIM
    c = lax.broadcasted_iota(jnp.int32, (MXU_DIM, MXU_DIM), 1) // HEAD_DIM
    return jnp.where(r == c, 1.0 / HEAD_DIM, 0.0).astype(BF16)


def _head_rms(x, gain, mean_mat):
    sq = (x * x).astype(BF16)
    ms = jnp.concatenate(
        [_dot(sq[:, j:j + MXU_DIM], mean_mat) for j in range(0, x.shape[1], MXU_DIM)], axis=-1)
    return x * lax.rsqrt(ms + NORM_EPS) * gain


def _rope(x, cos, sin):
    width = x.shape[1]
    reps = width // LANES
    c = jnp.concatenate([cos] * reps, axis=-1)
    s = jnp.concatenate([sin] * reps, axis=-1)
    lane = lax.broadcasted_iota(jnp.int32, x.shape, 1)
    partner = jnp.where((lane & 1) == 0, pltpu.roll(x, width - 1, 1), pltpu.roll(x, 1, 1))
    return x * c + partner * s


def _mods_kernel(c_ref, w_ref, b_ref, o_ref):
    c = c_ref[...]
    a = (c * jax.nn.sigmoid(c)).astype(BF16)
    o_ref[0] = _dot(a, w_ref[0].astype(BF16)) + b_ref[0]


def _modulation(c_all, w_mod, b_mod):
    depth, d, n = w_mod.shape
    rows = c_all.shape[0]
    tn = n // 4
    return pl.pallas_call(
        _mods_kernel,
        grid=(depth, n // tn),
        in_specs=[pl.BlockSpec((rows, d), lambda l, j: (0, 0)),
                  pl.BlockSpec((1, d, tn), lambda l, j: (l, 0, j)),
                  pl.BlockSpec((1, 1, tn), lambda l, j: (l, 0, j))],
        out_specs=pl.BlockSpec((1, rows, tn), lambda l, j: (l, 0, j)),
        out_shape=jax.ShapeDtypeStruct((depth, rows, n), F32),
        compiler_params=_params(2),
        name="adaln_modulation",
    )(c_all, w_mod, b_mod.reshape(depth, 1, n))


def _stream_tile(refs, i, nct):
    if len(refs) == 1:
        return refs[0][i]
    return jnp.where(pl.program_id(1) < nct, refs[0][i], refs[1][i])


def _stream_specs(xs, nct):
    d = xs[0].shape[-1]
    if len(xs) == 1:
        return [pl.BlockSpec((BATCH_BLOCK, TOK_TILE, d), lambda b, t: (b, t, 0))]
    return [pl.BlockSpec((BATCH_BLOCK, TOK_TILE, d), lambda b, t: (b, jnp.minimum(t, nct - 1), 0)),
            pl.BlockSpec((BATCH_BLOCK, TOK_TILE, d), lambda b, t: (b, jnp.maximum(t - nct, 0), 0))]


def _pre_attn_kernel(*refs, **static):
    for i in range(BATCH_BLOCK):
        _pre_attn_sample(i, *refs, **static)


def _pre_attn_sample(i, *refs, n_streams, nct, k_width, rope, dup_kv):
    x_refs, refs = refs[:n_streams], refs[n_streams:]
    if rope:
        (mod_ref, ng_ref, w_ref, qg_ref, kg_ref, cos_ref, sin_ref, q_ref, k_ref, v_ref) = refs
    else:
        mod_ref, ng_ref, w_ref, qg_ref, kg_ref, q_ref, k_ref, v_ref = refs
    q_ref, k_ref, v_ref = q_ref.at[i], k_ref.at[i], v_ref.at[i]
    m = mod_ref[i, 0]
    h = _norm_mod(_stream_tile(x_refs, i, nct), ng_ref[...], m[0:1, :], m[1:2, :]).astype(BF16)
    mean_mat = _group_mean_matrix()
    d = D_MODEL
    q = _head_rms(_dot(h, w_ref[:, 0:d]), qg_ref[...], mean_mat)
    k = _head_rms(_dot(h, w_ref[:, d:d + k_width]), kg_ref[...], mean_mat)
    v = _dot(h, w_ref[:, d + k_width:d + 2 * k_width])
    if rope:
        q = _rope(q, cos_ref[...], sin_ref[...])
        k = _rope(k, cos_ref[...], sin_ref[...])
    q = q * (ATTN_SCALE * LOG2E)
    for p in range(N_PAIRS):
        q_ref[p] = q[:, p * LANES:(p + 1) * LANES].astype(BF16)
    if dup_kv:
        for g in range(k_width // HEAD_DIM):
            kh = k[:, g * HEAD_DIM:(g + 1) * HEAD_DIM]
            k_ref[g] = jnp.concatenate([kh, kh], axis=-1).astype(BF16)
    else:
        for p in range(k_width // LANES):
            k_ref[p] = k[:, p * LANES:(p + 1) * LANES].astype(BF16)
    fill = jnp.ones((v.shape[0], LANES), F32)
    if dup_kv:
        for g in range(k_width // HEAD_DIM):
            vh = v[:, g * HEAD_DIM:(g + 1) * HEAD_DIM]
            v_ref[g] = jnp.concatenate([vh, vh, fill], axis=-1).astype(BF16)
    else:
        for p in range(k_width // LANES):
            v_ref[p] = jnp.concatenate(
                [v[:, p * LANES:(p + 1) * LANES], fill], axis=-1).astype(BF16)


def _mod_spec(nct, first=0):
    return pl.BlockSpec((BATCH_BLOCK, 1, N_MOD, D_MODEL),
                        lambda b, t: (b, jnp.where(t + first >= nct, 1, 0), 0, 0))


def _pre_attn(xs, modsel, norm_g, w_qkv, q_gain, k_gain, rope_tabs, *, k_width, dup_kv, nct):
    bsz, _, d = xs[0].shape
    n_tok = sum(a.shape[1] for a in xs)
    nt = n_tok // TOK_TILE
    n_w = w_qkv.shape[1]
    k_slots = k_width // HEAD_DIM if dup_kv else k_width // LANES
    rope = rope_tabs is not None
    in_specs = _stream_specs(xs, nct) + [
        _mod_spec(nct),
        _resident((1, d)),
        _resident((d, n_w)),
        _resident((1, d)),
        _resident((1, k_width))]
    args = list(xs) + [modsel, norm_g.reshape(1, d), w_qkv,
                       jnp.tile(q_gain, d // HEAD_DIM).reshape(1, d),
                       jnp.tile(k_gain, k_width // HEAD_DIM).reshape(1, k_width)]
    if rope:
        in_specs += [pl.BlockSpec((TOK_TILE, LANES), lambda b, t: (t, 0))] * 2
        args += list(rope_tabs)
    qo = jax.ShapeDtypeStruct((bsz, N_PAIRS, n_tok, LANES), BF16)
    ko = jax.ShapeDtypeStruct((bsz, k_slots, n_tok, LANES), BF16)
    vo = jax.ShapeDtypeStruct((bsz, k_slots, n_tok, MXU_DIM), BF16)
    return pl.pallas_call(
        functools.partial(_pre_attn_kernel, n_streams=len(xs), nct=nct, k_width=k_width,
                          rope=rope, dup_kv=dup_kv),
        grid=(bsz // BATCH_BLOCK, nt),
        in_specs=in_specs,
        out_specs=[
            pl.BlockSpec((BATCH_BLOCK, N_PAIRS, TOK_TILE, LANES), lambda b, t: (b, 0, t, 0)),
            pl.BlockSpec((BATCH_BLOCK, k_slots, TOK_TILE, LANES), lambda b, t: (b, 0, t, 0)),
            pl.BlockSpec((BATCH_BLOCK, k_slots, TOK_TILE, MXU_DIM), lambda b, t: (b, 0, t, 0))],
        out_shape=[qo, ko, vo],
        compiler_params=_params(2),
        name="norm_mod_qkv",
    )(*args)


def _pre_lru_kernel(x_ref, mod_ref, ng_ref, w_ref, gate_ref, rec_ref):
    d = D_MODEL
    for i in range(BATCH_BLOCK):
        m = mod_ref[i, 0]
        h = _norm_mod(x_ref[i], ng_ref[...], m[0:1, :], m[1:2, :]).astype(BF16)
        gate_ref[i] = jax.nn.gelu(_dot(h, w_ref[:, 0:d])).astype(BF16)
        rec_ref[i] = _dot(h, w_ref[:, d:2 * d])


def _pre_lru(xs, modsel, norm_g, w_in, *, nct):
    bsz, n_tok, d = xs.shape
    nt = n_tok // TOK_TILE
    tok_spec = pl.BlockSpec((BATCH_BLOCK, TOK_TILE, d), lambda b, t: (b, t, 0))
    return pl.pallas_call(
        _pre_lru_kernel,
        grid=(bsz // BATCH_BLOCK, nt),
        in_specs=[tok_spec, _mod_spec(nct), _resident((1, d)), _resident((d, 2 * d))],
        out_specs=[tok_spec, tok_spec],
        out_shape=[jax.ShapeDtypeStruct((bsz, n_tok, d), BF16),
                   jax.ShapeDtypeStruct((bsz, n_tok, d), F32)],
        compiler_params=_params(2),
        name="norm_mod_lru_in",
    )(xs, modsel, norm_g.reshape(1, d), w_in)


def _split_pair(q2):
    low = lax.broadcasted_iota(jnp.int32, q2.shape, 1) < HEAD_DIM
    zero = jnp.zeros_like(q2)
    return jnp.where(low, q2, zero), jnp.where(low, zero, q2)


def _exp2_parts(parts):
    m = functools.reduce(jnp.maximum, [jnp.max(s, axis=-1, keepdims=True) for s in parts])
    return [jnp.exp2(s - m) for s in parts]


def _pipeline3(n_units, scores, softmax, values):
    assert n_units % 2 == 0 and n_units >= 4
    scores(0, 0)
    scores(1, 1)
    softmax(0, 0)
    for u in range(2, n_units, 2):
        scores(u, 0)
        softmax(u - 1, 1)
        values(u - 2, 0)
        scores(u + 1, 1)
        softmax(u, 0)
        values(u - 1, 1)
    softmax(n_units - 1, 1)
    values(n_units - 2, 0)
    values(n_units - 1, 1)


def _sub_layer_norm(o, gain, lam_init):
    o = o * lax.rsqrt(jnp.mean(o * o, axis=-1, keepdims=True) + NORM_EPS) * gain
    return (o * (1.0 - lam_init)).astype(BF16)


def _scores_stage(s_buf, m_buf, q2, key_parts, bias_parts):
    for half, qh in enumerate(_split_pair(q2)):
        m, col = None, 0
        for kp, bp in zip(key_parts, bias_parts):
            s = _dot_nt(qh, kp)
            if bp is not None:
                s = s + bp[half].astype(F32)
            s_buf[half, :, col:col + kp.shape[0]] = s
            part_max = jnp.max(s, axis=-1, keepdims=True)
            m = part_max if m is None else jnp.maximum(m, part_max)
            col += kp.shape[0]
        m_buf[half] = jnp.broadcast_to(m, m_buf.shape[1:])


def _softmax_stage(s_buf, m_buf, p_buf):
    for half in range(2):
        p_buf[half] = jnp.exp2(s_buf[half] - m_buf[half][:, 0:1]).astype(BF16)


def _values_stage(p_buf, value_parts):
    outs = []
    for half in range(2):
        acc, col = None, 0
        for vp in value_parts:
            part = _dot(p_buf[half, :, col:col + vp.shape[0]], vp)
            acc = part if acc is None else acc + part
            col += vp.shape[0]
        outs.append(acc[:, :LANES] / acc[:, LANES:])
    return outs


def _direct_pair(q2, k2, v2):
    outs = []
    for qh in _split_pair(q2):
        (e,) = _exp2_parts([_dot_nt(qh, k2)])
        acc = _dot(e.astype(BF16), v2)
        outs.append(acc[:, :LANES] / acc[:, LANES:])
    return outs


def _attn_scratch(n_cols):
    scratch = [pltpu.VMEM((2, TOK_TILE, n_cols), F32)] * 2
    scratch += [pltpu.VMEM((2, TOK_TILE, LANES), F32)] * 2
    scratch += [pltpu.VMEM((2, TOK_TILE, n_cols), BF16)] * 2
    return scratch


def _diff_attn_kernel(lam_ref, sg_ref, q_ref, k_ref, v_ref, o_ref, s0, s1, m0, m1, p0, p1,
                      *, nct, n_ctx, lam_init):
    t = pl.program_id(1)
    lv = lam_ref[...]
    lam = (jnp.exp(jnp.sum(lv[0:1, :] * lv[1:2, :], axis=-1, keepdims=True))
           - jnp.exp(jnp.sum(lv[2:3, :] * lv[3:4, :], axis=-1, keepdims=True)) + lam_init)

    def finish(h, outs):
        o_ref[0, h] = _sub_layer_norm(outs[0] - lam * outs[1], sg_ref[...], lam_init)

    @pl.when(t < nct)
    def _():
        for h in range(DA_HEADS):
            finish(h, _direct_pair(q_ref[0, h], k_ref[0, h, 0:n_ctx, :], v_ref[0, h, 0:n_ctx, :]))

    @pl.when(t >= nct)
    def _():
        s_bufs, m_bufs, p_bufs = (s0, s1), (m0, m1), (p0, p1)
        _pipeline3(
            DA_HEADS,
            lambda h, slot: _scores_stage(s_bufs[slot], m_bufs[slot], q_ref[0, h],
                                          [k_ref[0, h]], [None]),
            lambda h, slot: _softmax_stage(s_bufs[slot], m_bufs[slot], p_bufs[slot]),
            lambda h, slot: finish(h, _values_stage(p_bufs[slot], [v_ref[0, h]])))


def _diff_attention(q, k, v, lam_vecs, subln_g, *, nct, n_ctx, lam_init):
    bsz, _, n_tok, _ = q.shape
    nt = n_tok // TOK_TILE
    tile = pl.BlockSpec((1, N_PAIRS, TOK_TILE, LANES), lambda b, t: (b, 0, t, 0))
    whole_k = pl.BlockSpec((1, DA_HEADS, n_tok, LANES), lambda b, t: (b, 0, 0, 0))
    whole_v = pl.BlockSpec((1, DA_HEADS, n_tok, MXU_DIM), lambda b, t: (b, 0, 0, 0))
    scratch = _attn_scratch(n_tok)
    return pl.pallas_call(
        functools.partial(_diff_attn_kernel, nct=nct, n_ctx=n_ctx, lam_init=lam_init),
        grid=(bsz, nt),
        in_specs=[_resident((4, HEAD_DIM)), _resident((1, LANES)), tile, whole_k, whole_v],
        out_specs=tile,
        out_shape=jax.ShapeDtypeStruct(q.shape, BF16),
        scratch_shapes=scratch,
        compiler_params=_params(2),
        name="diff_attention",
    )(lam_vecs, subln_g.reshape(1, LANES), q, k, v)


def _pair_output(outs):
    low = lax.broadcasted_iota(jnp.int32, outs[0].shape, 1) < HEAD_DIM
    return jnp.where(low, outs[0], outs[1]).astype(BF16)


def _gqa_attn_kernel(q_ref, k_ref, v_ref, o_ref, s0, s1, m0, m1, p0, p1):
    s_bufs, m_bufs, p_bufs = (s0, s1), (m0, m1), (p0, p1)

    def values(p, slot):
        o_ref[0, p] = _pair_output(_values_stage(p_bufs[slot], [v_ref[0, p // 2]]))

    _pipeline3(
        N_PAIRS,
        lambda p, slot: _scores_stage(s_bufs[slot], m_bufs[slot], q_ref[0, p],
                                      [k_ref[0, p // 2]], [None]),
        lambda p, slot: _softmax_stage(s_bufs[slot], m_bufs[slot], p_bufs[slot]),
        values)


def _gqa_attention(q, k, v, *, nct):
    bsz, _, n_tok, _ = q.shape
    nt = n_tok // TOK_TILE - nct
    tile = pl.BlockSpec((1, N_PAIRS, TOK_TILE, LANES), lambda b, t: (b, 0, t + nct, 0))
    whole_k = pl.BlockSpec((1, GQA_KV_HEADS, n_tok, LANES), lambda b, t: (b, 0, 0, 0))
    whole_v = pl.BlockSpec((1, GQA_KV_HEADS, n_tok, MXU_DIM), lambda b, t: (b, 0, 0, 0))
    scratch = _attn_scratch(n_tok)
    return pl.pallas_call(
        _gqa_attn_kernel,
        grid=(bsz, nt),
        in_specs=[tile, whole_k, whole_v],
        out_specs=tile,
        out_shape=jax.ShapeDtypeStruct(q.shape, BF16),
        scratch_shapes=scratch,
        compiler_params=_params(2),
        name="gqa_attention",
    )(q, k, v)


def _na_attn_kernel(bias_ref, q_ref, k_ref, v_ref, o_ref, s0, s1, m0, m1, p0, p1,
                    *, nct, n_ctx, n_rows):
    t = pl.program_id(1)

    @pl.when(t < nct)
    def _():
        for p in range(N_PAIRS):
            o_ref[0, p] = _pair_output(_direct_pair(
                q_ref[0, p], k_ref[0, p, 0:n_ctx, :], v_ref[0, p, 0:n_ctx, :]))

    @pl.when(t >= nct)
    def _():
        first_row = jnp.clip(NA_Q_ROWS * (t - nct) - NA_ROWS_MAX // 2, 0, n_rows - NA_K_ROWS)
        start = pl.multiple_of(n_ctx + first_row * GRID_W, GRID_W)
        s_bufs, m_bufs, p_bufs = (s0, s1), (m0, m1), (p0, p1)

        def scores(p, slot):
            keys = [k_ref[0, p, pl.ds(0, n_ctx), :], k_ref[0, p, pl.ds(start, NA_BAND), :]]
            bias = (bias_ref[0, 2 * p], bias_ref[0, 2 * p + 1])
            _scores_stage(s_bufs[slot], m_bufs[slot], q_ref[0, p], keys, [None, bias])

        def values(p, slot):
            vals = [v_ref[0, p, pl.ds(0, n_ctx), :], v_ref[0, p, pl.ds(start, NA_BAND), :]]
            o_ref[0, p] = _pair_output(_values_stage(p_bufs[slot], vals))

        _pipeline3(
            N_PAIRS, scores,
            lambda p, slot: _softmax_stage(s_bufs[slot], m_bufs[slot], p_bufs[slot]),
            values)


def _na_bias_table(rpb, n_rows):
    heads = rpb.shape[0]
    kr_rows = min(NA_ROWS_MAX, n_rows)
    variants = (0, 1, n_rows // NA_Q_ROWS - 1)
    n_tiles = len(variants)
    tt = jnp.array(variants)[:, None, None]
    qr = jnp.arange(NA_Q_ROWS)[None, :, None]
    kr = jnp.arange(NA_K_ROWS)[None, None, :]
    r = NA_Q_ROWS * tt + qr
    rp = jnp.clip(NA_Q_ROWS * tt - NA_ROWS_MAX // 2, 0, n_rows - NA_K_ROWS) + kr
    r0 = jnp.clip(r - kr_rows // 2, 0, n_rows - kr_rows)
    row_ok = (rp >= r0) & (rp < r0 + kr_rows)
    row_idx = jnp.clip(rp - r + NA_ROWS_MAX - 1, 0, 2 * NA_ROWS_MAX - 2)
    c = jnp.arange(GRID_W)[:, None]
    cp = jnp.arange(GRID_W)[None, :]
    cs = jnp.clip(c - NA_COLS // 2, 0, GRID_W - NA_COLS)
    col_ok = (cp >= cs) & (cp < cs + NA_COLS)
    col_idx = jnp.clip(cp - c, -(NA_COLS - 1), NA_COLS - 1) + NA_COLS - 1
    rows_sel = rpb[:, row_idx, :]
    onehot = (col_idx.reshape(-1)[None, :] == jnp.arange(2 * NA_COLS - 1)[:, None]).astype(F32)
    tab = jnp.dot(rows_sel.reshape(-1, 2 * NA_COLS - 1), onehot, precision=lax.Precision.HIGHEST)
    tab = tab.reshape(heads, n_tiles, NA_Q_ROWS, NA_K_ROWS, GRID_W, GRID_W)
    ok = row_ok[None, :, :, :, None, None] & col_ok[None, None, None, None, :, :]
    tab = jnp.where(ok, tab * LOG2E, MASK_VALUE)
    tab = tab.transpose(1, 0, 2, 4, 3, 5)
    return tab.reshape(n_tiles, heads, TOK_TILE, NA_BAND).astype(BF16)


def _na_attention(q, k, v, bias_tab, *, nct, n_ctx):
    bsz, _, n_tok, _ = q.shape
    nt = n_tok // TOK_TILE
    heads = bias_tab.shape[1]
    n_rows = (n_tok - n_ctx) // GRID_W
    tile = pl.BlockSpec((1, N_PAIRS, TOK_TILE, LANES), lambda b, t: (b, 0, t, 0))
    whole_k = pl.BlockSpec((1, N_PAIRS, n_tok, LANES), lambda b, t: (b, 0, 0, 0))
    whole_v = pl.BlockSpec((1, N_PAIRS, n_tok, MXU_DIM), lambda b, t: (b, 0, 0, 0))
    last = nt - nct - 1

    def bias_index(b, t):
        tile = t - nct
        return (jnp.where(tile <= 0, 0, jnp.where(tile == last, 2, 1)), 0, 0, 0)

    bias_spec = pl.BlockSpec((1, heads, TOK_TILE, NA_BAND), bias_index)
    scratch = _attn_scratch(n_ctx + NA_BAND)
    return pl.pallas_call(
        functools.partial(_na_attn_kernel, nct=nct, n_ctx=n_ctx, n_rows=n_rows),
        grid=(bsz, nt),
        in_specs=[bias_spec, tile, whole_k, whole_v],
        out_specs=tile,
        out_shape=jax.ShapeDtypeStruct(q.shape, BF16),
        scratch_shapes=scratch,
        compiler_params=_params(2),
        name="neighbourhood_attention",
    )(bias_tab, q, k, v)


LRU_CHUNK = 256
LRU_PAD = 8


def _softplus(z):
    return jnp.maximum(z, 0.0) + jnp.log1p(jnp.exp(-jnp.abs(z)))


def _sqrt_nonneg(z):
    return z * lax.rsqrt(jnp.maximum(z, float(jnp.finfo(F32).tiny)))


def _block_scan(a_ref, b_ref, slab, r0, reverse):
    n_blk = LRU_CHUNK // SUBLANES
    order = range(SUBLANES - 1, -1, -1) if reverse else range(SUBLANES)
    acc_a = acc_b = None
    for r in order:
        rows = pl.ds(r0 + r, n_blk, stride=SUBLANES)
        a_r = a_ref[slab, rows, :]
        b_r = b_ref[slab, rows, :]
        if acc_a is None:
            acc_a, acc_b = a_r, b_r
            continue
        acc_b = a_r * acc_b + b_r
        acc_a = a_r * acc_a
        a_ref[slab, rows, :] = acc_a
        b_ref[slab, rows, :] = acc_b


def _lru_kernel(rec_ref, gate_ref, cw_ref, cb_ref,
                fwa_ref, fwx_ref, bwa_ref, bwx_ref,
                fba_ref, fbx_ref, bba_ref, bbx_ref, flam_ref, blam_ref,
                o_ref, xp_ref, af_ref, bf_ref, ab_ref, bb_ref, *, n_ctx, n_tok):
    width = LRU_BLOCK
    xp_ref[0:LRU_PAD, :] = jnp.zeros((LRU_PAD, width), F32)
    xp_ref[LRU_PAD + n_tok:2 * LRU_PAD + n_tok, :] = jnp.zeros((LRU_PAD, width), F32)
    xp_ref[LRU_PAD:LRU_PAD + n_tok, :] = rec_ref[0]

    directions = (
        (fwa_ref, fwx_ref, fba_ref, fbx_ref, (-0.5 * LRU_C) * _softplus(-flam_ref[...]),
         af_ref, bf_ref, False),
        (bwa_ref, bwx_ref, bba_ref, bbx_ref, (-0.5 * LRU_C) * _softplus(-blam_ref[...]),
         ab_ref, bb_ref, True),
    )
    n_slabs = width // LANES
    wrow = lax.broadcasted_iota(jnp.int32, (LRU_CHUNK + 2 * LRU_PAD, 1), 0)

    for r0 in range(0, n_tok, LRU_CHUNK):
        xw = xp_ref[r0:r0 + LRU_CHUNK + 2 * LRU_PAD, :]
        if r0 == n_ctx:
            xw = jnp.where(wrow < LRU_PAD, 0.0, xw)
        if r0 + LRU_CHUNK == n_ctx:
            xw = jnp.where(wrow >= LRU_PAD + LRU_CHUNK, 0.0, xw)
        u = cb_ref[...]
        for j in range(LRU_CONV):
            off = LRU_PAD + j - LRU_CONV // 2
            u = u + xw[off:off + LRU_CHUNK, :] * cw_ref[j:j + 1, :]
        ub = u.astype(BF16)
        for wa_ref, wx_ref, ba_ref, bx_ref, decay, a_ref, b_ref, reverse in directions:
            log_a = decay * jnp.tanh(_dot(ub, wa_ref[0]) + ba_ref[...]) + decay
            i = 0.5 * jnp.tanh(_dot(ub, wx_ref[0]) + bx_ref[...]) + 0.5
            a = jnp.exp(log_a)
            b = _sqrt_nonneg(-jnp.tanh(log_a) * (a * a + 1.0)) * (i * u)
            for s in range(n_slabs):
                a_ref[s, r0:r0 + LRU_CHUNK, :] = a[:, s * LANES:(s + 1) * LANES]
                b_ref[s, r0:r0 + LRU_CHUNK, :] = b[:, s * LANES:(s + 1) * LANES]
                _block_scan(a_ref, b_ref, s, r0, reverse)

    n_all = n_tok // SUBLANES
    n_cb = n_ctx // SUBLANES

    def step(i, carry):
        cf, cr = carry
        rf = pl.multiple_of(i * SUBLANES, SUBLANES)
        jb = jnp.where(i < n_cb, n_cb - 1 - i, n_all - 1 - (i - n_cb))
        rb = pl.multiple_of(jb * SUBLANES, SUBLANES)
        hf = af_ref[:, pl.ds(rf, SUBLANES), :] * cf + bf_ref[:, pl.ds(rf, SUBLANES), :]
        bf_ref[:, pl.ds(rf, SUBLANES), :] = hf
        hb = ab_ref[:, pl.ds(rb, SUBLANES), :] * cr + bb_ref[:, pl.ds(rb, SUBLANES), :]
        bb_ref[:, pl.ds(rb, SUBLANES), :] = hb
        return (jnp.broadcast_to(hf[:, SUBLANES - 1:SUBLANES, :], hf.shape),
                jnp.broadcast_to(hb[:, 0:1, :], hb.shape))

    zero = jnp.zeros((n_slabs, SUBLANES, LANES), F32)
    lax.fori_loop(0, n_all, step, (zero, zero))

    gate = gate_ref[0]
    for s in range(n_slabs):
        y = (bf_ref[s] + bb_ref[s]) * gate[:, s * LANES:(s + 1) * LANES].astype(F32)
        o_ref[0, s] = y.astype(BF16)


def _lru_mixer(rec, gate, p, *, n_ctx):
    (conv_w, conv_b, f_wa, f_ba, f_wx, f_bx, f_lam, b_wa, b_ba, b_wx, b_bx, b_lam) = p
    bsz, n_tok, d = rec.shape
    n_blocks = d // LRU_BLOCK
    seq = pl.BlockSpec((1, n_tok, LRU_BLOCK), lambda b, n: (b, 0, n))
    vec = pl.BlockSpec((1, LRU_BLOCK), lambda b, n: (0, n))
    mat = pl.BlockSpec((1, LRU_BLOCK, LRU_BLOCK), lambda b, n: (n, 0, 0))
    row = lambda a: a.reshape(1, d)
    half = lambda a: 0.5 * a
    assert n_ctx % LRU_CHUNK == 0 and n_tok % LRU_CHUNK == 0
    scratch = [pltpu.VMEM((n_tok + 2 * LRU_PAD, LRU_BLOCK), F32)]
    scratch += [pltpu.VMEM((LRU_BLOCK // LANES, n_tok, LANES), F32)] * 4
    return pl.pallas_call(
        functools.partial(_lru_kernel, n_ctx=n_ctx, n_tok=n_tok),
        grid=(bsz, n_blocks),
        in_specs=[seq, seq, pl.BlockSpec((LRU_CONV, LRU_BLOCK), lambda b, n: (0, n)), vec,
                  mat, mat, mat, mat, vec, vec, vec, vec, vec, vec],
        out_specs=pl.BlockSpec((1, LRU_BLOCK // LANES, n_tok, LANES), lambda b, n: (b, n, 0, 0)),
        out_shape=jax.ShapeDtypeStruct((bsz, d // LANES, n_tok, LANES), BF16),
        scratch_shapes=scratch,
        compiler_params=_params(2),
        name="rglru_scan",
    )(rec, gate, conv_w, row(conv_b),
      half(f_wa).astype(BF16), half(f_wx).astype(BF16),
      half(b_wa).astype(BF16), half(b_wx).astype(BF16),
      row(half(f_ba)), row(half(f_bx)), row(half(b_ba)), row(half(b_bx)), row(f_lam), row(b_lam))


FF_CHUNK = 1024


def _post_kernel(o_ref, *refs, n_streams, nct):
    x_refs = refs[:n_streams]
    mod_ref, ng_ref, wo_ref, w1_ref, w2_ref, out_ref = refs[n_streams:]
    for i in range(BATCH_BLOCK):
        m = mod_ref[i, 0]
        o = jnp.concatenate([o_ref[i, p] for p in range(N_PAIRS)], axis=-1)
        x1 = _stream_tile(x_refs, i, nct) + m[2:3, :] * _dot(o, wo_ref[...])
        h = _norm_mod(x1, ng_ref[...], m[3:4, :], m[4:5, :]).astype(BF16)
        acc = jnp.zeros_like(x1)
        for c in range(0, D_FF, FF_CHUNK):
            a = jnp.maximum(_dot(h, w1_ref[:, c:c + FF_CHUNK]), 0.0)
            acc = acc + _dot((a * a).astype(BF16), w2_ref[c:c + FF_CHUNK, :])
        out_ref[i] = x1 + m[5:6, :] * acc


def _post(o, xs, modsel, norm_g, w_o, w1, w2, *, nct, skip_ctx):
    bsz, _, d = xs[0].shape
    n_tok = sum(a.shape[1] for a in xs)
    first = nct if skip_ctx else 0
    nt = n_tok // TOK_TILE - first
    if len(xs) == 1:
        x_specs = [pl.BlockSpec((BATCH_BLOCK, TOK_TILE, d), lambda b, t: (b, t + first, 0))]
    else:
        assert not skip_ctx
        x_specs = _stream_specs(xs, nct)
    return pl.pallas_call(
        functools.partial(_post_kernel, n_streams=len(xs), nct=nct),
        grid=(bsz // BATCH_BLOCK, nt),
        in_specs=[pl.BlockSpec((BATCH_BLOCK, N_PAIRS, TOK_TILE, LANES),
                               lambda b, t: (b, 0, t + first, 0))]
        + x_specs
        + [_mod_spec(nct, first),
           _resident((1, d)), _resident((d, d)), _resident((d, D_FF)), _resident((D_FF, d))],
        out_specs=pl.BlockSpec((BATCH_BLOCK, TOK_TILE, d), lambda b, t: (b, t, 0)),
        out_shape=jax.ShapeDtypeStruct((bsz, nt * TOK_TILE, d), F32),
        compiler_params=_params(2),
        name="proj_residual_mlp",
    )(o, *xs, modsel, norm_g.reshape(1, d), w_o, w1, w2)


def _rope_tables(n_ctx, n_lat):
    t = jnp.arange(n_lat)
    row = (t // GRID_W).astype(F32)
    col = (t % GRID_W).astype(F32)
    n_freq = HEAD_DIM // 4
    inv = ROPE_THETA ** (-jnp.arange(n_freq, dtype=F32) / n_freq)
    ang = jnp.concatenate([row[:, None] * inv, col[:, None] * inv], axis=-1)
    cos = jnp.repeat(jnp.cos(ang), 2, axis=-1)
    sin = jnp.repeat(jnp.sin(ang), 2, axis=-1) * jnp.tile(jnp.array([-1.0, 1.0], F32), HEAD_DIM // 2)
    cos = jnp.concatenate([jnp.ones((n_ctx, HEAD_DIM), F32), cos], axis=0)
    sin = jnp.concatenate([jnp.zeros((n_ctx, HEAD_DIM), F32), sin], axis=0)
    return jnp.tile(cos, (1, LANES // HEAD_DIM)), jnp.tile(sin, (1, LANES // HEAD_DIM))


def kernel(x, c, ctx, c_ctx, norm1_g, norm2_g, w_mod, b_mod, w_mlp1, w_mlp2, a_w_qkv, a_q_norm_g, a_k_norm_g, a_lambda_q1, a_lambda_k1, a_lambda_q2, a_lambda_k2, a_subln_g, a_w_o, b_w_qkv, b_q_norm_g, b_k_norm_g, b_rpb, b_w_o, c_w_in, c_conv_w, c_conv_b, c_fwd_w_a, c_fwd_b_a, c_fwd_w_x, c_fwd_b_x, c_fwd_lam, c_bwd_w_a, c_bwd_b_a, c_bwd_w_x, c_bwd_b_x, c_bwd_lam, c_w_o, d_w_qkv, d_q_norm_g, d_k_norm_g, d_w_o):
    bsz, n_lat, d = x.shape
    n_ctx = ctx.shape[1]
    depth = w_mod.shape[0]
    assert d == D_MODEL and depth == 4 and n_ctx % TOK_TILE == 0 and n_lat % TOK_TILE == 0
    assert bsz % BATCH_BLOCK == 0
    assert (n_lat // GRID_W) % NA_Q_ROWS == 0 and n_lat // GRID_W >= NA_K_ROWS
    nct = n_ctx // TOK_TILE

    rows = -(-(bsz + 1) // SUBLANES) * SUBLANES
    c_all = jnp.concatenate([c, c_ctx[None, :], jnp.zeros((rows - bsz - 1, d), F32)], axis=0)
    mods = _modulation(c_all, w_mod, b_mod).reshape(depth, rows, N_MOD, d)
    mod_ctx = jnp.broadcast_to(mods[:, bsz][:, None], (depth, bsz, N_MOD, d))
    modsel = jnp.stack([mod_ctx, mods[:, :bsz]], axis=2)

    rope_tabs = _rope_tables(n_ctx, n_lat)
    w1 = w_mlp1.astype(BF16)
    w2 = w_mlp2.astype(BF16)

    xs = (ctx, x)
    q, k, v = _pre_attn(xs, modsel[0], norm1_g[0], a_w_qkv[0].astype(BF16), a_q_norm_g[0],
                        a_k_norm_g[0], rope_tabs, k_width=d, dup_kv=False, nct=nct)
    lam_vecs = jnp.stack([a_lambda_q1[0], a_lambda_k1[0], a_lambda_q2[0], a_lambda_k2[0]])
    lam_init = 0.8 - 0.6 * math.exp(-0.3 * 0)
    o = _diff_attention(q, k, v, lam_vecs, a_subln_g[0], nct=nct, n_ctx=n_ctx, lam_init=lam_init)
    xs = (_post(o, xs, modsel[0], norm2_g[0], a_w_o[0].astype(BF16), w1[0], w2[0],
                nct=nct, skip_ctx=False),)

    q, k, v = _pre_attn(xs, modsel[1], norm1_g[1], b_w_qkv[0].astype(BF16), b_q_norm_g[0],
                        b_k_norm_g[0], None, k_width=d, dup_kv=False, nct=nct)
    bias_tab = _na_bias_table(b_rpb[0], n_lat // GRID_W)
    o = _na_attention(q, k, v, bias_tab, nct=nct, n_ctx=n_ctx)
    xs = (_post(o, xs, modsel[1], norm2_g[1], b_w_o[0].astype(BF16), w1[1], w2[1],
                nct=nct, skip_ctx=False),)

    gate, rec = _pre_lru(xs[0], modsel[2], norm1_g[2], c_w_in[0].astype(BF16), nct=nct)
    lru_p = (c_conv_w[0], c_conv_b[0], c_fwd_w_a[0], c_fwd_b_a[0], c_fwd_w_x[0], c_fwd_b_x[0],
             c_fwd_lam[0], c_bwd_w_a[0], c_bwd_b_a[0], c_bwd_w_x[0], c_bwd_b_x[0], c_bwd_lam[0])
    o = _lru_mixer(rec, gate, lru_p, n_ctx=n_ctx)
    xs = (_post(o, xs, modsel[2], norm2_g[2], c_w_o[0].astype(BF16), w1[2], w2[2],
                nct=nct, skip_ctx=False),)

    q, k, v = _pre_attn(xs, modsel[3], norm1_g[3], d_w_qkv[0].astype(BF16), d_q_norm_g[0],
                        d_k_norm_g[0], rope_tabs, k_width=GQA_KV_HEADS * HEAD_DIM, dup_kv=True,
                        nct=nct)
    o = _gqa_attention(q, k, v, nct=nct)
    return _post(o, xs, modsel[3], norm2_g[3], d_w_o[0].astype(BF16), w1[3], w2[3],
                 nct=nct, skip_ctx=True)
```

```python
import functools
import math

import jax
import jax.numpy as jnp
from jax import lax
from jax.experimental import pallas as pl
from jax.experimental.pallas import tpu as pltpu

F32 = jnp.float32
BF16 = jnp.bfloat16

D_MODEL = 1024
HEAD_DIM = 64
GRID_W = 64
N_MOD = 6
ATTN_SCALE = HEAD_DIM ** -0.5
LOG2E = math.log2(math.e)
ROPE_THETA = 10000.0
NORM_EPS = 1e-6
DA_HEADS = D_MODEL // (2 * HEAD_DIM)
NA_ROWS_MAX = 8
NA_COLS = 16
LRU_BLOCK = 256
LRU_CONV = 4
LRU_C = 8.0
GQA_KV_HEADS = 4
D_FF = 4 * D_MODEL

LANES = 128
SUBLANES = 8
MXU_DIM = 256
VMEM_LIMIT = 56 * 1024 * 1024

TOK_TILE = 256
BATCH_BLOCK = 4
N_PAIRS = D_MODEL // LANES
MASK_VALUE = -1e30
NA_Q_ROWS = TOK_TILE // GRID_W
NA_K_ROWS = 12
NA_BAND = NA_K_ROWS * GRID_W


def _params(n_axes):
    return pltpu.CompilerParams(
        dimension_semantics=("arbitrary",) * n_axes, vmem_limit_bytes=VMEM_LIMIT)


def _resident(shape):
    zeros = (0,) * len(shape)
    return pl.BlockSpec(shape, lambda *_: zeros, pipeline_mode=pl.Buffered(1))


def _dot(a, b):
    return jnp.dot(a, b, preferred_element_type=F32)


def _dot_nt(a, b):
    return lax.dot_general(a, b, (((1,), (1,)), ((), ())), preferred_element_type=F32)


def _norm_mod(x, gain, shift, scale):
    y = x * lax.rsqrt(jnp.mean(x * x, axis=-1, keepdims=True) + NORM_EPS) * gain
    return y * (1.0 + scale) + shift


def _group_mean_matrix():
    r = lax.broadcasted_iota(jnp.int32, (MXU_DIM, MXU_DIM), 0) // HEAD_DIM
    c = lax.broadcasted_iota(jnp.int32, (MXU_DIM, MXU_DIM), 1) // HEAD_DIM
    return jnp.where(r == c, 1.0 / HEAD_DIM, 0.0).astype(BF16)


def _head_rms(x, gain, mean_mat):
    sq = (x * x).astype(BF16)
    ms = jnp.concatenate(
        [_dot(sq[:, j:j + MXU_DIM], mean_mat) for j in range(0, x.shape[1], MXU_DIM)], axis=-1)
    return x * lax.rsqrt(ms + NORM_EPS) * gain


def _rope(x, cos, sin):
    width = x.shape[1]
    reps = width // LANES
    c = jnp.concatenate([cos] * reps, axis=-1)
    s = jnp.concatenate([sin] * reps, axis=-1)
    lane = lax.broadcasted_iota(jnp.int32, x.shape, 1)
    partner = jnp.where((lane & 1) == 0, pltpu.roll(x, width - 1, 1), pltpu.roll(x, 1, 1))
    return x * c + partner * s


def _mods_kernel(c_ref, w_ref, b_ref, o_ref):
    c = c_ref[...]
    a = (c * jax.nn.sigmoid(c)).astype(BF16)
    o_ref[0] = _dot(a, w_ref[0].astype(BF16)) + b_ref[0]


def _modulation(c_all, w_mod, b_mod):
    depth, d, n = w_mod.shape
    rows = c_all.shape[0]
    tn = n // 4
    return pl.pallas_call(
        _mods_kernel,
        grid=(depth, n // tn),
        in_specs=[pl.BlockSpec((rows, d), lambda l, j: (0, 0)),
                  pl.BlockSpec((1, d, tn), lambda l, j: (l, 0, j)),
                  pl.BlockSpec((1, 1, tn), lambda l, j: (l, 0, j))],
        out_specs=pl.BlockSpec((1, rows, tn), lambda l, j: (l, 0, j)),
        out_shape=jax.ShapeDtypeStruct((depth, rows, n), F32),
        compiler_params=_params(2),
        name="adaln_modulation",
    )(c_all, w_mod, b_mod.reshape(depth, 1, n))


def _stream_tile(refs, i, nct):
    if len(refs) == 1:
        return refs[0][i]
    return jnp.where(pl.program_id(1) < nct, refs[0][i], refs[1][i])


def _stream_specs(xs, nct):
    d = xs[0].shape[-1]
    if len(xs) == 1:
        return [pl.BlockSpec((BATCH_BLOCK, TOK_TILE, d), lambda b, t: (b, t, 0))]
    return [pl.BlockSpec((BATCH_BLOCK, TOK_TILE, d), lambda b, t: (b, jnp.minimum(t, nct - 1), 0)),
            pl.BlockSpec((BATCH_BLOCK, TOK_TILE, d), lambda b, t: (b, jnp.maximum(t - nct, 0), 0))]


def _pre_attn_kernel(*refs, **static):
    for i in range(BATCH_BLOCK):
        _pre_attn_sample(i, *refs, **static)


def _pre_attn_sample(i, *refs, n_streams, nct, k_width, rope, dup_kv):
    x_refs, refs = refs[:n_streams], refs[n_streams:]
    if rope:
        (mod_ref, ng_ref, w_ref, qg_ref, kg_ref, cos_ref, sin_ref, q_ref, k_ref, v_ref) = refs
    else:
        mod_ref, ng_ref, w_ref, qg_ref, kg_ref, q_ref, k_ref, v_ref = refs
    q_ref, k_ref, v_ref = q_ref.at[i], k_ref.at[i], v_ref.at[i]
    m = mod_ref[i, 0]
    h = _norm_mod(_stream_tile(x_refs, i, nct), ng_ref[...], m[0:1, :], m[1:2, :]).astype(BF16)
    mean_mat = _group_mean_matrix()
    d = D_MODEL
    q = _head_rms(_dot(h, w_ref[:, 0:d]), qg_ref[...], mean_mat)
    k = _head_rms(_dot(h, w_ref[:, d:d + k_width]), kg_ref[...], mean_mat)
    v = _dot(h, w_ref[:, d + k_width:d + 2 * k_width])
    if rope:
        q = _rope(q, cos_ref[...], sin_ref[...])
        k = _rope(k, cos_ref[...], sin_ref[...])
    q = q * (ATTN_SCALE * LOG2E)
    for p in range(N_PAIRS):
        q_ref[p] = q[:, p * LANES:(p + 1) * LANES].astype(BF16)
    if dup_kv:
        for g in range(k_width // HEAD_DIM):
            kh = k[:, g * HEAD_DIM:(g + 1) * HEAD_DIM]
            k_ref[g] = jnp.concatenate([kh, kh], axis=-1).astype(BF16)
    else:
        for p in range(k_width // LANES):
            k_ref[p] = k[:, p * LANES:(p + 1) * LANES].astype(BF16)
    fill = jnp.ones((v.shape[0], LANES), F32)
    if dup_kv:
        for g in range(k_width // HEAD_DIM):
            vh = v[:, g * HEAD_DIM:(g + 1) * HEAD_DIM]
            v_ref[g] = jnp.concatenate([vh, vh, fill], axis=-1).astype(BF16)
    else:
        for p in range(k_width // LANES):
            v_ref[p] = jnp.concatenate(
                [v[:, p * LANES:(p + 1) * LANES], fill], axis=-1).astype(BF16)


def _mod_spec(nct, first=0):
    return pl.BlockSpec((BATCH_BLOCK, 1, N_MOD, D_MODEL),
                        lambda b, t: (b, jnp.where(t + first >= nct, 1, 0), 0, 0))


def _pre_attn(xs, modsel, norm_g, w_qkv, q_gain, k_gain, rope_tabs, *, k_width, dup_kv, nct):
    bsz, _, d = xs[0].shape
    n_tok = sum(a.shape[1] for a in xs)
    nt = n_tok // TOK_TILE
    n_w = w_qkv.shape[1]
    k_slots = k_width // HEAD_DIM if dup_kv else k_width // LANES
    rope = rope_tabs is not None
    in_specs = _stream_specs(xs, nct) + [
        _mod_spec(nct),
        _resident((1, d)),
        _resident((d, n_w)),
        _resident((1, d)),
        _resident((1, k_width))]
    args = list(xs) + [modsel, norm_g.reshape(1, d), w_qkv,
                       jnp.tile(q_gain, d // HEAD_DIM).reshape(1, d),
                       jnp.tile(k_gain, k_width // HEAD_DIM).reshape(1, k_width)]
    if rope:
        in_specs += [pl.BlockSpec((TOK_TILE, LANES), lambda b, t: (t, 0))] * 2
        args += list(rope_tabs)
    qo = jax.ShapeDtypeStruct((bsz, N_PAIRS, n_tok, LANES), BF16)
    ko = jax.ShapeDtypeStruct((bsz, k_slots, n_tok, LANES), BF16)
    vo = jax.ShapeDtypeStruct((bsz, k_slots, n_tok, MXU_DIM), BF16)
    return pl.pallas_call(
        functools.partial(_pre_attn_kernel, n_streams=len(xs), nct=nct, k_width=k_width,
                          rope=rope, dup_kv=dup_kv),
        grid=(bsz // BATCH_BLOCK, nt),
        in_specs=in_specs,
        out_specs=[
            pl.BlockSpec((BATCH_BLOCK, N_PAIRS, TOK_TILE, LANES), lambda b, t: (b, 0, t, 0)),
            pl.BlockSpec((BATCH_BLOCK, k_slots, TOK_TILE, LANES), lambda b, t: (b, 0, t, 0)),
            pl.BlockSpec((BATCH_BLOCK, k_slots, TOK_TILE, MXU_DIM), lambda b, t: (b, 0, t, 0))],
        out_shape=[qo, ko, vo],
        compiler_params=_params(2),
        name="norm_mod_qkv",
    )(*args)


def _pre_lru_kernel(x_ref, mod_ref, ng_ref, w_ref, gate_ref, rec_ref):
    d = D_MODEL
    for i in range(BATCH_BLOCK):
        m = mod_ref[i, 0]
        h = _norm_mod(x_ref[i], ng_ref[...], m[0:1, :], m[1:2, :]).astype(BF16)
        gate_ref[i] = jax.nn.gelu(_dot(h, w_ref[:, 0:d])).astype(BF16)
        rec_ref[i] = _dot(h, w_ref[:, d:2 * d])


def _pre_lru(xs, modsel, norm_g, w_in, *, nct):
    bsz, n_tok, d = xs.shape
    nt = n_tok // TOK_TILE
    tok_spec = pl.BlockSpec((BATCH_BLOCK, TOK_TILE, d), lambda b, t: (b, t, 0))
    return pl.pallas_call(
        _pre_lru_kernel,
        grid=(bsz // BATCH_BLOCK, nt),
        in_specs=[tok_spec, _mod_spec(nct), _resident((1, d)), _resident((d, 2 * d))],
        out_specs=[tok_spec, tok_spec],
        out_shape=[jax.ShapeDtypeStruct((bsz, n_tok, d), BF16),
                   jax.ShapeDtypeStruct((bsz, n_tok, d), F32)],
        compiler_params=_params(2),
        name="norm_mod_lru_in",
    )(xs, modsel, norm_g.reshape(1, d), w_in)


def _split_pair(q2):
    low = lax.broadcasted_iota(jnp.int32, q2.shape, 1) < HEAD_DIM
    zero = jnp.zeros_like(q2)
    return jnp.where(low, q2, zero), jnp.where(low, zero, q2)


def _exp2_parts(parts):
    m = functools.reduce(jnp.maximum, [jnp.max(s, axis=-1, keepdims=True) for s in parts])
    return [jnp.exp2(s - m) for s in parts]


def _pipeline3(n_units, scores, softmax, values):
    assert n_units % 2 == 0 and n_units >= 4
    scores(0, 0)
    scores(1, 1)
    softmax(0, 0)
    for u in range(2, n_units, 2):
        scores(u, 0)
        softmax(u - 1, 1)
        values(u - 2, 0)
        scores(u + 1, 1)
        softmax(u, 0)
        values(u - 1, 1)
    softmax(n_units - 1, 1)
    values(n_units - 2, 0)
    values(n_units - 1, 1)


def _sub_layer_norm(o, gain, lam_init):
    o = o * lax.rsqrt(jnp.mean(o * o, axis=-1, keepdims=True) + NORM_EPS) * gain
    return (o * (1.0 - lam_init)).astype(BF16)


def _scores_stage(s_buf, m_buf, q2, key_parts, bias_parts):
    for half, qh in enumerate(_split_pair(q2)):
        m, col = None, 0
        for kp, bp in zip(key_parts, bias_parts):
            s = _dot_nt(qh, kp)
            if bp is not None:
                s = s + bp[half].astype(F32)
            s_buf[half, :, col:col + kp.shape[0]] = s
            part_max = jnp.max(s, axis=-1, keepdims=True)
            m = part_max if m is None else jnp.maximum(m, part_max)
            col += kp.shape[0]
        m_buf[half] = jnp.broadcast_to(m, m_buf.shape[1:])


def _softmax_stage(s_buf, m_buf, p_buf):
    for half in range(2):
        p_buf[half] = jnp.exp2(s_buf[half] - m_buf[half][:, 0:1]).astype(BF16)


def _values_stage(p_buf, value_parts):
    outs = []
    for half in range(2):
        acc, col = None, 0
        for vp in value_parts:
            part = _dot(p_buf[half, :, col:col + vp.shape[0]], vp)
            acc = part if acc is None else acc + part
            col += vp.shape[0]
        outs.append(acc[:, :LANES] / acc[:, LANES:])
    return outs


def _direct_pair(q2, k2, v2):
    outs = []
    for qh in _split_pair(q2):
        (e,) = _exp2_parts([_dot_nt(qh, k2)])
        acc = _dot(e.astype(BF16), v2)
        outs.append(acc[:, :LANES] / acc[:, LANES:])
    return outs


def _attn_scratch(n_cols):
    scratch = [pltpu.VMEM((2, TOK_TILE, n_cols), F32)] * 2
    scratch += [pltpu.VMEM((2, TOK_TILE, LANES), F32)] * 2
    scratch += [pltpu.VMEM((2, TOK_TILE, n_cols), BF16)] * 2
    return scratch


def _diff_attn_kernel(lam_ref, sg_ref, q_ref, k_ref, v_ref, o_ref, s0, s1, m0, m1, p0, p1,
                      *, nct, n_ctx, lam_init):
    t = pl.program_id(1)
    lv = lam_ref[...]
    lam = (jnp.exp(jnp.sum(lv[0:1, :] * lv[1:2, :], axis=-1, keepdims=True))
           - jnp.exp(jnp.sum(lv[2:3, :] * lv[3:4, :], axis=-1, keepdims=True)) + lam_init)

    def finish(h, outs):
        o_ref[0, h] = _sub_layer_norm(outs[0] - lam * outs[1], sg_ref[...], lam_init)

    @pl.when(t < nct)
    def _():
        for h in range(DA_HEADS):
            finish(h, _direct_pair(q_ref[0, h], k_ref[0, h, 0:n_ctx, :], v_ref[0, h, 0:n_ctx, :]))

    @pl.when(t >= nct)
    def _():
        s_bufs, m_bufs, p_bufs = (s0, s1), (m0, m1), (p0, p1)
        _pipeline3(
            DA_HEADS,
            lambda h, slot: _scores_stage(s_bufs[slot], m_bufs[slot], q_ref[0, h],
                                          [k_ref[0, h]], [None]),
            lambda h, slot: _softmax_stage(s_bufs[slot], m_bufs[slot], p_bufs[slot]),
            lambda h, slot: finish(h, _values_stage(p_bufs[slot], [v_ref[0, h]])))


def _diff_attention(q, k, v, lam_vecs, subln_g, *, nct, n_ctx, lam_init):
    bsz, _, n_tok, _ = q.shape
    nt = n_tok // TOK_TILE
    tile = pl.BlockSpec((1, N_PAIRS, TOK_TILE, LANES), lambda b, t: (b, 0, t, 0))
    whole_k = pl.BlockSpec((1, DA_HEADS, n_tok, LANES), lambda b, t: (b, 0, 0, 0))
    whole_v = pl.BlockSpec((1, DA_HEADS, n_tok, MXU_DIM), lambda b, t: (b, 0, 0, 0))
    scratch = _attn_scratch(n_tok)
    return pl.pallas_call(
        functools.partial(_diff_attn_kernel, nct=nct, n_ctx=n_ctx, lam_init=lam_init),
        grid=(bsz, nt),
        in_specs=[_resident((4, HEAD_DIM)), _resident((1, LANES)), tile, whole_k, whole_v],
        out_specs=tile,
        out_shape=jax.ShapeDtypeStruct(q.shape, BF16),
        scratch_shapes=scratch,
        compiler_params=_params(2),
        name="diff_attention",
    )(lam_vecs, subln_g.reshape(1, LANES), q, k, v)


def _pair_output(outs):
    low = lax.broadcasted_iota(jnp.int32, outs[0].shape, 1) < HEAD_DIM
    return jnp.where(low, outs[0], outs[1]).astype(BF16)


def _gqa_attn_kernel(q_ref, k_ref, v_ref, o_ref, s0, s1, m0, m1, p0, p1):
    s_bufs, m_bufs, p_bufs = (s0, s1), (m0, m1), (p0, p1)

    def values(p, slot):
        o_ref[0, p] = _pair_output(_values_stage(p_bufs[slot], [v_ref[0, p // 2]]))

    _pipeline3(
        N_PAIRS,
        lambda p, slot: _scores_stage(s_bufs[slot], m_bufs[slot], q_ref[0, p],
                                      [k_ref[0, p // 2]], [None]),
        lambda p, slot: _softmax_stage(s_bufs[slot], m_bufs[slot], p_bufs[slot]),
        values)


def _gqa_attention(q, k, v, *, nct):
    bsz, _, n_tok, _ = q.shape
    nt = n_tok // TOK_TILE - nct
    tile = pl.BlockSpec((1, N_PAIRS, TOK_TILE, LANES), lambda b, t: (b, 0, t + nct, 0))
    whole_k = pl.BlockSpec((1, GQA_KV_HEADS, n_tok, LANES), lambda b, t: (b, 0, 0, 0))
    whole_v = pl.BlockSpec((1, GQA_KV_HEADS, n_tok, MXU_DIM), lambda b, t: (b, 0, 0, 0))
    scratch = _attn_scratch(n_tok)
    return pl.pallas_call(
        _gqa_attn_kernel,
        grid=(bsz, nt),
        in_specs=[tile, whole_k, whole_v],
        out_specs=tile,
        out_shape=jax.ShapeDtypeStruct(q.shape, BF16),
        scratch_shapes=scratch,
        compiler_params=_params(2),
        name="gqa_attention",
    )(q, k, v)


def _na_attn_kernel(bias_ref, q_ref, k_ref, v_ref, o_ref, s0, s1, m0, m1, p0, p1,
                    *, nct, n_ctx, n_rows):
    t = pl.program_id(1)

    @pl.when(t < nct)
    def _():
        for p in range(N_PAIRS):
            o_ref[0, p] = _pair_output(_direct_pair(
                q_ref[0, p], k_ref[0, p, 0:n_ctx, :], v_ref[0, p, 0:n_ctx, :]))

    @pl.when(t >= nct)
    def _():
        first_row = jnp.clip(NA_Q_ROWS * (t - nct) - NA_ROWS_MAX // 2, 0, n_rows - NA_K_ROWS)
        start = pl.multiple_of(n_ctx + first_row * GRID_W, GRID_W)
        s_bufs, m_bufs, p_bufs = (s0, s1), (m0, m1), (p0, p1)

        def scores(p, slot):
            keys = [k_ref[0, p, pl.ds(0, n_ctx), :], k_ref[0, p, pl.ds(start, NA_BAND), :]]
            bias = (bias_ref[0, 2 * p], bias_ref[0, 2 * p + 1])
            _scores_stage(s_bufs[slot], m_bufs[slot], q_ref[0, p], keys, [None, bias])

        def values(p, slot):
            vals = [v_ref[0, p, pl.ds(0, n_ctx), :], v_ref[0, p, pl.ds(start, NA_BAND), :]]
            o_ref[0, p] = _pair_output(_values_stage(p_bufs[slot], vals))

        _pipeline3(
            N_PAIRS, scores,
            lambda p, slot: _softmax_stage(s_bufs[slot], m_bufs[slot], p_bufs[slot]),
            values)


def _na_bias_table(rpb, n_rows):
    heads = rpb.shape[0]
    kr_rows = min(NA_ROWS_MAX, n_rows)
    variants = (0, 1, n_rows // NA_Q_ROWS - 1)
    n_tiles = len(variants)
    tt = jnp.array(variants)[:, None, None]
    qr = jnp.arange(NA_Q_ROWS)[None, :, None]
    kr = jnp.arange(NA_K_ROWS)[None, None, :]
    r = NA_Q_ROWS * tt + qr
    rp = jnp.clip(NA_Q_ROWS * tt - NA_ROWS_MAX // 2, 0, n_rows - NA_K_ROWS) + kr
    r0 = jnp.clip(r - kr_rows // 2, 0, n_rows - kr_rows)
    row_ok = (rp >= r0) & (rp < r0 + kr_rows)
    row_idx = jnp.clip(rp - r + NA_ROWS_MAX - 1, 0, 2 * NA_ROWS_MAX - 2)
    c = jnp.arange(GRID_W)[:, None]
    cp = jnp.arange(GRID_W)[None, :]
    cs = jnp.clip(c - NA_COLS // 2, 0, GRID_W - NA_COLS)
    col_ok = (cp >= cs) & (cp < cs + NA_COLS)
    col_idx = jnp.clip(cp - c, -(NA_COLS - 1), NA_COLS - 1) + NA_COLS - 1
    rows_sel = rpb[:, row_idx, :]
    onehot = (col_idx.reshape(-1)[None, :] == jnp.arange(2 * NA_COLS - 1)[:, None]).astype(F32)
    tab = jnp.dot(rows_sel.reshape(-1, 2 * NA_COLS - 1), onehot, precision=lax.Precision.HIGHEST)
    tab = tab.reshape(heads, n_tiles, NA_Q_ROWS, NA_K_ROWS, GRID_W, GRID_W)
    ok = row_ok[None, :, :, :, None, None] & col_ok[None, None, None, None, :, :]
    tab = jnp.where(ok, tab * LOG2E, MASK_VALUE)
    tab = tab.transpose(1, 0, 2, 4, 3, 5)
    return tab.reshape(n_tiles, heads, TOK_TILE, NA_BAND).astype(BF16)


def _na_attention(q, k, v, bias_tab, *, nct, n_ctx):
    bsz, _, n_tok, _ = q.shape
    nt = n_tok // TOK_TILE
    heads = bias_tab.shape[1]
    n_rows = (n_tok - n_ctx) // GRID_W
    tile = pl.BlockSpec((1, N_PAIRS, TOK_TILE, LANES), lambda b, t: (b, 0, t, 0))
    whole_k = pl.BlockSpec((1, N_PAIRS, n_tok, LANES), lambda b, t: (b, 0, 0, 0))
    whole_v = pl.BlockSpec((1, N_PAIRS, n_tok, MXU_DIM), lambda b, t: (b, 0, 0, 0))
    last = nt - nct - 1

    def bias_index(b, t):
        tile = t - nct
        return (jnp.where(tile <= 0, 0, jnp.where(tile == last, 2, 1)), 0, 0, 0)

    bias_spec = pl.BlockSpec((1, heads, TOK_TILE, NA_BAND), bias_index)
    scratch = _attn_scratch(n_ctx + NA_BAND)
    return pl.pallas_call(
        functools.partial(_na_attn_kernel, nct=nct, n_ctx=n_ctx, n_rows=n_rows),
        grid=(bsz, nt),
        in_specs=[bias_spec, tile, whole_k, whole_v],
        out_specs=tile,
        out_shape=jax.ShapeDtypeStruct(q.shape, BF16),
        scratch_shapes=scratch,
        compiler_params=_params(2),
        name="neighbourhood_attention",
    )(bias_tab, q, k, v)


LRU_CHUNK = 256
LRU_PAD = 8


def _softplus(z):
    return jnp.maximum(z, 0.0) + jnp.log1p(jnp.exp(-jnp.abs(z)))


def _sqrt_nonneg(z):
    return z * lax.rsqrt(jnp.maximum(z, float(jnp.finfo(F32).tiny)))


def _block_scan(a_ref, b_ref, slab, r0, reverse):
    n_blk = LRU_CHUNK // SUBLANES
    order = range(SUBLANES - 1, -1, -1) if reverse else range(SUBLANES)
    acc_a = acc_b = None
    for r in order:
        rows = pl.ds(r0 + r, n_blk, stride=SUBLANES)
        a_r = a_ref[slab, rows, :]
        b_r = b_ref[slab, rows, :]
        if acc_a is None:
            acc_a, acc_b = a_r, b_r
            continue
        acc_b = a_r * acc_b + b_r
        acc_a = a_r * acc_a
        a_ref[slab, rows, :] = acc_a
        b_ref[slab, rows, :] = acc_b


def _lru_kernel(rec_ref, gate_ref, cw_ref, cb_ref,
                fwa_ref, fwx_ref, bwa_ref, bwx_ref,
                fba_ref, fbx_ref, bba_ref, bbx_ref, flam_ref, blam_ref,
                o_ref, xp_ref, af_ref, bf_ref, ab_ref, bb_ref, *, n_ctx, n_tok):
    width = LRU_BLOCK
    xp_ref[0:LRU_PAD, :] = jnp.zeros((LRU_PAD, width), F32)
    xp_ref[LRU_PAD + n_tok:2 * LRU_PAD + n_tok, :] = jnp.zeros((LRU_PAD, width), F32)
    xp_ref[LRU_PAD:LRU_PAD + n_tok, :] = rec_ref[0]

    directions = (
        (fwa_ref, fwx_ref, fba_ref, fbx_ref, (-0.5 * LRU_C) * _softplus(-flam_ref[...]),
         af_ref, bf_ref, False),
        (bwa_ref, bwx_ref, bba_ref, bbx_ref, (-0.5 * LRU_C) * _softplus(-blam_ref[...]),
         ab_ref, bb_ref, True),
    )
    n_slabs = width // LANES
    wrow = lax.broadcasted_iota(jnp.int32, (LRU_CHUNK + 2 * LRU_PAD, 1), 0)

    for r0 in range(0, n_tok, LRU_CHUNK):
        xw = xp_ref[r0:r0 + LRU_CHUNK + 2 * LRU_PAD, :]
        if r0 == n_ctx:
            xw = jnp.where(wrow < LRU_PAD, 0.0, xw)
        if r0 + LRU_CHUNK == n_ctx:
            xw = jnp.where(wrow >= LRU_PAD + LRU_CHUNK, 0.0, xw)
        u = cb_ref[...]
        for j in range(LRU_CONV):
            off = LRU_PAD + j - LRU_CONV // 2
            u = u + xw[off:off + LRU_CHUNK, :] * cw_ref[j:j + 1, :]
        ub = u.astype(BF16)
        for wa_ref, wx_ref, ba_ref, bx_ref, decay, a_ref, b_ref, reverse in directions:
            log_a = decay * jnp.tanh(_dot(ub, wa_ref[0]) + ba_ref[...]) + decay
            i = 0.5 * jnp.tanh(_dot(ub, wx_ref[0]) + bx_ref[...]) + 0.5
            a = jnp.exp(log_a)
            b = _sqrt_nonneg(-jnp.tanh(log_a) * (a * a + 1.0)) * (i * u)
            for s in range(n_slabs):
                a_ref[s, r0:r0 + LRU_CHUNK, :] = a[:, s * LANES:(s + 1) * LANES]
                b_ref[s, r0:r0 + LRU_CHUNK, :] = b[:, s * LANES:(s + 1) * LANES]
                _block_scan(a_ref, b_ref, s, r0, reverse)

    n_all = n_tok // SUBLANES
    n_cb = n_ctx // SUBLANES

    def step(i, carry, rev_end):
        cf, cr = carry
        rf = pl.multiple_of(i * SUBLANES, SUBLANES)
        rb = pl.multiple_of((rev_end - i) * SUBLANES, SUBLANES)
        a_f, b_f = af_ref[:, pl.ds(rf, SUBLANES), :], bf_ref[:, pl.ds(rf, SUBLANES), :]
        a_r, b_r = ab_ref[:, pl.ds(rb, SUBLANES), :], bb_ref[:, pl.ds(rb, SUBLANES), :]
        bf_ref[:, pl.ds(rf, SUBLANES), :] = a_f * cf + b_f
        bb_ref[:, pl.ds(rb, SUBLANES), :] = a_r * cr + b_r

        def edge(x, row):
            return jnp.broadcast_to(x[:, row:row + 1, :], x.shape)
        last = SUBLANES - 1
        return (edge(a_f, last) * cf + edge(b_f, last), edge(a_r, 0) * cr + edge(b_r, 0))

    zero = jnp.zeros((n_slabs, SUBLANES, LANES), F32)
    carry = lax.fori_loop(0, n_cb, functools.partial(step, rev_end=n_cb - 1), (zero, zero),
                          unroll=2)
    lax.fori_loop(n_cb, n_all, functools.partial(step, rev_end=n_all - 1 + n_cb), carry, unroll=2)

    gate = gate_ref[0]
    for s in range(n_slabs):
        y = (bf_ref[s] + bb_ref[s]) * gate[:, s * LANES:(s + 1) * LANES].astype(F32)
        o_ref[0, s] = y.astype(BF16)


def _lru_mixer(rec, gate, p, *, n_ctx):
    (conv_w, conv_b, f_wa, f_ba, f_wx, f_bx, f_lam, b_wa, b_ba, b_wx, b_bx, b_lam) = p
    bsz, n_tok, d = rec.shape
    n_blocks = d // LRU_BLOCK
    seq = pl.BlockSpec((1, n_tok, LRU_BLOCK), lambda b, n: (b, 0, n))
    vec = pl.BlockSpec((1, LRU_BLOCK), lambda b, n: (0, n))
    mat = pl.BlockSpec((1, LRU_BLOCK, LRU_BLOCK), lambda b, n: (n, 0, 0))
    row = lambda a: a.reshape(1, d)
    half = lambda a: 0.5 * a
    assert n_ctx % LRU_CHUNK == 0 and n_tok % LRU_CHUNK == 0
    scratch = [pltpu.VMEM((n_tok + 2 * LRU_PAD, LRU_BLOCK), F32)]
    scratch += [pltpu.VMEM((LRU_BLOCK // LANES, n_tok, LANES), F32)] * 4
    return pl.pallas_call(
        functools.partial(_lru_kernel, n_ctx=n_ctx, n_tok=n_tok),
        grid=(bsz, n_blocks),
        in_specs=[seq, seq, pl.BlockSpec((LRU_CONV, LRU_BLOCK), lambda b, n: (0, n)), vec,
                  mat, mat, mat, mat, vec, vec, vec, vec, vec, vec],
        out_specs=pl.BlockSpec((1, LRU_BLOCK // LANES, n_tok, LANES), lambda b, n: (b, n, 0, 0)),
        out_shape=jax.ShapeDtypeStruct((bsz, d // LANES, n_tok, LANES), BF16),
        scratch_shapes=scratch,
        compiler_params=_params(2),
        name="rglru_scan",
    )(rec, gate, conv_w, row(conv_b),
      half(f_wa).astype(BF16), half(f_wx).astype(BF16),
      half(b_wa).astype(BF16), half(b_wx).astype(BF16),
      row(half(f_ba)), row(half(f_bx)), row(half(b_ba)), row(half(b_bx)), row(f_lam), row(b_lam))


FF_CHUNK = 1024


def _post_kernel(o_ref, *refs, n_streams, nct):
    x_refs = refs[:n_streams]
    mod_ref, ng_ref, wo_ref, w1_ref, w2_ref, out_ref = refs[n_streams:]
    for i in range(BATCH_BLOCK):
        m = mod_ref[i, 0]
        o = jnp.concatenate([o_ref[i, p] for p in range(N_PAIRS)], axis=-1)
        x1 = _stream_tile(x_refs, i, nct) + m[2:3, :] * _dot(o, wo_ref[...])
        h = _norm_mod(x1, ng_ref[...], m[3:4, :], m[4:5, :]).astype(BF16)
        acc = jnp.zeros_like(x1)
        for c in range(0, D_FF, FF_CHUNK):
            a = jnp.maximum(_dot(h, w1_ref[:, c:c + FF_CHUNK]), 0.0)
            acc = acc + _dot((a * a).astype(BF16), w2_ref[c:c + FF_CHUNK, :])
        out_ref[i] = x1 + m[5:6, :] * acc


def _post(o, xs, modsel, norm_g, w_o, w1, w2, *, nct, skip_ctx):
    bsz, _, d = xs[0].shape
    n_tok = sum(a.shape[1] for a in xs)
    first = nct if skip_ctx else 0
    nt = n_tok // TOK_TILE - first
    if len(xs) == 1:
        x_specs = [pl.BlockSpec((BATCH_BLOCK, TOK_TILE, d), lambda b, t: (b, t + first, 0))]
    else:
        assert not skip_ctx
        x_specs = _stream_specs(xs, nct)
    return pl.pallas_call(
        functools.partial(_post_kernel, n_streams=len(xs), nct=nct),
        grid=(bsz // BATCH_BLOCK, nt),
        in_specs=[pl.BlockSpec((BATCH_BLOCK, N_PAIRS, TOK_TILE, LANES),
                               lambda b, t: (b, 0, t + first, 0))]
        + x_specs
        + [_mod_spec(nct, first),
           _resident((1, d)), _resident((d, d)), _resident((d, D_FF)), _resident((D_FF, d))],
        out_specs=pl.BlockSpec((BATCH_BLOCK, TOK_TILE, d), lambda b, t: (b, t, 0)),
        out_shape=jax.ShapeDtypeStruct((bsz, nt * TOK_TILE, d), F32),
        compiler_params=_params(2),
        name="proj_residual_mlp",
    )(o, *xs, modsel, norm_g.reshape(1, d), w_o, w1, w2)


def _rope_tables(n_ctx, n_lat):
    t = jnp.arange(n_lat)
    row = (t // GRID_W).astype(F32)
    col = (t % GRID_W).astype(F32)
    n_freq = HEAD_DIM // 4
    inv = ROPE_THETA ** (-jnp.arange(n_freq, dtype=F32) / n_freq)
    ang = jnp.concatenate([row[:, None] * inv, col[:, None] * inv], axis=-1)
    cos = jnp.repeat(jnp.cos(ang), 2, axis=-1)
    sin = jnp.repeat(jnp.sin(ang), 2, axis=-1) * jnp.tile(jnp.array([-1.0, 1.0], F32), HEAD_DIM // 2)
    cos = jnp.concatenate([jnp.ones((n_ctx, HEAD_DIM), F32), cos], axis=0)
    sin = jnp.concatenate([jnp.zeros((n_ctx, HEAD_DIM), F32), sin], axis=0)
    return jnp.tile(cos, (1, LANES // HEAD_DIM)), jnp.tile(sin, (1, LANES // HEAD_DIM))


def kernel(x, c, ctx, c_ctx, norm1_g, norm2_g, w_mod, b_mod, w_mlp1, w_mlp2, a_w_qkv, a_q_norm_g, a_k_norm_g, a_lambda_q1, a_lambda_k1, a_lambda_q2, a_lambda_k2, a_subln_g, a_w_o, b_w_qkv, b_q_norm_g, b_k_norm_g, b_rpb, b_w_o, c_w_in, c_conv_w, c_conv_b, c_fwd_w_a, c_fwd_b_a, c_fwd_w_x, c_fwd_b_x, c_fwd_lam, c_bwd_w_a, c_bwd_b_a, c_bwd_w_x, c_bwd_b_x, c_bwd_lam, c_w_o, d_w_qkv, d_q_norm_g, d_k_norm_g, d_w_o):
    bsz, n_lat, d = x.shape
    n_ctx = ctx.shape[1]
    depth = w_mod.shape[0]
    assert d == D_MODEL and depth == 4 and n_ctx % TOK_TILE == 0 and n_lat % TOK_TILE == 0
    assert bsz % BATCH_BLOCK == 0
    assert (n_lat // GRID_W) % NA_Q_ROWS == 0 and n_lat // GRID_W >= NA_K_ROWS
    nct = n_ctx // TOK_TILE

    rows = -(-(bsz + 1) // SUBLANES) * SUBLANES
    c_all = jnp.concatenate([c, c_ctx[None, :], jnp.zeros((rows - bsz - 1, d), F32)], axis=0)
    mods = _modulation(c_all, w_mod, b_mod).reshape(depth, rows, N_MOD, d)
    mod_ctx = jnp.broadcast_to(mods[:, bsz][:, None], (depth, bsz, N_MOD, d))
    modsel = jnp.stack([mod_ctx, mods[:, :bsz]], axis=2)

    rope_tabs = _rope_tables(n_ctx, n_lat)
    w1 = w_mlp1.astype(BF16)
    w2 = w_mlp2.astype(BF16)

    xs = (ctx, x)
    q, k, v = _pre_attn(xs, modsel[0], norm1_g[0], a_w_qkv[0].astype(BF16), a_q_norm_g[0],
                        a_k_norm_g[0], rope_tabs, k_width=d, dup_kv=False, nct=nct)
    lam_vecs = jnp.stack([a_lambda_q1[0], a_lambda_k1[0], a_lambda_q2[0], a_lambda_k2[0]])
    lam_init = 0.8 - 0.6 * math.exp(-0.3 * 0)
    o = _diff_attention(q, k, v, lam_vecs, a_subln_g[0], nct=nct, n_ctx=n_ctx, lam_init=lam_init)
    xs = (_post(o, xs, modsel[0], norm2_g[0], a_w_o[0].astype(BF16), w1[0], w2[0],
                nct=nct, skip_ctx=False),)

    q, k, v = _pre_attn(xs, modsel[1], norm1_g[1], b_w_qkv[0].astype(BF16), b_q_norm_g[0],
                        b_k_norm_g[0], None, k_width=d, dup_kv=False, nct=nct)
    bias_tab = _na_bias_table(b_rpb[0], n_lat // GRID_W)
    o = _na_attention(q, k, v, bias_tab, nct=nct, n_ctx=n_ctx)
    xs = (_post(o, xs, modsel[1], norm2_g[1], b_w_o[0].astype(BF16), w1[1], w2[1],
                nct=nct, skip_ctx=False),)

    gate, rec = _pre_lru(xs[0], modsel[2], norm1_g[2], c_w_in[0].astype(BF16), nct=nct)
    lru_p = (c_conv_w[0], c_conv_b[0], c_fwd_w_a[0], c_fwd_b_a[0], c_fwd_w_x[0], c_fwd_b_x[0],
             c_fwd_lam[0], c_bwd_w_a[0], c_bwd_b_a[0], c_bwd_w_x[0], c_bwd_b_x[0], c_bwd_lam[0])
    o = _lru_mixer(rec, gate, lru_p, n_ctx=n_ctx)
    xs = (_post(o, xs, modsel[2], norm2_g[2], c_w_o[0].astype(BF16), w1[2], w2[2],
                nct=nct, skip_ctx=False),)

    q, k, v = _pre_attn(xs, modsel[3], norm1_g[3], d_w_qkv[0].astype(BF16), d_q_norm_g[0],
                        d_k_norm_g[0], rope_tabs, k_width=GQA_KV_HEADS * HEAD_DIM, dup_kv=True,
                        nct=nct)
    o = _gqa_attention(q, k, v, nct=nct)
    return _post(o, xs, modsel[3], norm2_g[3], d_w_o[0].astype(BF16), w1[3], w2[3],
                 nct=nct, skip_ctx=True)
```

```python
import functools
import math

import jax
import jax.numpy as jnp
from jax import lax
from jax.experimental import pallas as pl
from jax.experimental.pallas import tpu as pltpu

F32 = jnp.float32
BF16 = jnp.bfloat16

D_MODEL = 1024
HEAD_DIM = 64
GRID_W = 64
N_MOD = 6
ATTN_SCALE = HEAD_DIM ** -0.5
LOG2E = math.log2(math.e)
ROPE_THETA = 10000.0
NORM_EPS = 1e-6
DA_HEADS = D_MODEL // (2 * HEAD_DIM)
NA_ROWS_MAX = 8
NA_COLS = 16
LRU_BLOCK = 256
LRU_CONV = 4
LRU_C = 8.0
GQA_KV_HEADS = 4
D_FF = 4 * D_MODEL

LANES = 128
SUBLANES = 8
MXU_DIM = 256
VMEM_LIMIT = 56 * 1024 * 1024

TOK_TILE = 256
BATCH_BLOCK = 4
N_PAIRS = D_MODEL // LANES
MASK_VALUE = -1e30
NA_Q_ROWS = TOK_TILE // GRID_W
NA_K_ROWS = 12
NA_BAND = NA_K_ROWS * GRID_W


def _params(n_axes):
    return pltpu.CompilerParams(
        dimension_semantics=("arbitrary",) * n_axes, vmem_limit_bytes=VMEM_LIMIT)


def _resident(shape):
    zeros = (0,) * len(shape)
    return pl.BlockSpec(shape, lambda *_: zeros, pipeline_mode=pl.Buffered(1))


def _dot(a, b):
    return jnp.dot(a, b, preferred_element_type=F32)


def _dot_nt(a, b):
    return lax.dot_general(a, b, (((1,), (1,)), ((), ())), preferred_element_type=F32)


def _norm_mod(x, gain, shift, scale):
    y = x * lax.rsqrt(jnp.mean(x * x, axis=-1, keepdims=True) + NORM_EPS) * gain
    return y * (1.0 + scale) + shift


def _group_mean_matrix():
    r = lax.broadcasted_iota(jnp.int32, (MXU_DIM, MXU_DIM), 0) // HEAD_DIM
    c = lax.broadcasted_iota(jnp.int32, (MXU_DIM, MXU_DIM), 1) // HEAD_DIM
    return jnp.where(r == c, 1.0 / HEAD_DIM, 0.0).astype(BF16)


def _head_rms(x, gain, mean_mat):
    sq = (x * x).astype(BF16)
    ms = jnp.concatenate(
        [_dot(sq[:, j:j + MXU_DIM], mean_mat) for j in range(0, x.shape[1], MXU_DIM)], axis=-1)
    return x * lax.rsqrt(ms + NORM_EPS) * gain


def _rope(x, cos, sin):
    width = x.shape[1]
    reps = width // LANES
    c = jnp.concatenate([cos] * reps, axis=-1)
    s = jnp.concatenate([sin] * reps, axis=-1)
    lane = lax.broadcasted_iota(jnp.int32, x.shape, 1)
    partner = jnp.where((lane & 1) == 0, pltpu.roll(x, width - 1, 1), pltpu.roll(x, 1, 1))
    return x * c + partner * s


def _mods_kernel(c_ref, w_ref, b_ref, o_ref):
    c = c_ref[...]
    a = (c * jax.nn.sigmoid(c)).astype(BF16)
    o_ref[0] = _dot(a, w_ref[0].astype(BF16)) + b_ref[0]


def _modulation(c_all, w_mod, b_mod):
    depth, d, n = w_mod.shape
    rows = c_all.shape[0]
    tn = n // 4
    return pl.pallas_call(
        _mods_kernel,
        grid=(depth, n // tn),
        in_specs=[pl.BlockSpec((rows, d), lambda l, j: (0, 0)),
                  pl.BlockSpec((1, d, tn), lambda l, j: (l, 0, j)),
                  pl.BlockSpec((1, 1, tn), lambda l, j: (l, 0, j))],
        out_specs=pl.BlockSpec((1, rows, tn), lambda l, j: (l, 0, j)),
        out_shape=jax.ShapeDtypeStruct((depth, rows, n), F32),
        compiler_params=_params(2),
        name="adaln_modulation",
    )(c_all, w_mod, b_mod.reshape(depth, 1, n))


def _stream_tile(refs, i, nct):
    if len(refs) == 1:
        return refs[0][i]
    return jnp.where(pl.program_id(1) < nct, refs[0][i], refs[1][i])


def _stream_specs(xs, nct):
    d = xs[0].shape[-1]
    if len(xs) == 1:
        return [pl.BlockSpec((BATCH_BLOCK, TOK_TILE, d), lambda b, t: (b, t, 0))]
    return [pl.BlockSpec((BATCH_BLOCK, TOK_TILE, d), lambda b, t: (b, jnp.minimum(t, nct - 1), 0)),
            pl.BlockSpec((BATCH_BLOCK, TOK_TILE, d), lambda b, t: (b, jnp.maximum(t - nct, 0), 0))]


def _pre_attn_kernel(*refs, **static):
    for i in range(BATCH_BLOCK):
        _pre_attn_sample(i, *refs, **static)


def _pre_attn_sample(i, *refs, n_streams, nct, k_width, rope, dup_kv):
    x_refs, refs = refs[:n_streams], refs[n_streams:]
    if rope:
        (mod_ref, ng_ref, w_ref, qg_ref, kg_ref, cos_ref, sin_ref, q_ref, k_ref, v_ref) = refs
    else:
        mod_ref, ng_ref, w_ref, qg_ref, kg_ref, q_ref, k_ref, v_ref = refs
    q_ref, k_ref, v_ref = q_ref.at[i], k_ref.at[i], v_ref.at[i]
    m = mod_ref[i, 0]
    h = _norm_mod(_stream_tile(x_refs, i, nct), ng_ref[...], m[0:1, :], m[1:2, :]).astype(BF16)
    mean_mat = _group_mean_matrix()
    d = D_MODEL
    q = _head_rms(_dot(h, w_ref[:, 0:d]), qg_ref[...], mean_mat)
    k = _head_rms(_dot(h, w_ref[:, d:d + k_width]), kg_ref[...], mean_mat)
    v = _dot(h, w_ref[:, d + k_width:d + 2 * k_width])
    if rope:
        q = _rope(q, cos_ref[...], sin_ref[...])
        k = _rope(k, cos_ref[...], sin_ref[...])
    q = q * (ATTN_SCALE * LOG2E)
    for p in range(N_PAIRS):
        q_ref[p] = q[:, p * LANES:(p + 1) * LANES].astype(BF16)
    if dup_kv:
        for g in range(k_width // HEAD_DIM):
            kh = k[:, g * HEAD_DIM:(g + 1) * HEAD_DIM]
            k_ref[g] = jnp.concatenate([kh, kh], axis=-1).astype(BF16)
    else:
        for p in range(k_width // LANES):
            k_ref[p] = k[:, p * LANES:(p + 1) * LANES].astype(BF16)
    if dup_kv:
        for g in range(k_width // HEAD_DIM):
            vh = v[:, g * HEAD_DIM:(g + 1) * HEAD_DIM]
            v_ref[g] = jnp.concatenate([vh, vh], axis=-1).astype(BF16)
    else:
        for p in range(k_width // LANES):
            v_ref[p] = v[:, p * LANES:(p + 1) * LANES].astype(BF16)


def _mod_spec(nct, first=0):
    return pl.BlockSpec((BATCH_BLOCK, 1, N_MOD, D_MODEL),
                        lambda b, t: (b, jnp.where(t + first >= nct, 1, 0), 0, 0))


def _pre_attn(xs, modsel, norm_g, w_qkv, q_gain, k_gain, rope_tabs, *, k_width, dup_kv, nct):
    bsz, _, d = xs[0].shape
    n_tok = sum(a.shape[1] for a in xs)
    nt = n_tok // TOK_TILE
    n_w = w_qkv.shape[1]
    k_slots = k_width // HEAD_DIM if dup_kv else k_width // LANES
    rope = rope_tabs is not None
    in_specs = _stream_specs(xs, nct) + [
        _mod_spec(nct),
        _resident((1, d)),
        _resident((d, n_w)),
        _resident((1, d)),
        _resident((1, k_width))]
    args = list(xs) + [modsel, norm_g.reshape(1, d), w_qkv,
                       jnp.tile(q_gain, d // HEAD_DIM).reshape(1, d),
                       jnp.tile(k_gain, k_width // HEAD_DIM).reshape(1, k_width)]
    if rope:
        in_specs += [pl.BlockSpec((TOK_TILE, LANES), lambda b, t: (t, 0))] * 2
        args += list(rope_tabs)
    qo = jax.ShapeDtypeStruct((bsz, N_PAIRS, n_tok, LANES), BF16)
    ko = jax.ShapeDtypeStruct((bsz, k_slots, n_tok, LANES), BF16)
    vo = jax.ShapeDtypeStruct((bsz, k_slots, n_tok, LANES), BF16)
    return pl.pallas_call(
        functools.partial(_pre_attn_kernel, n_streams=len(xs), nct=nct, k_width=k_width,
                          rope=rope, dup_kv=dup_kv),
        grid=(bsz // BATCH_BLOCK, nt),
        in_specs=in_specs,
        out_specs=[
            pl.BlockSpec((BATCH_BLOCK, N_PAIRS, TOK_TILE, LANES), lambda b, t: (b, 0, t, 0)),
            pl.BlockSpec((BATCH_BLOCK, k_slots, TOK_TILE, LANES), lambda b, t: (b, 0, t, 0)),
            pl.BlockSpec((BATCH_BLOCK, k_slots, TOK_TILE, LANES), lambda b, t: (b, 0, t, 0))],
        out_shape=[qo, ko, vo],
        compiler_params=_params(2),
        name="norm_mod_qkv",
    )(*args)


def _pre_lru_kernel(x_ref, mod_ref, ng_ref, w_ref, gate_ref, rec_ref):
    d = D_MODEL
    for i in range(BATCH_BLOCK):
        m = mod_ref[i, 0]
        h = _norm_mod(x_ref[i], ng_ref[...], m[0:1, :], m[1:2, :]).astype(BF16)
        gate_ref[i] = jax.nn.gelu(_dot(h, w_ref[:, 0:d])).astype(BF16)
        rec_ref[i] = _dot(h, w_ref[:, d:2 * d])


def _pre_lru(xs, modsel, norm_g, w_in, *, nct):
    bsz, n_tok, d = xs.shape
    nt = n_tok // TOK_TILE
    tok_spec = pl.BlockSpec((BATCH_BLOCK, TOK_TILE, d), lambda b, t: (b, t, 0))
    return pl.pallas_call(
        _pre_lru_kernel,
        grid=(bsz // BATCH_BLOCK, nt),
        in_specs=[tok_spec, _mod_spec(nct), _resident((1, d)), _resident((d, 2 * d))],
        out_specs=[tok_spec, tok_spec],
        out_shape=[jax.ShapeDtypeStruct((bsz, n_tok, d), BF16),
                   jax.ShapeDtypeStruct((bsz, n_tok, d), F32)],
        compiler_params=_params(2),
        name="norm_mod_lru_in",
    )(xs, modsel, norm_g.reshape(1, d), w_in)


def _split_pair(q2):
    low = lax.broadcasted_iota(jnp.int32, q2.shape, 1) < HEAD_DIM
    zero = jnp.zeros_like(q2)
    return jnp.where(low, q2, zero), jnp.where(low, zero, q2)


def _exp2_parts(parts):
    m = functools.reduce(jnp.maximum, [jnp.max(s, axis=-1, keepdims=True) for s in parts])
    return [jnp.exp2(s - m) for s in parts]


def _pipeline3(n_units, scores, softmax, values):
    assert n_units % 2 == 0 and n_units >= 4
    scores(0, 0)
    scores(1, 1)
    softmax(0, 0)
    for u in range(2, n_units, 2):
        scores(u, 0)
        softmax(u - 1, 1)
        values(u - 2, 0)
        scores(u + 1, 1)
        softmax(u, 0)
        values(u - 1, 1)
    softmax(n_units - 1, 1)
    values(n_units - 2, 0)
    values(n_units - 1, 1)


def _sub_layer_norm(o, gain, lam_init):
    o = o * lax.rsqrt(jnp.mean(o * o, axis=-1, keepdims=True) + NORM_EPS) * gain
    return (o * (1.0 - lam_init)).astype(BF16)


def _scores_stage(s_buf, m_buf, q2, key_parts, bias_parts):
    for half, qh in enumerate(_split_pair(q2)):
        m, col = None, 0
        for kp, bp in zip(key_parts, bias_parts):
            s = _dot_nt(qh, kp)
            if bp is not None:
                s = s + bp[half].astype(F32)
            s_buf[half, :, col:col + kp.shape[0]] = s
            part_max = jnp.max(s, axis=-1, keepdims=True)
            m = part_max if m is None else jnp.maximum(m, part_max)
            col += kp.shape[0]
        m_buf[half] = jnp.broadcast_to(m, m_buf.shape[1:])


def _softmax_stage(s_buf, m_buf, p_buf):
    for half in range(2):
        p_buf[half] = jnp.exp2(s_buf[half] - m_buf[half][:, 0:1]).astype(BF16)


def _with_ones(v):
    return jnp.concatenate([v, jnp.ones_like(v)], axis=-1)


def _values_stage(p_buf, value_parts):
    outs = []
    for half in range(2):
        acc, col = None, 0
        for vp in value_parts:
            part = _dot(p_buf[half, :, col:col + vp.shape[0]], _with_ones(vp))
            acc = part if acc is None else acc + part
            col += vp.shape[0]
        outs.append(acc[:, :LANES] / acc[:, LANES:])
    return outs


def _direct_pair(q2, k2, v2):
    outs = []
    for qh in _split_pair(q2):
        (e,) = _exp2_parts([_dot_nt(qh, k2)])
        acc = _dot(e.astype(BF16), _with_ones(v2))
        outs.append(acc[:, :LANES] / acc[:, LANES:])
    return outs


def _attn_scratch(n_cols):
    scratch = [pltpu.VMEM((2, TOK_TILE, n_cols), F32)] * 2
    scratch += [pltpu.VMEM((2, TOK_TILE, LANES), F32)] * 2
    scratch += [pltpu.VMEM((2, TOK_TILE, n_cols), BF16)] * 2
    return scratch


def _diff_attn_kernel(lam_ref, sg_ref, q_ref, k_ref, v_ref, o_ref, s0, s1, m0, m1, p0, p1,
                      *, nct, n_ctx, lam_init):
    t = pl.program_id(1)
    lv = lam_ref[...]
    lam = (jnp.exp(jnp.sum(lv[0:1, :] * lv[1:2, :], axis=-1, keepdims=True))
           - jnp.exp(jnp.sum(lv[2:3, :] * lv[3:4, :], axis=-1, keepdims=True)) + lam_init)

    def finish(h, outs):
        o_ref[0, h] = _sub_layer_norm(outs[0] - lam * outs[1], sg_ref[...], lam_init)

    @pl.when(t < nct)
    def _():
        for h in range(DA_HEADS):
            finish(h, _direct_pair(q_ref[0, h], k_ref[0, h, 0:n_ctx, :], v_ref[0, h, 0:n_ctx, :]))

    @pl.when(t >= nct)
    def _():
        s_bufs, m_bufs, p_bufs = (s0, s1), (m0, m1), (p0, p1)
        _pipeline3(
            DA_HEADS,
            lambda h, slot: _scores_stage(s_bufs[slot], m_bufs[slot], q_ref[0, h],
                                          [k_ref[0, h]], [None]),
            lambda h, slot: _softmax_stage(s_bufs[slot], m_bufs[slot], p_bufs[slot]),
            lambda h, slot: finish(h, _values_stage(p_bufs[slot], [v_ref[0, h]])))


def _diff_attention(q, k, v, lam_vecs, subln_g, *, nct, n_ctx, lam_init):
    bsz, _, n_tok, _ = q.shape
    nt = n_tok // TOK_TILE
    tile = pl.BlockSpec((1, N_PAIRS, TOK_TILE, LANES), lambda b, t: (b, 0, t, 0))
    whole_k = pl.BlockSpec((1, DA_HEADS, n_tok, LANES), lambda b, t: (b, 0, 0, 0))
    whole_v = pl.BlockSpec((1, DA_HEADS, n_tok, LANES), lambda b, t: (b, 0, 0, 0))
    scratch = _attn_scratch(n_tok)
    return pl.pallas_call(
        functools.partial(_diff_attn_kernel, nct=nct, n_ctx=n_ctx, lam_init=lam_init),
        grid=(bsz, nt),
        in_specs=[_resident((4, HEAD_DIM)), _resident((1, LANES)), tile, whole_k, whole_v],
        out_specs=tile,
        out_shape=jax.ShapeDtypeStruct(q.shape, BF16),
        scratch_shapes=scratch,
        compiler_params=_params(2),
        name="diff_attention",
    )(lam_vecs, subln_g.reshape(1, LANES), q, k, v)


def _pair_output(outs):
    low = lax.broadcasted_iota(jnp.int32, outs[0].shape, 1) < HEAD_DIM
    return jnp.where(low, outs[0], outs[1]).astype(BF16)


def _gqa_attn_kernel(q_ref, k_ref, v_ref, o_ref, s0, s1, m0, m1, p0, p1):
    s_bufs, m_bufs, p_bufs = (s0, s1), (m0, m1), (p0, p1)

    def values(p, slot):
        o_ref[0, p] = _pair_output(_values_stage(p_bufs[slot], [v_ref[0, p // 2]]))

    _pipeline3(
        N_PAIRS,
        lambda p, slot: _scores_stage(s_bufs[slot], m_bufs[slot], q_ref[0, p],
                                      [k_ref[0, p // 2]], [None]),
        lambda p, slot: _softmax_stage(s_bufs[slot], m_bufs[slot], p_bufs[slot]),
        values)


def _gqa_attention(q, k, v, *, nct):
    bsz, _, n_tok, _ = q.shape
    nt = n_tok // TOK_TILE - nct
    tile = pl.BlockSpec((1, N_PAIRS, TOK_TILE, LANES), lambda b, t: (b, 0, t + nct, 0))
    whole_k = pl.BlockSpec((1, GQA_KV_HEADS, n_tok, LANES), lambda b, t: (b, 0, 0, 0))
    whole_v = pl.BlockSpec((1, GQA_KV_HEADS, n_tok, LANES), lambda b, t: (b, 0, 0, 0))
    scratch = _attn_scratch(n_tok)
    return pl.pallas_call(
        _gqa_attn_kernel,
        grid=(bsz, nt),
        in_specs=[tile, whole_k, whole_v],
        out_specs=tile,
        out_shape=jax.ShapeDtypeStruct(q.shape, BF16),
        scratch_shapes=scratch,
        compiler_params=_params(2),
        name="gqa_attention",
    )(q, k, v)


def _na_attn_kernel(bias_ref, q_ref, k_ref, v_ref, o_ref, s0, s1, m0, m1, p0, p1,
                    *, nct, n_ctx, n_rows):
    t = pl.program_id(1)

    @pl.when(t < nct)
    def _():
        for p in range(N_PAIRS):
            o_ref[0, p] = _pair_output(_direct_pair(
                q_ref[0, p], k_ref[0, p, 0:n_ctx, :], v_ref[0, p, 0:n_ctx, :]))

    @pl.when(t >= nct)
    def _():
        first_row = jnp.clip(NA_Q_ROWS * (t - nct) - NA_ROWS_MAX // 2, 0, n_rows - NA_K_ROWS)
        start = pl.multiple_of(n_ctx + first_row * GRID_W, GRID_W)
        s_bufs, m_bufs, p_bufs = (s0, s1), (m0, m1), (p0, p1)

        def scores(p, slot):
            keys = [k_ref[0, p, pl.ds(0, n_ctx), :], k_ref[0, p, pl.ds(start, NA_BAND), :]]
            bias = (bias_ref[0, 2 * p], bias_ref[0, 2 * p + 1])
            _scores_stage(s_bufs[slot], m_bufs[slot], q_ref[0, p], keys, [None, bias])

        def values(p, slot):
            vals = [v_ref[0, p, pl.ds(0, n_ctx), :], v_ref[0, p, pl.ds(start, NA_BAND), :]]
            o_ref[0, p] = _pair_output(_values_stage(p_bufs[slot], vals))

        _pipeline3(
            N_PAIRS, scores,
            lambda p, slot: _softmax_stage(s_bufs[slot], m_bufs[slot], p_bufs[slot]),
            values)


def _na_bias_table(rpb, n_rows):
    heads = rpb.shape[0]
    kr_rows = min(NA_ROWS_MAX, n_rows)
    variants = (0, 1, n_rows // NA_Q_ROWS - 1)
    n_tiles = len(variants)
    tt = jnp.array(variants)[:, None, None]
    qr = jnp.arange(NA_Q_ROWS)[None, :, None]
    kr = jnp.arange(NA_K_ROWS)[None, None, :]
    r = NA_Q_ROWS * tt + qr
    rp = jnp.clip(NA_Q_ROWS * tt - NA_ROWS_MAX // 2, 0, n_rows - NA_K_ROWS) + kr
    r0 = jnp.clip(r - kr_rows // 2, 0, n_rows - kr_rows)
    row_ok = (rp >= r0) & (rp < r0 + kr_rows)
    row_idx = jnp.clip(rp - r + NA_ROWS_MAX - 1, 0, 2 * NA_ROWS_MAX - 2)
    c = jnp.arange(GRID_W)[:, None]
    cp = jnp.arange(GRID_W)[None, :]
    cs = jnp.clip(c - NA_COLS // 2, 0, GRID_W - NA_COLS)
    col_ok = (cp >= cs) & (cp < cs + NA_COLS)
    col_idx = jnp.clip(cp - c, -(NA_COLS - 1), NA_COLS - 1) + NA_COLS - 1
    rows_sel = rpb[:, row_idx, :]
    onehot = (col_idx.reshape(-1)[None, :] == jnp.arange(2 * NA_COLS - 1)[:, None]).astype(F32)
    tab = jnp.dot(rows_sel.reshape(-1, 2 * NA_COLS - 1), onehot, precision=lax.Precision.HIGHEST)
    tab = tab.reshape(heads, n_tiles, NA_Q_ROWS, NA_K_ROWS, GRID_W, GRID_W)
    ok = row_ok[None, :, :, :, None, None] & col_ok[None, None, None, None, :, :]
    tab = jnp.where(ok, tab * LOG2E, MASK_VALUE)
    tab = tab.transpose(1, 0, 2, 4, 3, 5)
    return tab.reshape(n_tiles, heads, TOK_TILE, NA_BAND).astype(BF16)


def _na_attention(q, k, v, bias_tab, *, nct, n_ctx):
    bsz, _, n_tok, _ = q.shape
    nt = n_tok // TOK_TILE
    heads = bias_tab.shape[1]
    n_rows = (n_tok - n_ctx) // GRID_W
    tile = pl.BlockSpec((1, N_PAIRS, TOK_TILE, LANES), lambda b, t: (b, 0, t, 0))
    whole_k = pl.BlockSpec((1, N_PAIRS, n_tok, LANES), lambda b, t: (b, 0, 0, 0))
    whole_v = pl.BlockSpec((1, N_PAIRS, n_tok, LANES), lambda b, t: (b, 0, 0, 0))
    last = nt - nct - 1

    def bias_index(b, t):
        tile = t - nct
        return (jnp.where(tile <= 0, 0, jnp.where(tile == last, 2, 1)), 0, 0, 0)

    bias_spec = pl.BlockSpec((1, heads, TOK_TILE, NA_BAND), bias_index)
    scratch = _attn_scratch(n_ctx + NA_BAND)
    return pl.pallas_call(
        functools.partial(_na_attn_kernel, nct=nct, n_ctx=n_ctx, n_rows=n_rows),
        grid=(bsz, nt),
        in_specs=[bias_spec, tile, whole_k, whole_v],
        out_specs=tile,
        out_shape=jax.ShapeDtypeStruct(q.shape, BF16),
        scratch_shapes=scratch,
        compiler_params=_params(2),
        name="neighbourhood_attention",
    )(bias_tab, q, k, v)


LRU_CHUNK = 256
LRU_PAD = 8


def _softplus(z):
    return jnp.maximum(z, 0.0) + jnp.log1p(jnp.exp(-jnp.abs(z)))


def _sqrt_nonneg(z):
    return z * lax.rsqrt(jnp.maximum(z, float(jnp.finfo(F32).tiny)))


def _block_scan(a_ref, b_ref, slab, r0, reverse):
    n_blk = LRU_CHUNK // SUBLANES
    order = range(SUBLANES - 1, -1, -1) if reverse else range(SUBLANES)
    acc_a = acc_b = None
    for r in order:
        rows = pl.ds(r0 + r, n_blk, stride=SUBLANES)
        a_r = a_ref[slab, rows, :]
        b_r = b_ref[slab, rows, :]
        if acc_a is None:
            acc_a, acc_b = a_r, b_r
            continue
        acc_b = a_r * acc_b + b_r
        acc_a = a_r * acc_a
        a_ref[slab, rows, :] = acc_a
        b_ref[slab, rows, :] = acc_b


def _lru_kernel(rec_ref, gate_ref, cw_ref, cb_ref,
                fwa_ref, fwx_ref, bwa_ref, bwx_ref,
                fba_ref, fbx_ref, bba_ref, bbx_ref, flam_ref, blam_ref,
                o_ref, xp_ref, af_ref, bf_ref, ab_ref, bb_ref, *, n_ctx, n_tok):
    width = LRU_BLOCK
    xp_ref[0:LRU_PAD, :] = jnp.zeros((LRU_PAD, width), F32)
    xp_ref[LRU_PAD + n_tok:2 * LRU_PAD + n_tok, :] = jnp.zeros((LRU_PAD, width), F32)
    xp_ref[LRU_PAD:LRU_PAD + n_tok, :] = rec_ref[0]

    directions = (
        (fwa_ref, fwx_ref, fba_ref, fbx_ref, (-0.5 * LRU_C) * _softplus(-flam_ref[...]),
         af_ref, bf_ref, False),
        (bwa_ref, bwx_ref, bba_ref, bbx_ref, (-0.5 * LRU_C) * _softplus(-blam_ref[...]),
         ab_ref, bb_ref, True),
    )
    n_slabs = width // LANES
    wrow = lax.broadcasted_iota(jnp.int32, (LRU_CHUNK + 2 * LRU_PAD, 1), 0)

    for r0 in range(0, n_tok, LRU_CHUNK):
        xw = xp_ref[r0:r0 + LRU_CHUNK + 2 * LRU_PAD, :]
        if r0 == n_ctx:
            xw = jnp.where(wrow < LRU_PAD, 0.0, xw)
        if r0 + LRU_CHUNK == n_ctx:
            xw = jnp.where(wrow >= LRU_PAD + LRU_CHUNK, 0.0, xw)
        u = cb_ref[...]
        for j in range(LRU_CONV):
            off = LRU_PAD + j - LRU_CONV // 2
            u = u + xw[off:off + LRU_CHUNK, :] * cw_ref[j:j + 1, :]
        ub = u.astype(BF16)
        for wa_ref, wx_ref, ba_ref, bx_ref, decay, a_ref, b_ref, reverse in directions:
            log_a = decay * jnp.tanh(_dot(ub, wa_ref[0]) + ba_ref[...]) + decay
            i = 0.5 * jnp.tanh(_dot(ub, wx_ref[0]) + bx_ref[...]) + 0.5
            a = jnp.exp(log_a)
            b = _sqrt_nonneg(-jnp.tanh(log_a) * (a * a + 1.0)) * (i * u)
            for s in range(n_slabs):
                a_ref[s, r0:r0 + LRU_CHUNK, :] = a[:, s * LANES:(s + 1) * LANES]
                b_ref[s, r0:r0 + LRU_CHUNK, :] = b[:, s * LANES:(s + 1) * LANES]
                _block_scan(a_ref, b_ref, s, r0, reverse)

    n_all = n_tok // SUBLANES
    n_cb = n_ctx // SUBLANES

    def step(i, carry, rev_end):
        cf, cr = carry
        rf = pl.multiple_of(i * SUBLANES, SUBLANES)
        rb = pl.multiple_of((rev_end - i) * SUBLANES, SUBLANES)
        a_f, b_f = af_ref[:, pl.ds(rf, SUBLANES), :], bf_ref[:, pl.ds(rf, SUBLANES), :]
        a_r, b_r = ab_ref[:, pl.ds(rb, SUBLANES), :], bb_ref[:, pl.ds(rb, SUBLANES), :]
        bf_ref[:, pl.ds(rf, SUBLANES), :] = a_f * cf + b_f
        bb_ref[:, pl.ds(rb, SUBLANES), :] = a_r * cr + b_r

        def edge(x, row):
            return jnp.broadcast_to(x[:, row:row + 1, :], x.shape)
        last = SUBLANES - 1
        return (edge(a_f, last) * cf + edge(b_f, last), edge(a_r, 0) * cr + edge(b_r, 0))

    zero = jnp.zeros((n_slabs, SUBLANES, LANES), F32)
    carry = lax.fori_loop(0, n_cb, functools.partial(step, rev_end=n_cb - 1), (zero, zero),
                          unroll=2)
    lax.fori_loop(n_cb, n_all, functools.partial(step, rev_end=n_all - 1 + n_cb), carry, unroll=2)

    gate = gate_ref[0]
    for s in range(n_slabs):
        y = (bf_ref[s] + bb_ref[s]) * gate[:, s * LANES:(s + 1) * LANES].astype(F32)
        o_ref[0, s] = y.astype(BF16)


def _lru_mixer(rec, gate, p, *, n_ctx):
    (conv_w, conv_b, f_wa, f_ba, f_wx, f_bx, f_lam, b_wa, b_ba, b_wx, b_bx, b_lam) = p
    bsz, n_tok, d = rec.shape
    n_blocks = d // LRU_BLOCK
    seq = pl.BlockSpec((1, n_tok, LRU_BLOCK), lambda b, n: (b, 0, n))
    vec = pl.BlockSpec((1, LRU_BLOCK), lambda b, n: (0, n))
    mat = pl.BlockSpec((1, LRU_BLOCK, LRU_BLOCK), lambda b, n: (n, 0, 0))
    row = lambda a: a.reshape(1, d)
    half = lambda a: 0.5 * a
    assert n_ctx % LRU_CHUNK == 0 and n_tok % LRU_CHUNK == 0
    scratch = [pltpu.VMEM((n_tok + 2 * LRU_PAD, LRU_BLOCK), F32)]
    scratch += [pltpu.VMEM((LRU_BLOCK // LANES, n_tok, LANES), F32)] * 4
    return pl.pallas_call(
        functools.partial(_lru_kernel, n_ctx=n_ctx, n_tok=n_tok),
        grid=(bsz, n_blocks),
        in_specs=[seq, seq, pl.BlockSpec((LRU_CONV, LRU_BLOCK), lambda b, n: (0, n)), vec,
                  mat, mat, mat, mat, vec, vec, vec, vec, vec, vec],
        out_specs=pl.BlockSpec((1, LRU_BLOCK // LANES, n_tok, LANES), lambda b, n: (b, n, 0, 0)),
        out_shape=jax.ShapeDtypeStruct((bsz, d // LANES, n_tok, LANES), BF16),
        scratch_shapes=scratch,
        compiler_params=_params(2),
        name="rglru_scan",
    )(rec, gate, conv_w, row(conv_b),
      half(f_wa).astype(BF16), half(f_wx).astype(BF16),
      half(b_wa).astype(BF16), half(b_wx).astype(BF16),
      row(half(f_ba)), row(half(f_bx)), row(half(b_ba)), row(half(b_bx)), row(f_lam), row(b_lam))


FF_CHUNK = 1024


def _post_kernel(o_ref, *refs, n_streams, nct):
    x_refs = refs[:n_streams]
    mod_ref, ng_ref, wo_ref, w1_ref, w2_ref, out_ref = refs[n_streams:]
    for i in range(BATCH_BLOCK):
        m = mod_ref[i, 0]
        o = jnp.concatenate([o_ref[i, p] for p in range(N_PAIRS)], axis=-1)
        x1 = _stream_tile(x_refs, i, nct) + m[2:3, :] * _dot(o, wo_ref[...])
        h = _norm_mod(x1, ng_ref[...], m[3:4, :], m[4:5, :]).astype(BF16)
        acc = jnp.zeros_like(x1)
        for c in range(0, D_FF, FF_CHUNK):
            a = jnp.maximum(_dot(h, w1_ref[:, c:c + FF_CHUNK]), 0.0)
            acc = acc + _dot((a * a).astype(BF16), w2_ref[c:c + FF_CHUNK, :])
        out_ref[i] = x1 + m[5:6, :] * acc


def _post(o, xs, modsel, norm_g, w_o, w1, w2, *, nct, skip_ctx):
    bsz, _, d = xs[0].shape
    n_tok = sum(a.shape[1] for a in xs)
    first = nct if skip_ctx else 0
    nt = n_tok // TOK_TILE - first
    if len(xs) == 1:
        x_specs = [pl.BlockSpec((BATCH_BLOCK, TOK_TILE, d), lambda b, t: (b, t + first, 0))]
    else:
        assert not skip_ctx
        x_specs = _stream_specs(xs, nct)
    return pl.pallas_call(
        functools.partial(_post_kernel, n_streams=len(xs), nct=nct),
        grid=(bsz // BATCH_BLOCK, nt),
        in_specs=[pl.BlockSpec((BATCH_BLOCK, N_PAIRS, TOK_TILE, LANES),
                               lambda b, t: (b, 0, t + first, 0))]
        + x_specs
        + [_mod_spec(nct, first),
           _resident((1, d)), _resident((d, d)), _resident((d, D_FF)), _resident((D_FF, d))],
        out_specs=pl.BlockSpec((BATCH_BLOCK, TOK_TILE, d), lambda b, t: (b, t, 0)),
        out_shape=jax.ShapeDtypeStruct((bsz, nt * TOK_TILE, d), F32),
        compiler_params=_params(2),
        name="proj_residual_mlp",
    )(o, *xs, modsel, norm_g.reshape(1, d), w_o, w1, w2)


def _rope_tables(n_ctx, n_lat):
    t = jnp.arange(n_lat)
    row = (t // GRID_W).astype(F32)
    col = (t % GRID_W).astype(F32)
    n_freq = HEAD_DIM // 4
    inv = ROPE_THETA ** (-jnp.arange(n_freq, dtype=F32) / n_freq)
    ang = jnp.concatenate([row[:, None] * inv, col[:, None] * inv], axis=-1)
    cos = jnp.repeat(jnp.cos(ang), 2, axis=-1)
    sin = jnp.repeat(jnp.sin(ang), 2, axis=-1) * jnp.tile(jnp.array([-1.0, 1.0], F32), HEAD_DIM // 2)
    cos = jnp.concatenate([jnp.ones((n_ctx, HEAD_DIM), F32), cos], axis=0)
    sin = jnp.concatenate([jnp.zeros((n_ctx, HEAD_DIM), F32), sin], axis=0)
    return jnp.tile(cos, (1, LANES // HEAD_DIM)), jnp.tile(sin, (1, LANES // HEAD_DIM))


def kernel(x, c, ctx, c_ctx, norm1_g, norm2_g, w_mod, b_mod, w_mlp1, w_mlp2, a_w_qkv, a_q_norm_g, a_k_norm_g, a_lambda_q1, a_lambda_k1, a_lambda_q2, a_lambda_k2, a_subln_g, a_w_o, b_w_qkv, b_q_norm_g, b_k_norm_g, b_rpb, b_w_o, c_w_in, c_conv_w, c_conv_b, c_fwd_w_a, c_fwd_b_a, c_fwd_w_x, c_fwd_b_x, c_fwd_lam, c_bwd_w_a, c_bwd_b_a, c_bwd_w_x, c_bwd_b_x, c_bwd_lam, c_w_o, d_w_qkv, d_q_norm_g, d_k_norm_g, d_w_o):
    bsz, n_lat, d = x.shape
    n_ctx = ctx.shape[1]
    depth = w_mod.shape[0]
    assert d == D_MODEL and depth == 4 and n_ctx % TOK_TILE == 0 and n_lat % TOK_TILE == 0
    assert bsz % BATCH_BLOCK == 0
    assert (n_lat // GRID_W) % NA_Q_ROWS == 0 and n_lat // GRID_W >= NA_K_ROWS
    nct = n_ctx // TOK_TILE

    rows = -(-(bsz + 1) // SUBLANES) * SUBLANES
    c_all = jnp.concatenate([c, c_ctx[None, :], jnp.zeros((rows - bsz - 1, d), F32)], axis=0)
    mods = _modulation(c_all, w_mod, b_mod).reshape(depth, rows, N_MOD, d)
    mod_ctx = jnp.broadcast_to(mods[:, bsz][:, None], (depth, bsz, N_MOD, d))
    modsel = jnp.stack([mod_ctx, mods[:, :bsz]], axis=2)

    rope_tabs = _rope_tables(n_ctx, n_lat)
    w1 = w_mlp1.astype(BF16)
    w2 = w_mlp2.astype(BF16)

    xs = (ctx, x)
    q, k, v = _pre_attn(xs, modsel[0], norm1_g[0], a_w_qkv[0].astype(BF16), a_q_norm_g[0],
                        a_k_norm_g[0], rope_tabs, k_width=d, dup_kv=False, nct=nct)
    lam_vecs = jnp.stack([a_lambda_q1[0], a_lambda_k1[0], a_lambda_q2[0], a_lambda_k2[0]])
    lam_init = 0.8 - 0.6 * math.exp(-0.3 * 0)
    o = _diff_attention(q, k, v, lam_vecs, a_subln_g[0], nct=nct, n_ctx=n_ctx, lam_init=lam_init)
    xs = (_post(o, xs, modsel[0], norm2_g[0], a_w_o[0].astype(BF16), w1[0], w2[0],
                nct=nct, skip_ctx=False),)

    q, k, v = _pre_attn(xs, modsel[1], norm1_g[1], b_w_qkv[0].astype(BF16), b_q_norm_g[0],
                        b_k_norm_g[0], None, k_width=d, dup_kv=False, nct=nct)
    bias_tab = _na_bias_table(b_rpb[0], n_lat // GRID_W)
    o = _na_attention(q, k, v, bias_tab, nct=nct, n_ctx=n_ctx)
    xs = (_post(o, xs, modsel[1], norm2_g[1], b_w_o[0].astype(BF16), w1[1], w2[1],
                nct=nct, skip_ctx=False),)

    gate, rec = _pre_lru(xs[0], modsel[2], norm1_g[2], c_w_in[0].astype(BF16), nct=nct)
    lru_p = (c_conv_w[0], c_conv_b[0], c_fwd_w_a[0], c_fwd_b_a[0], c_fwd_w_x[0], c_fwd_b_x[0],
             c_fwd_lam[0], c_bwd_w_a[0], c_bwd_b_a[0], c_bwd_w_x[0], c_bwd_b_x[0], c_bwd_lam[0])
    o = _lru_mixer(rec, gate, lru_p, n_ctx=n_ctx)
    xs = (_post(o, xs, modsel[2], norm2_g[2], c_w_o[0].astype(BF16), w1[2], w2[2],
                nct=nct, skip_ctx=False),)

    q, k, v = _pre_attn(xs, modsel[3], norm1_g[3], d_w_qkv[0].astype(BF16), d_q_norm_g[0],
                        d_k_norm_g[0], rope_tabs, k_width=GQA_KV_HEADS * HEAD_DIM, dup_kv=True,
                        nct=nct)
    o = _gqa_attention(q, k, v, nct=nct)
    return _post(o, xs, modsel[3], norm2_g[3], d_w_o[0].astype(BF16), w1[3], w2[3],
                 nct=nct, skip_ctx=True)
```

```python
import functools
import math

import jax
import jax.numpy as jnp
from jax import lax
from jax.experimental import pallas as pl
from jax.experimental.pallas import tpu as pltpu

F32 = jnp.float32
BF16 = jnp.bfloat16

D_MODEL = 1024
HEAD_DIM = 64
GRID_W = 64
N_MOD = 6
ATTN_SCALE = HEAD_DIM ** -0.5
LOG2E = math.log2(math.e)
ROPE_THETA = 10000.0
NORM_EPS = 1e-6
DA_HEADS = D_MODEL // (2 * HEAD_DIM)
NA_ROWS_MAX = 8
NA_COLS = 16
LRU_BLOCK = 256
LRU_CONV = 4
LRU_C = 8.0
GQA_KV_HEADS = 4
D_FF = 4 * D_MODEL

LANES = 128
SUBLANES = 8
MXU_DIM = 256
VMEM_LIMIT = 56 * 1024 * 1024

TOK_TILE = 256
BATCH_BLOCK = 4
N_PAIRS = D_MODEL // LANES
MASK_VALUE = -1e30
NA_Q_ROWS = TOK_TILE // GRID_W
NA_K_ROWS = 12
NA_BAND = NA_K_ROWS * GRID_W


def _params(n_axes):
    return pltpu.CompilerParams(
        dimension_semantics=("arbitrary",) * n_axes, vmem_limit_bytes=VMEM_LIMIT)


def _resident(shape):
    zeros = (0,) * len(shape)
    return pl.BlockSpec(shape, lambda *_: zeros, pipeline_mode=pl.Buffered(1))


def _dot(a, b):
    return jnp.dot(a, b, preferred_element_type=F32)


def _dot_nt(a, b):
    return lax.dot_general(a, b, (((1,), (1,)), ((), ())), preferred_element_type=F32)


def _norm_mod(x, gain, shift, scale):
    y = x * lax.rsqrt(jnp.mean(x * x, axis=-1, keepdims=True) + NORM_EPS) * gain
    return y * (1.0 + scale) + shift


def _group_mean_matrix():
    r = lax.broadcasted_iota(jnp.int32, (MXU_DIM, MXU_DIM), 0) // HEAD_DIM
    c = lax.broadcasted_iota(jnp.int32, (MXU_DIM, MXU_DIM), 1) // HEAD_DIM
    return jnp.where(r == c, 1.0 / HEAD_DIM, 0.0).astype(BF16)


def _head_rms(x, gain, mean_mat):
    sq = (x * x).astype(BF16)
    ms = jnp.concatenate(
        [_dot(sq[:, j:j + MXU_DIM], mean_mat) for j in range(0, x.shape[1], MXU_DIM)], axis=-1)
    return x * lax.rsqrt(ms + NORM_EPS) * gain


def _rope(x, cos, sin):
    width = x.shape[1]
    reps = width // LANES
    c = jnp.concatenate([cos] * reps, axis=-1)
    s = jnp.concatenate([sin] * reps, axis=-1)
    lane = lax.broadcasted_iota(jnp.int32, x.shape, 1)
    partner = jnp.where((lane & 1) == 0, pltpu.roll(x, width - 1, 1), pltpu.roll(x, 1, 1))
    return x * c + partner * s


def _mods_kernel(c_ref, w_ref, b_ref, o_ref):
    c = c_ref[...]
    a = (c * jax.nn.sigmoid(c)).astype(BF16)
    o_ref[0] = _dot(a, w_ref[0].astype(BF16)) + b_ref[0]


def _modulation(c_all, w_mod, b_mod):
    depth, d, n = w_mod.shape
    rows = c_all.shape[0]
    tn = n // 4
    return pl.pallas_call(
        _mods_kernel,
        grid=(depth, n // tn),
        in_specs=[pl.BlockSpec((rows, d), lambda l, j: (0, 0)),
                  pl.BlockSpec((1, d, tn), lambda l, j: (l, 0, j)),
                  pl.BlockSpec((1, 1, tn), lambda l, j: (l, 0, j))],
        out_specs=pl.BlockSpec((1, rows, tn), lambda l, j: (l, 0, j)),
        out_shape=jax.ShapeDtypeStruct((depth, rows, n), F32),
        compiler_params=_params(2),
        name="adaln_modulation",
    )(c_all, w_mod, b_mod.reshape(depth, 1, n))


def _stream_tile(refs, i, nct):
    if len(refs) == 1:
        return refs[0][i]
    return jnp.where(pl.program_id(1) < nct, refs[0][i], refs[1][i])


def _stream_specs(xs, nct):
    d = xs[0].shape[-1]
    if len(xs) == 1:
        return [pl.BlockSpec((BATCH_BLOCK, TOK_TILE, d), lambda b, t: (b, t, 0))]
    return [pl.BlockSpec((BATCH_BLOCK, TOK_TILE, d), lambda b, t: (b, jnp.minimum(t, nct - 1), 0)),
            pl.BlockSpec((BATCH_BLOCK, TOK_TILE, d), lambda b, t: (b, jnp.maximum(t - nct, 0), 0))]


def _pre_attn_kernel(*refs, **static):
    mean_mat = _group_mean_matrix()
    for i in range(BATCH_BLOCK):
        _pre_attn_sample(i, mean_mat, *refs, **static)


def _pre_attn_sample(i, mean_mat, *refs, n_streams, nct, k_width, rope, dup_kv):
    x_refs, refs = refs[:n_streams], refs[n_streams:]
    if rope:
        (mod_ref, ng_ref, w_ref, qg_ref, kg_ref, cos_ref, sin_ref, q_ref, k_ref, v_ref) = refs
    else:
        mod_ref, ng_ref, w_ref, qg_ref, kg_ref, q_ref, k_ref, v_ref = refs
    q_ref, k_ref, v_ref = q_ref.at[i], k_ref.at[i], v_ref.at[i]
    m = mod_ref[i, 0]
    h = _norm_mod(_stream_tile(x_refs, i, nct), ng_ref[...], m[0:1, :], m[1:2, :]).astype(BF16)
    d = D_MODEL
    q = _head_rms(_dot(h, w_ref[:, 0:d]), qg_ref[...], mean_mat)
    k = _head_rms(_dot(h, w_ref[:, d:d + k_width]), kg_ref[...], mean_mat)
    v = _dot(h, w_ref[:, d + k_width:d + 2 * k_width])
    if rope:
        q = _rope(q, cos_ref[...], sin_ref[...])
        k = _rope(k, cos_ref[...], sin_ref[...])
    for p in range(N_PAIRS):
        q_ref[p] = q[:, p * LANES:(p + 1) * LANES].astype(BF16)
    if dup_kv:
        for g in range(k_width // HEAD_DIM):
            kh = k[:, g * HEAD_DIM:(g + 1) * HEAD_DIM]
            k_ref[g] = jnp.concatenate([kh, kh], axis=-1).astype(BF16)
    else:
        for p in range(k_width // LANES):
            k_ref[p] = k[:, p * LANES:(p + 1) * LANES].astype(BF16)
    if dup_kv:
        for g in range(k_width // HEAD_DIM):
            vh = v[:, g * HEAD_DIM:(g + 1) * HEAD_DIM]
            v_ref[g] = jnp.concatenate([vh, vh], axis=-1).astype(BF16)
    else:
        for p in range(k_width // LANES):
            v_ref[p] = v[:, p * LANES:(p + 1) * LANES].astype(BF16)


def _mod_spec(nct, first=0):
    return pl.BlockSpec((BATCH_BLOCK, 1, N_MOD, D_MODEL),
                        lambda b, t: (b, jnp.where(t + first >= nct, 1, 0), 0, 0))


def _pre_attn(xs, modsel, norm_g, w_qkv, q_gain, k_gain, rope_tabs, *, k_width, dup_kv, nct):
    bsz, _, d = xs[0].shape
    n_tok = sum(a.shape[1] for a in xs)
    nt = n_tok // TOK_TILE
    n_w = w_qkv.shape[1]
    k_slots = k_width // HEAD_DIM if dup_kv else k_width // LANES
    rope = rope_tabs is not None
    in_specs = _stream_specs(xs, nct) + [
        _mod_spec(nct),
        _resident((1, d)),
        _resident((d, n_w)),
        _resident((1, d)),
        _resident((1, k_width))]
    args = list(xs) + [modsel, norm_g.reshape(1, d), w_qkv,
                       jnp.tile(q_gain * (ATTN_SCALE * LOG2E), d // HEAD_DIM).reshape(1, d),
                       jnp.tile(k_gain, k_width // HEAD_DIM).reshape(1, k_width)]
    if rope:
        in_specs += [pl.BlockSpec((TOK_TILE, LANES), lambda b, t: (t, 0))] * 2
        args += list(rope_tabs)
    qo = jax.ShapeDtypeStruct((bsz, N_PAIRS, n_tok, LANES), BF16)
    ko = jax.ShapeDtypeStruct((bsz, k_slots, n_tok, LANES), BF16)
    vo = jax.ShapeDtypeStruct((bsz, k_slots, n_tok, LANES), BF16)
    return pl.pallas_call(
        functools.partial(_pre_attn_kernel, n_streams=len(xs), nct=nct, k_width=k_width,
                          rope=rope, dup_kv=dup_kv),
        grid=(bsz // BATCH_BLOCK, nt),
        in_specs=in_specs,
        out_specs=[
            pl.BlockSpec((BATCH_BLOCK, N_PAIRS, TOK_TILE, LANES), lambda b, t: (b, 0, t, 0)),
            pl.BlockSpec((BATCH_BLOCK, k_slots, TOK_TILE, LANES), lambda b, t: (b, 0, t, 0)),
            pl.BlockSpec((BATCH_BLOCK, k_slots, TOK_TILE, LANES), lambda b, t: (b, 0, t, 0))],
        out_shape=[qo, ko, vo],
        compiler_params=_params(2),
        name="norm_mod_qkv",
    )(*args)


def _pre_lru_kernel(x_ref, mod_ref, ng_ref, w_ref, gate_ref, rec_ref):
    d = D_MODEL
    for i in range(BATCH_BLOCK):
        m = mod_ref[i, 0]
        h = _norm_mod(x_ref[i], ng_ref[...], m[0:1, :], m[1:2, :]).astype(BF16)
        gate_ref[i] = jax.nn.gelu(_dot(h, w_ref[:, 0:d])).astype(BF16)
        rec_ref[i] = _dot(h, w_ref[:, d:2 * d])


def _pre_lru(xs, modsel, norm_g, w_in, *, nct):
    bsz, n_tok, d = xs.shape
    nt = n_tok // TOK_TILE
    tok_spec = pl.BlockSpec((BATCH_BLOCK, TOK_TILE, d), lambda b, t: (b, t, 0))
    return pl.pallas_call(
        _pre_lru_kernel,
        grid=(bsz // BATCH_BLOCK, nt),
        in_specs=[tok_spec, _mod_spec(nct), _resident((1, d)), _resident((d, 2 * d))],
        out_specs=[tok_spec, tok_spec],
        out_shape=[jax.ShapeDtypeStruct((bsz, n_tok, d), BF16),
                   jax.ShapeDtypeStruct((bsz, n_tok, d), F32)],
        compiler_params=_params(2),
        name="norm_mod_lru_in",
    )(xs, modsel, norm_g.reshape(1, d), w_in)


def _split_pair(q2):
    low = lax.broadcasted_iota(jnp.int32, q2.shape, 1) < HEAD_DIM
    zero = jnp.zeros_like(q2)
    return jnp.where(low, q2, zero), jnp.where(low, zero, q2)


def _exp2_parts(parts):
    m = functools.reduce(jnp.maximum, [jnp.max(s, axis=-1, keepdims=True) for s in parts])
    return [jnp.exp2(s - m) for s in parts]


def _pipeline3(n_units, scores, softmax, values):
    assert n_units % 2 == 0 and n_units >= 4
    scores(0, 0)
    scores(1, 1)
    softmax(0, 0)
    for u in range(2, n_units, 2):
        scores(u, 0)
        softmax(u - 1, 1)
        values(u - 2, 0)
        scores(u + 1, 1)
        softmax(u, 0)
        values(u - 1, 1)
    softmax(n_units - 1, 1)
    values(n_units - 2, 0)
    values(n_units - 1, 1)


def _sub_layer_norm(o, gain, lam_init):
    o = o * lax.rsqrt(jnp.mean(o * o, axis=-1, keepdims=True) + NORM_EPS) * gain
    return (o * (1.0 - lam_init)).astype(BF16)


def _scores_stage(s_buf, m_buf, q2, key_parts, bias_parts):
    for half, qh in enumerate(_split_pair(q2)):
        m, col = None, 0
        for kp, bp in zip(key_parts, bias_parts):
            s = _dot_nt(qh, kp)
            if bp is not None:
                s = s + bp[half].astype(F32)
            s_buf[half, :, col:col + kp.shape[0]] = s
            part_max = jnp.max(s, axis=-1, keepdims=True)
            m = part_max if m is None else jnp.maximum(m, part_max)
            col += kp.shape[0]
        m_buf[half] = jnp.broadcast_to(m, m_buf.shape[1:])


def _softmax_stage(s_buf, m_buf, p_buf):
    for half in range(2):
        p_buf[half] = jnp.exp2(s_buf[half] - m_buf[half][:, 0:1]).astype(BF16)


def _with_ones(v):
    return jnp.concatenate([v, jnp.ones_like(v)], axis=-1)


def _values_stage(p_buf, value_parts):
    outs = []
    for half in range(2):
        acc, col = None, 0
        for vp in value_parts:
            part = _dot(p_buf[half, :, col:col + vp.shape[0]], _with_ones(vp))
            acc = part if acc is None else acc + part
            col += vp.shape[0]
        outs.append(acc[:, :LANES] / acc[:, LANES:])
    return outs


def _direct_pair(q2, k2, v2):
    outs = []
    for qh in _split_pair(q2):
        (e,) = _exp2_parts([_dot_nt(qh, k2)])
        acc = _dot(e.astype(BF16), _with_ones(v2))
        outs.append(acc[:, :LANES] / acc[:, LANES:])
    return outs


def _attn_scratch(n_cols):
    scratch = [pltpu.VMEM((2, TOK_TILE, n_cols), F32)] * 2
    scratch += [pltpu.VMEM((2, TOK_TILE, LANES), F32)] * 2
    scratch += [pltpu.VMEM((2, TOK_TILE, n_cols), BF16)] * 2
    return scratch


def _diff_attn_kernel(lam_ref, sg_ref, q_ref, k_ref, v_ref, o_ref, s0, s1, m0, m1, p0, p1,
                      *, nct, n_ctx, lam_init):
    t = pl.program_id(1)
    lv = lam_ref[...]
    lam = (jnp.exp(jnp.sum(lv[0:1, :] * lv[1:2, :], axis=-1, keepdims=True))
           - jnp.exp(jnp.sum(lv[2:3, :] * lv[3:4, :], axis=-1, keepdims=True)) + lam_init)

    def finish(h, outs):
        o_ref[0, h] = _sub_layer_norm(outs[0] - lam * outs[1], sg_ref[...], lam_init)

    @pl.when(t < nct)
    def _():
        for h in range(DA_HEADS):
            finish(h, _direct_pair(q_ref[0, h], k_ref[0, h, 0:n_ctx, :], v_ref[0, h, 0:n_ctx, :]))

    @pl.when(t >= nct)
    def _():
        s_bufs, m_bufs, p_bufs = (s0, s1), (m0, m1), (p0, p1)
        _pipeline3(
            DA_HEADS,
            lambda h, slot: _scores_stage(s_bufs[slot], m_bufs[slot], q_ref[0, h],
                                          [k_ref[0, h]], [None]),
            lambda h, slot: _softmax_stage(s_bufs[slot], m_bufs[slot], p_bufs[slot]),
            lambda h, slot: finish(h, _values_stage(p_bufs[slot], [v_ref[0, h]])))


def _diff_attention(q, k, v, lam_vecs, subln_g, *, nct, n_ctx, lam_init):
    bsz, _, n_tok, _ = q.shape
    nt = n_tok // TOK_TILE
    tile = pl.BlockSpec((1, N_PAIRS, TOK_TILE, LANES), lambda b, t: (b, 0, t, 0))
    whole_k = pl.BlockSpec((1, DA_HEADS, n_tok, LANES), lambda b, t: (b, 0, 0, 0))
    whole_v = pl.BlockSpec((1, DA_HEADS, n_tok, LANES), lambda b, t: (b, 0, 0, 0))
    scratch = _attn_scratch(n_tok)
    return pl.pallas_call(
        functools.partial(_diff_attn_kernel, nct=nct, n_ctx=n_ctx, lam_init=lam_init),
        grid=(bsz, nt),
        in_specs=[_resident((4, HEAD_DIM)), _resident((1, LANES)), tile, whole_k, whole_v],
        out_specs=tile,
        out_shape=jax.ShapeDtypeStruct(q.shape, BF16),
        scratch_shapes=scratch,
        compiler_params=_params(2),
        name="diff_attention",
    )(lam_vecs, subln_g.reshape(1, LANES), q, k, v)


def _pair_output(outs):
    low = lax.broadcasted_iota(jnp.int32, outs[0].shape, 1) < HEAD_DIM
    return jnp.where(low, outs[0], outs[1]).astype(BF16)


def _gqa_attn_kernel(q_ref, k_ref, v_ref, o_ref, s0, s1, m0, m1, p0, p1):
    s_bufs, m_bufs, p_bufs = (s0, s1), (m0, m1), (p0, p1)

    def values(p, slot):
        o_ref[0, p] = _pair_output(_values_stage(p_bufs[slot], [v_ref[0, p // 2]]))

    _pipeline3(
        N_PAIRS,
        lambda p, slot: _scores_stage(s_bufs[slot], m_bufs[slot], q_ref[0, p],
                                      [k_ref[0, p // 2]], [None]),
        lambda p, slot: _softmax_stage(s_bufs[slot], m_bufs[slot], p_bufs[slot]),
        values)


def _gqa_attention(q, k, v, *, nct):
    bsz, _, n_tok, _ = q.shape
    nt = n_tok // TOK_TILE - nct
    tile = pl.BlockSpec((1, N_PAIRS, TOK_TILE, LANES), lambda b, t: (b, 0, t + nct, 0))
    whole_k = pl.BlockSpec((1, GQA_KV_HEADS, n_tok, LANES), lambda b, t: (b, 0, 0, 0))
    whole_v = pl.BlockSpec((1, GQA_KV_HEADS, n_tok, LANES), lambda b, t: (b, 0, 0, 0))
    scratch = _attn_scratch(n_tok)
    return pl.pallas_call(
        _gqa_attn_kernel,
        grid=(bsz, nt),
        in_specs=[tile, whole_k, whole_v],
        out_specs=tile,
        out_shape=jax.ShapeDtypeStruct(q.shape, BF16),
        scratch_shapes=scratch,
        compiler_params=_params(2),
        name="gqa_attention",
    )(q, k, v)


def _na_attn_kernel(bias_ref, q_ref, k_ref, v_ref, o_ref, s0, s1, m0, m1, p0, p1,
                    *, nct, n_ctx, n_rows):
    t = pl.program_id(1)

    @pl.when(t < nct)
    def _():
        for p in range(N_PAIRS):
            o_ref[0, p] = _pair_output(_direct_pair(
                q_ref[0, p], k_ref[0, p, 0:n_ctx, :], v_ref[0, p, 0:n_ctx, :]))

    @pl.when(t >= nct)
    def _():
        first_row = jnp.clip(NA_Q_ROWS * (t - nct) - NA_ROWS_MAX // 2, 0, n_rows - NA_K_ROWS)
        start = pl.multiple_of(n_ctx + first_row * GRID_W, GRID_W)
        s_bufs, m_bufs, p_bufs = (s0, s1), (m0, m1), (p0, p1)

        def scores(p, slot):
            keys = [k_ref[0, p, pl.ds(0, n_ctx), :], k_ref[0, p, pl.ds(start, NA_BAND), :]]
            bias = (bias_ref[0, 2 * p], bias_ref[0, 2 * p + 1])
            _scores_stage(s_bufs[slot], m_bufs[slot], q_ref[0, p], keys, [None, bias])

        def values(p, slot):
            vals = [v_ref[0, p, pl.ds(0, n_ctx), :], v_ref[0, p, pl.ds(start, NA_BAND), :]]
            o_ref[0, p] = _pair_output(_values_stage(p_bufs[slot], vals))

        _pipeline3(
            N_PAIRS, scores,
            lambda p, slot: _softmax_stage(s_bufs[slot], m_bufs[slot], p_bufs[slot]),
            values)


def _na_bias_table(rpb, n_rows):
    heads = rpb.shape[0]
    kr_rows = min(NA_ROWS_MAX, n_rows)
    variants = (0, 1, n_rows // NA_Q_ROWS - 1)
    n_tiles = len(variants)
    tt = jnp.array(variants)[:, None, None]
    qr = jnp.arange(NA_Q_ROWS)[None, :, None]
    kr = jnp.arange(NA_K_ROWS)[None, None, :]
    r = NA_Q_ROWS * tt + qr
    rp = jnp.clip(NA_Q_ROWS * tt - NA_ROWS_MAX // 2, 0, n_rows - NA_K_ROWS) + kr
    r0 = jnp.clip(r - kr_rows // 2, 0, n_rows - kr_rows)
    row_ok = (rp >= r0) & (rp < r0 + kr_rows)
    row_idx = jnp.clip(rp - r + NA_ROWS_MAX - 1, 0, 2 * NA_ROWS_MAX - 2)
    c = jnp.arange(GRID_W)[:, None]
    cp = jnp.arange(GRID_W)[None, :]
    cs = jnp.clip(c - NA_COLS // 2, 0, GRID_W - NA_COLS)
    col_ok = (cp >= cs) & (cp < cs + NA_COLS)
    col_idx = jnp.clip(cp - c, -(NA_COLS - 1), NA_COLS - 1) + NA_COLS - 1
    rows_sel = rpb[:, row_idx, :]
    onehot = (col_idx.reshape(-1)[None, :] == jnp.arange(2 * NA_COLS - 1)[:, None]).astype(F32)
    tab = jnp.dot(rows_sel.reshape(-1, 2 * NA_COLS - 1), onehot, precision=lax.Precision.HIGHEST)
    tab = tab.reshape(heads, n_tiles, NA_Q_ROWS, NA_K_ROWS, GRID_W, GRID_W)
    ok = row_ok[None, :, :, :, None, None] & col_ok[None, None, None, None, :, :]
    tab = jnp.where(ok, tab * LOG2E, MASK_VALUE)
    tab = tab.transpose(1, 0, 2, 4, 3, 5)
    return tab.reshape(n_tiles, heads, TOK_TILE, NA_BAND).astype(BF16)


def _na_attention(q, k, v, bias_tab, *, nct, n_ctx):
    bsz, _, n_tok, _ = q.shape
    nt = n_tok // TOK_TILE
    heads = bias_tab.shape[1]
    n_rows = (n_tok - n_ctx) // GRID_W
    tile = pl.BlockSpec((1, N_PAIRS, TOK_TILE, LANES), lambda b, t: (b, 0, t, 0))
    whole_k = pl.BlockSpec((1, N_PAIRS, n_tok, LANES), lambda b, t: (b, 0, 0, 0))
    whole_v = pl.BlockSpec((1, N_PAIRS, n_tok, LANES), lambda b, t: (b, 0, 0, 0))
    last = nt - nct - 1

    def bias_index(b, t):
        tile = t - nct
        return (jnp.where(tile <= 0, 0, jnp.where(tile == last, 2, 1)), 0, 0, 0)

    bias_spec = pl.BlockSpec((1, heads, TOK_TILE, NA_BAND), bias_index)
    scratch = _attn_scratch(n_ctx + NA_BAND)
    return pl.pallas_call(
        functools.partial(_na_attn_kernel, nct=nct, n_ctx=n_ctx, n_rows=n_rows),
        grid=(bsz, nt),
        in_specs=[bias_spec, tile, whole_k, whole_v],
        out_specs=tile,
        out_shape=jax.ShapeDtypeStruct(q.shape, BF16),
        scratch_shapes=scratch,
        compiler_params=_params(2),
        name="neighbourhood_attention",
    )(bias_tab, q, k, v)


LRU_CHUNK = 256
LRU_PAD = 8


def _softplus(z):
    return jnp.maximum(z, 0.0) + jnp.log1p(jnp.exp(-jnp.abs(z)))


def _sqrt_nonneg(z):
    return z * lax.rsqrt(jnp.maximum(z, float(jnp.finfo(F32).tiny)))


def _block_scan(a_ref, b_ref, slab, r0, reverse):
    n_blk = LRU_CHUNK // SUBLANES
    order = range(SUBLANES - 1, -1, -1) if reverse else range(SUBLANES)
    acc_a = acc_b = None
    for r in order:
        rows = pl.ds(r0 + r, n_blk, stride=SUBLANES)
        a_r = a_ref[slab, rows, :]
        b_r = b_ref[slab, rows, :]
        if acc_a is None:
            acc_a, acc_b = a_r, b_r
            continue
        acc_b = a_r * acc_b + b_r
        acc_a = a_r * acc_a
        a_ref[slab, rows, :] = acc_a
        b_ref[slab, rows, :] = acc_b


def _lru_kernel(rec_ref, gate_ref, cw_ref, cb_ref,
                fwa_ref, fwx_ref, bwa_ref, bwx_ref,
                fba_ref, fbx_ref, bba_ref, bbx_ref, flam_ref, blam_ref,
                o_ref, xp_ref, af_ref, bf_ref, ab_ref, bb_ref, *, n_ctx, n_tok):
    width = LRU_BLOCK
    xp_ref[0:LRU_PAD, :] = jnp.zeros((LRU_PAD, width), F32)
    xp_ref[LRU_PAD + n_tok:2 * LRU_PAD + n_tok, :] = jnp.zeros((LRU_PAD, width), F32)
    xp_ref[LRU_PAD:LRU_PAD + n_tok, :] = rec_ref[0]

    directions = (
        (fwa_ref, fwx_ref, fba_ref, fbx_ref, (-0.5 * LRU_C) * _softplus(-flam_ref[...]),
         af_ref, bf_ref, False),
        (bwa_ref, bwx_ref, bba_ref, bbx_ref, (-0.5 * LRU_C) * _softplus(-blam_ref[...]),
         ab_ref, bb_ref, True),
    )
    n_slabs = width // LANES
    wrow = lax.broadcasted_iota(jnp.int32, (LRU_CHUNK + 2 * LRU_PAD, 1), 0)

    for r0 in range(0, n_tok, LRU_CHUNK):
        xw = xp_ref[r0:r0 + LRU_CHUNK + 2 * LRU_PAD, :]
        if r0 == n_ctx:
            xw = jnp.where(wrow < LRU_PAD, 0.0, xw)
        if r0 + LRU_CHUNK == n_ctx:
            xw = jnp.where(wrow >= LRU_PAD + LRU_CHUNK, 0.0, xw)
        u = cb_ref[...]
        for j in range(LRU_CONV):
            off = LRU_PAD + j - LRU_CONV // 2
            u = u + xw[off:off + LRU_CHUNK, :] * cw_ref[j:j + 1, :]
        ub = u.astype(BF16)
        for wa_ref, wx_ref, ba_ref, bx_ref, decay, a_ref, b_ref, reverse in directions:
            log_a = decay * jnp.tanh(_dot(ub, wa_ref[0]) + ba_ref[...]) + decay
            i = 0.5 * jnp.tanh(_dot(ub, wx_ref[0]) + bx_ref[...]) + 0.5
            a = jnp.exp(log_a)
            b = _sqrt_nonneg(-jnp.tanh(log_a) * (a * a + 1.0)) * (i * u)
            for s in range(n_slabs):
                a_ref[s, r0:r0 + LRU_CHUNK, :] = a[:, s * LANES:(s + 1) * LANES]
                b_ref[s, r0:r0 + LRU_CHUNK, :] = b[:, s * LANES:(s + 1) * LANES]
                _block_scan(a_ref, b_ref, s, r0, reverse)

    n_all = n_tok // SUBLANES
    n_cb = n_ctx // SUBLANES

    def step(i, carry, rev_end):
        cf, cr = carry
        rf = pl.multiple_of(i * SUBLANES, SUBLANES)
        rb = pl.multiple_of((rev_end - i) * SUBLANES, SUBLANES)
        a_f, b_f = af_ref[:, pl.ds(rf, SUBLANES), :], bf_ref[:, pl.ds(rf, SUBLANES), :]
        a_r, b_r = ab_ref[:, pl.ds(rb, SUBLANES), :], bb_ref[:, pl.ds(rb, SUBLANES), :]
        bf_ref[:, pl.ds(rf, SUBLANES), :] = a_f * cf + b_f
        bb_ref[:, pl.ds(rb, SUBLANES), :] = a_r * cr + b_r

        def edge(x, row):
            return jnp.broadcast_to(x[:, row:row + 1, :], x.shape)
        last = SUBLANES - 1
        return (edge(a_f, last) * cf + edge(b_f, last), edge(a_r, 0) * cr + edge(b_r, 0))

    zero = jnp.zeros((n_slabs, SUBLANES, LANES), F32)
    carry = lax.fori_loop(0, n_cb, functools.partial(step, rev_end=n_cb - 1), (zero, zero),
                          unroll=2)
    lax.fori_loop(n_cb, n_all, functools.partial(step, rev_end=n_all - 1 + n_cb), carry, unroll=2)

    gate = gate_ref[0]
    for s in range(n_slabs):
        y = (bf_ref[s] + bb_ref[s]) * gate[:, s * LANES:(s + 1) * LANES].astype(F32)
        o_ref[0, s] = y.astype(BF16)


def _lru_mixer(rec, gate, p, *, n_ctx):
    (conv_w, conv_b, f_wa, f_ba, f_wx, f_bx, f_lam, b_wa, b_ba, b_wx, b_bx, b_lam) = p
    bsz, n_tok, d = rec.shape
    n_blocks = d // LRU_BLOCK
    seq = pl.BlockSpec((1, n_tok, LRU_BLOCK), lambda b, n: (b, 0, n))
    vec = pl.BlockSpec((1, LRU_BLOCK), lambda b, n: (0, n))
    mat = pl.BlockSpec((1, LRU_BLOCK, LRU_BLOCK), lambda b, n: (n, 0, 0))
    row = lambda a: a.reshape(1, d)
    half = lambda a: 0.5 * a
    assert n_ctx % LRU_CHUNK == 0 and n_tok % LRU_CHUNK == 0
    scratch = [pltpu.VMEM((n_tok + 2 * LRU_PAD, LRU_BLOCK), F32)]
    scratch += [pltpu.VMEM((LRU_BLOCK // LANES, n_tok, LANES), F32)] * 4
    return pl.pallas_call(
        functools.partial(_lru_kernel, n_ctx=n_ctx, n_tok=n_tok),
        grid=(bsz, n_blocks),
        in_specs=[seq, seq, pl.BlockSpec((LRU_CONV, LRU_BLOCK), lambda b, n: (0, n)), vec,
                  mat, mat, mat, mat, vec, vec, vec, vec, vec, vec],
        out_specs=pl.BlockSpec((1, LRU_BLOCK // LANES, n_tok, LANES), lambda b, n: (b, n, 0, 0)),
        out_shape=jax.ShapeDtypeStruct((bsz, d // LANES, n_tok, LANES), BF16),
        scratch_shapes=scratch,
        compiler_params=_params(2),
        name="rglru_scan",
    )(rec, gate, conv_w, row(conv_b),
      half(f_wa).astype(BF16), half(f_wx).astype(BF16),
      half(b_wa).astype(BF16), half(b_wx).astype(BF16),
      row(half(f_ba)), row(half(f_bx)), row(half(b_ba)), row(half(b_bx)), row(f_lam), row(b_lam))


FF_CHUNK = 1024


def _post_kernel(o_ref, *refs, n_streams, nct):
    x_refs = refs[:n_streams]
    mod_ref, ng_ref, wo_ref, w1_ref, w2_ref, out_ref = refs[n_streams:]
    for i in range(BATCH_BLOCK):
        m = mod_ref[i, 0]
        o = jnp.concatenate([o_ref[i, p] for p in range(N_PAIRS)], axis=-1)
        x1 = _stream_tile(x_refs, i, nct) + m[2:3, :] * _dot(o, wo_ref[...])
        h = _norm_mod(x1, ng_ref[...], m[3:4, :], m[4:5, :]).astype(BF16)
        acc = jnp.zeros_like(x1)
        for c in range(0, D_FF, FF_CHUNK):
            a = jnp.maximum(_dot(h, w1_ref[:, c:c + FF_CHUNK]), 0.0)
            acc = acc + _dot((a * a).astype(BF16), w2_ref[c:c + FF_CHUNK, :])
        out_ref[i] = x1 + m[5:6, :] * acc


def _post(o, xs, modsel, norm_g, w_o, w1, w2, *, nct, skip_ctx):
    bsz, _, d = xs[0].shape
    n_tok = sum(a.shape[1] for a in xs)
    first = nct if skip_ctx else 0
    nt = n_tok // TOK_TILE - first
    if len(xs) == 1:
        x_specs = [pl.BlockSpec((BATCH_BLOCK, TOK_TILE, d), lambda b, t: (b, t + first, 0))]
    else:
        assert not skip_ctx
        x_specs = _stream_specs(xs, nct)
    return pl.pallas_call(
        functools.partial(_post_kernel, n_streams=len(xs), nct=nct),
        grid=(bsz // BATCH_BLOCK, nt),
        in_specs=[pl.BlockSpec((BATCH_BLOCK, N_PAIRS, TOK_TILE, LANES),
                               lambda b, t: (b, 0, t + first, 0))]
        + x_specs
        + [_mod_spec(nct, first),
           _resident((1, d)), _resident((d, d)), _resident((d, D_FF)), _resident((D_FF, d))],
        out_specs=pl.BlockSpec((BATCH_BLOCK, TOK_TILE, d), lambda b, t: (b, t, 0)),
        out_shape=jax.ShapeDtypeStruct((bsz, nt * TOK_TILE, d), F32),
        compiler_params=_params(2),
        name="proj_residual_mlp",
    )(o, *xs, modsel, norm_g.reshape(1, d), w_o, w1, w2)


def _rope_tables(n_ctx, n_lat):
    t = jnp.arange(n_lat)
    row = (t // GRID_W).astype(F32)
    col = (t % GRID_W).astype(F32)
    n_freq = HEAD_DIM // 4
    inv = ROPE_THETA ** (-jnp.arange(n_freq, dtype=F32) / n_freq)
    ang = jnp.concatenate([row[:, None] * inv, col[:, None] * inv], axis=-1)
    cos = jnp.repeat(jnp.cos(ang), 2, axis=-1)
    sin = jnp.repeat(jnp.sin(ang), 2, axis=-1) * jnp.tile(jnp.array([-1.0, 1.0], F32), HEAD_DIM // 2)
    cos = jnp.concatenate([jnp.ones((n_ctx, HEAD_DIM), F32), cos], axis=0)
    sin = jnp.concatenate([jnp.zeros((n_ctx, HEAD_DIM), F32), sin], axis=0)
    return jnp.tile(cos, (1, LANES // HEAD_DIM)), jnp.tile(sin, (1, LANES // HEAD_DIM))


def kernel(x, c, ctx, c_ctx, norm1_g, norm2_g, w_mod, b_mod, w_mlp1, w_mlp2, a_w_qkv, a_q_norm_g, a_k_norm_g, a_lambda_q1, a_lambda_k1, a_lambda_q2, a_lambda_k2, a_subln_g, a_w_o, b_w_qkv, b_q_norm_g, b_k_norm_g, b_rpb, b_w_o, c_w_in, c_conv_w, c_conv_b, c_fwd_w_a, c_fwd_b_a, c_fwd_w_x, c_fwd_b_x, c_fwd_lam, c_bwd_w_a, c_bwd_b_a, c_bwd_w_x, c_bwd_b_x, c_bwd_lam, c_w_o, d_w_qkv, d_q_norm_g, d_k_norm_g, d_w_o):
    bsz, n_lat, d = x.shape
    n_ctx = ctx.shape[1]
    depth = w_mod.shape[0]
    assert d == D_MODEL and depth == 4 and n_ctx % TOK_TILE == 0 and n_lat % TOK_TILE == 0
    assert bsz % BATCH_BLOCK == 0
    assert (n_lat // GRID_W) % NA_Q_ROWS == 0 and n_lat // GRID_W >= NA_K_ROWS
    nct = n_ctx // TOK_TILE

    rows = -(-(bsz + 1) // SUBLANES) * SUBLANES
    c_all = jnp.concatenate([c, c_ctx[None, :], jnp.zeros((rows - bsz - 1, d), F32)], axis=0)
    mods = _modulation(c_all, w_mod, b_mod).reshape(depth, rows, N_MOD, d)
    mod_ctx = jnp.broadcast_to(mods[:, bsz][:, None], (depth, bsz, N_MOD, d))
    modsel = jnp.stack([mod_ctx, mods[:, :bsz]], axis=2)

    rope_tabs = _rope_tables(n_ctx, n_lat)
    w1 = w_mlp1.astype(BF16)
    w2 = w_mlp2.astype(BF16)

    xs = (ctx, x)
    q, k, v = _pre_attn(xs, modsel[0], norm1_g[0], a_w_qkv[0].astype(BF16), a_q_norm_g[0],
                        a_k_norm_g[0], rope_tabs, k_width=d, dup_kv=False, nct=nct)
    lam_vecs = jnp.stack([a_lambda_q1[0], a_lambda_k1[0], a_lambda_q2[0], a_lambda_k2[0]])
    lam_init = 0.8 - 0.6 * math.exp(-0.3 * 0)
    o = _diff_attention(q, k, v, lam_vecs, a_subln_g[0], nct=nct, n_ctx=n_ctx, lam_init=lam_init)
    xs = (_post(o, xs, modsel[0], norm2_g[0], a_w_o[0].astype(BF16), w1[0], w2[0],
                nct=nct, skip_ctx=False),)

    q, k, v = _pre_attn(xs, modsel[1], norm1_g[1], b_w_qkv[0].astype(BF16), b_q_norm_g[0],
                        b_k_norm_g[0], None, k_width=d, dup_kv=False, nct=nct)
    bias_tab = _na_bias_table(b_rpb[0], n_lat // GRID_W)
    o = _na_attention(q, k, v, bias_tab, nct=nct, n_ctx=n_ctx)
    xs = (_post(o, xs, modsel[1], norm2_g[1], b_w_o[0].astype(BF16), w1[1], w2[1],
                nct=nct, skip_ctx=False),)

    gate, rec = _pre_lru(xs[0], modsel[2], norm1_g[2], c_w_in[0].astype(BF16), nct=nct)
    lru_p = (c_conv_w[0], c_conv_b[0], c_fwd_w_a[0], c_fwd_b_a[0], c_fwd_w_x[0], c_fwd_b_x[0],
             c_fwd_lam[0], c_bwd_w_a[0], c_bwd_b_a[0], c_bwd_w_x[0], c_bwd_b_x[0], c_bwd_lam[0])
    o = _lru_mixer(rec, gate, lru_p, n_ctx=n_ctx)
    xs = (_post(o, xs, modsel[2], norm2_g[2], c_w_o[0].astype(BF16), w1[2], w2[2],
                nct=nct, skip_ctx=False),)

    q, k, v = _pre_attn(xs, modsel[3], norm1_g[3], d_w_qkv[0].astype(BF16), d_q_norm_g[0],
                        d_k_norm_g[0], rope_tabs, k_width=GQA_KV_HEADS * HEAD_DIM, dup_kv=True,
                        nct=nct)
    o = _gqa_attention(q, k, v, nct=nct)
    return _post(o, xs, modsel[3], norm2_g[3], d_w_o[0].astype(BF16), w1[3], w2[3],
                 nct=nct, skip_ctx=True)
```

```python
import functools
import math

import jax
import jax.numpy as jnp
from jax import lax
from jax.experimental import pallas as pl
from jax.experimental.pallas import tpu as pltpu

F32 = jnp.float32
BF16 = jnp.bfloat16

D_MODEL = 1024
HEAD_DIM = 64
GRID_W = 64
N_MOD = 6
ATTN_SCALE = HEAD_DIM ** -0.5
LOG2E = math.log2(math.e)
ROPE_THETA = 10000.0
NORM_EPS = 1e-6
DA_HEADS = D_MODEL // (2 * HEAD_DIM)
NA_ROWS_MAX = 8
NA_COLS = 16
LRU_BLOCK = 256
LRU_CONV = 4
LRU_C = 8.0
GQA_KV_HEADS = 4
D_FF = 4 * D_MODEL

LANES = 128
SUBLANES = 8
MXU_DIM = 256
VMEM_LIMIT = 56 * 1024 * 1024

TOK_TILE = 256
BATCH_BLOCK = 4
N_PAIRS = D_MODEL // LANES
MASK_VALUE = -1e30
NA_Q_ROWS = TOK_TILE // GRID_W
NA_K_ROWS = 12
NA_BAND = NA_K_ROWS * GRID_W


def _params(n_axes):
    return pltpu.CompilerParams(
        dimension_semantics=("arbitrary",) * n_axes, vmem_limit_bytes=VMEM_LIMIT)


def _resident(shape):
    zeros = (0,) * len(shape)
    return pl.BlockSpec(shape, lambda *_: zeros, pipeline_mode=pl.Buffered(1))


def _dot(a, b):
    return jnp.dot(a, b, preferred_element_type=F32)


def _dot_nt(a, b):
    return lax.dot_general(a, b, (((1,), (1,)), ((), ())), preferred_element_type=F32)


def _norm_mod(x, gain, shift, scale):
    y = x * lax.rsqrt(jnp.mean(x * x, axis=-1, keepdims=True) + NORM_EPS) * gain
    return y * (1.0 + scale) + shift


def _group_mean_matrix():
    r = lax.broadcasted_iota(jnp.int32, (MXU_DIM, MXU_DIM), 0) // HEAD_DIM
    c = lax.broadcasted_iota(jnp.int32, (MXU_DIM, MXU_DIM), 1) // HEAD_DIM
    return jnp.where(r == c, 1.0 / HEAD_DIM, 0.0).astype(BF16)


def _head_rms(x, gain, mean_mat):
    sq = (x * x).astype(BF16)
    ms = jnp.concatenate(
        [_dot(sq[:, j:j + MXU_DIM], mean_mat) for j in range(0, x.shape[1], MXU_DIM)], axis=-1)
    return x * lax.rsqrt(ms + NORM_EPS) * gain


def _rope(x, cos, sin):
    width = x.shape[1]
    reps = width // LANES
    c = jnp.concatenate([cos] * reps, axis=-1)
    s = jnp.concatenate([sin] * reps, axis=-1)
    lane = lax.broadcasted_iota(jnp.int32, x.shape, 1)
    partner = jnp.where((lane & 1) == 0, pltpu.roll(x, width - 1, 1), pltpu.roll(x, 1, 1))
    return x * c + partner * s


def _mods_kernel(c_ref, w_ref, b_ref, o_ref):
    c = c_ref[...]
    a = (c * jax.nn.sigmoid(c)).astype(BF16)
    o_ref[0] = _dot(a, w_ref[0].astype(BF16)) + b_ref[0]


def _modulation(c_all, w_mod, b_mod):
    depth, d, n = w_mod.shape
    rows = c_all.shape[0]
    tn = n // 4
    return pl.pallas_call(
        _mods_kernel,
        grid=(depth, n // tn),
        in_specs=[pl.BlockSpec((rows, d), lambda l, j: (0, 0)),
                  pl.BlockSpec((1, d, tn), lambda l, j: (l, 0, j)),
                  pl.BlockSpec((1, 1, tn), lambda l, j: (l, 0, j))],
        out_specs=pl.BlockSpec((1, rows, tn), lambda l, j: (l, 0, j)),
        out_shape=jax.ShapeDtypeStruct((depth, rows, n), F32),
        compiler_params=_params(2),
        name="adaln_modulation",
    )(c_all, w_mod, b_mod.reshape(depth, 1, n))


def _stream_tile(refs, i, nct):
    if len(refs) == 1:
        return refs[0][i]
    return jnp.where(pl.program_id(1) < nct, refs[0][i], refs[1][i])


def _stream_specs(xs, nct):
    d = xs[0].shape[-1]
    if len(xs) == 1:
        return [pl.BlockSpec((BATCH_BLOCK, TOK_TILE, d), lambda b, t: (b, t, 0))]
    return [pl.BlockSpec((BATCH_BLOCK, TOK_TILE, d), lambda b, t: (b, jnp.minimum(t, nct - 1), 0)),
            pl.BlockSpec((BATCH_BLOCK, TOK_TILE, d), lambda b, t: (b, jnp.maximum(t - nct, 0), 0))]


def _pre_attn_kernel(*refs, **static):
    mean_mat = _group_mean_matrix()
    for i in range(BATCH_BLOCK):
        _pre_attn_sample(i, mean_mat, *refs, **static)


def _pre_attn_sample(i, mean_mat, *refs, n_streams, nct, k_width, rope, dup_kv):
    x_refs, refs = refs[:n_streams], refs[n_streams:]
    if rope:
        (mod_ref, ng_ref, w_ref, qg_ref, kg_ref, cos_ref, sin_ref, q_ref, k_ref, v_ref) = refs
    else:
        mod_ref, ng_ref, w_ref, qg_ref, kg_ref, q_ref, k_ref, v_ref = refs
    q_ref, k_ref, v_ref = q_ref.at[i], k_ref.at[i], v_ref.at[i]
    m = mod_ref[i, 0]
    h = _norm_mod(_stream_tile(x_refs, i, nct), ng_ref[...], m[0:1, :], m[1:2, :]).astype(BF16)
    d = D_MODEL
    q = _head_rms(_dot(h, w_ref[:, 0:d]), qg_ref[...], mean_mat)
    k = _head_rms(_dot(h, w_ref[:, d:d + k_width]), kg_ref[...], mean_mat)
    v = _dot(h, w_ref[:, d + k_width:d + 2 * k_width])
    if rope:
        q = _rope(q, cos_ref[...], sin_ref[...])
        k = _rope(k, cos_ref[...], sin_ref[...])
    for p in range(N_PAIRS):
        q_ref[p] = q[:, p * LANES:(p + 1) * LANES].astype(BF16)
    if dup_kv:
        for g in range(k_width // HEAD_DIM):
            kh = k[:, g * HEAD_DIM:(g + 1) * HEAD_DIM]
            k_ref[g] = jnp.concatenate([kh, kh], axis=-1).astype(BF16)
    else:
        for p in range(k_width // LANES):
            k_ref[p] = k[:, p * LANES:(p + 1) * LANES].astype(BF16)
    if dup_kv:
        for g in range(k_width // HEAD_DIM):
            vh = v[:, g * HEAD_DIM:(g + 1) * HEAD_DIM]
            v_ref[g] = jnp.concatenate([vh, vh], axis=-1).astype(BF16)
    else:
        for p in range(k_width // LANES):
            v_ref[p] = v[:, p * LANES:(p + 1) * LANES].astype(BF16)


def _mod_spec(nct, first=0):
    return pl.BlockSpec((BATCH_BLOCK, 1, N_MOD, D_MODEL),
                        lambda b, t: (b, jnp.where(t + first >= nct, 1, 0), 0, 0))


def _pre_attn(xs, modsel, norm_g, w_qkv, q_gain, k_gain, rope_tabs, *, k_width, dup_kv, nct):
    bsz, _, d = xs[0].shape
    n_tok = sum(a.shape[1] for a in xs)
    nt = n_tok // TOK_TILE
    n_w = w_qkv.shape[1]
    k_slots = k_width // HEAD_DIM if dup_kv else k_width // LANES
    rope = rope_tabs is not None
    in_specs = _stream_specs(xs, nct) + [
        _mod_spec(nct),
        _resident((1, d)),
        _resident((d, n_w)),
        _resident((1, d)),
        _resident((1, k_width))]
    args = list(xs) + [modsel, norm_g.reshape(1, d), w_qkv,
                       jnp.tile(q_gain * (ATTN_SCALE * LOG2E), d // HEAD_DIM).reshape(1, d),
                       jnp.tile(k_gain, k_width // HEAD_DIM).reshape(1, k_width)]
    if rope:
        in_specs += [pl.BlockSpec((TOK_TILE, LANES), lambda b, t: (t, 0))] * 2
        args += list(rope_tabs)
    qo = jax.ShapeDtypeStruct((bsz, N_PAIRS, n_tok, LANES), BF16)
    ko = jax.ShapeDtypeStruct((bsz, k_slots, n_tok, LANES), BF16)
    vo = jax.ShapeDtypeStruct((bsz, k_slots, n_tok, LANES), BF16)
    return pl.pallas_call(
        functools.partial(_pre_attn_kernel, n_streams=len(xs), nct=nct, k_width=k_width,
                          rope=rope, dup_kv=dup_kv),
        grid=(bsz // BATCH_BLOCK, nt),
        in_specs=in_specs,
        out_specs=[
            pl.BlockSpec((BATCH_BLOCK, N_PAIRS, TOK_TILE, LANES), lambda b, t: (b, 0, t, 0)),
            pl.BlockSpec((BATCH_BLOCK, k_slots, TOK_TILE, LANES), lambda b, t: (b, 0, t, 0)),
            pl.BlockSpec((BATCH_BLOCK, k_slots, TOK_TILE, LANES), lambda b, t: (b, 0, t, 0))],
        out_shape=[qo, ko, vo],
        compiler_params=_params(2),
        name="norm_mod_qkv",
    )(*args)


def _pre_lru_kernel(x_ref, mod_ref, ng_ref, w_ref, gate_ref, rec_ref):
    d = D_MODEL
    for i in range(BATCH_BLOCK):
        m = mod_ref[i, 0]
        h = _norm_mod(x_ref[i], ng_ref[...], m[0:1, :], m[1:2, :]).astype(BF16)
        gate_ref[i] = jax.nn.gelu(_dot(h, w_ref[:, 0:d])).astype(BF16)
        rec_ref[i] = _dot(h, w_ref[:, d:2 * d])


def _pre_lru(xs, modsel, norm_g, w_in, *, nct):
    bsz, n_tok, d = xs.shape
    nt = n_tok // TOK_TILE
    tok_spec = pl.BlockSpec((BATCH_BLOCK, TOK_TILE, d), lambda b, t: (b, t, 0))
    return pl.pallas_call(
        _pre_lru_kernel,
        grid=(bsz // BATCH_BLOCK, nt),
        in_specs=[tok_spec, _mod_spec(nct), _resident((1, d)), _resident((d, 2 * d))],
        out_specs=[tok_spec, tok_spec],
        out_shape=[jax.ShapeDtypeStruct((bsz, n_tok, d), BF16),
                   jax.ShapeDtypeStruct((bsz, n_tok, d), F32)],
        compiler_params=_params(2),
        name="norm_mod_lru_in",
    )(xs, modsel, norm_g.reshape(1, d), w_in)


def _split_pair(q2):
    low = lax.broadcasted_iota(jnp.int32, q2.shape, 1) < HEAD_DIM
    zero = jnp.zeros_like(q2)
    return jnp.where(low, q2, zero), jnp.where(low, zero, q2)


def _exp2_parts(parts):
    m = functools.reduce(jnp.maximum, [jnp.max(s, axis=-1, keepdims=True) for s in parts])
    return [jnp.exp2(s - m) for s in parts]


def _pipeline3(n_units, scores, softmax, values):
    assert n_units % 2 == 0 and n_units >= 4
    scores(0, 0)
    scores(1, 1)
    softmax(0, 0)
    for u in range(2, n_units, 2):
        scores(u, 0)
        softmax(u - 1, 1)
        values(u - 2, 0)
        scores(u + 1, 1)
        softmax(u, 0)
        values(u - 1, 1)
    softmax(n_units - 1, 1)
    values(n_units - 2, 0)
    values(n_units - 1, 1)


def _sub_layer_norm(o, gain, lam_init):
    o = o * lax.rsqrt(jnp.mean(o * o, axis=-1, keepdims=True) + NORM_EPS) * gain
    return (o * (1.0 - lam_init)).astype(BF16)


def _scores_stage(s_buf, m_buf, q2, key_parts, bias_parts):
    for half, qh in enumerate(_split_pair(q2)):
        m, col = None, 0
        for kp, bp in zip(key_parts, bias_parts):
            s = _dot_nt(qh, kp)
            if bp is not None:
                s = s + bp[half].astype(F32)
            s_buf[half, :, col:col + kp.shape[0]] = s
            part_max = jnp.max(s, axis=-1, keepdims=True)
            m = part_max if m is None else jnp.maximum(m, part_max)
            col += kp.shape[0]
        m_buf[half] = jnp.broadcast_to(m, m_buf.shape[1:])


def _softmax_stage(s_buf, m_buf, p_buf):
    for half in range(2):
        p_buf[half] = jnp.exp2(s_buf[half] - m_buf[half][:, 0:1]).astype(BF16)


def _with_ones(v):
    return jnp.concatenate([v, jnp.ones_like(v)], axis=-1)


def _values_stage(p_buf, value_parts):
    outs = []
    for half in range(2):
        acc, col = None, 0
        for vp in value_parts:
            part = _dot(p_buf[half, :, col:col + vp.shape[0]], _with_ones(vp))
            acc = part if acc is None else acc + part
            col += vp.shape[0]
        outs.append(acc[:, :LANES] / acc[:, LANES:])
    return outs


def _direct_pair(q2, k2, v2):
    outs = []
    for qh in _split_pair(q2):
        (e,) = _exp2_parts([_dot_nt(qh, k2)])
        acc = _dot(e.astype(BF16), _with_ones(v2))
        outs.append(acc[:, :LANES] / acc[:, LANES:])
    return outs


def _attn_scratch(n_cols):
    scratch = [pltpu.VMEM((2, TOK_TILE, n_cols), F32)] * 2
    scratch += [pltpu.VMEM((2, TOK_TILE, LANES), F32)] * 2
    scratch += [pltpu.VMEM((2, TOK_TILE, n_cols), BF16)] * 2
    return scratch


def _diff_attn_kernel(lam_ref, sg_ref, q_ref, k_ref, v_ref, o_ref, s0, s1, m0, m1, p0, p1,
                      *, nct, n_ctx, lam_init):
    t = pl.program_id(1)
    lv = lam_ref[...]
    lam = (jnp.exp(jnp.sum(lv[0:1, :] * lv[1:2, :], axis=-1, keepdims=True))
           - jnp.exp(jnp.sum(lv[2:3, :] * lv[3:4, :], axis=-1, keepdims=True)) + lam_init)

    def finish(h, outs):
        o_ref[0, h] = _sub_layer_norm(outs[0] - lam * outs[1], sg_ref[...], lam_init)

    @pl.when(t < nct)
    def _():
        for h in range(DA_HEADS):
            finish(h, _direct_pair(q_ref[0, h], k_ref[0, h, 0:n_ctx, :], v_ref[0, h, 0:n_ctx, :]))

    @pl.when(t >= nct)
    def _():
        s_bufs, m_bufs, p_bufs = (s0, s1), (m0, m1), (p0, p1)
        _pipeline3(
            DA_HEADS,
            lambda h, slot: _scores_stage(s_bufs[slot], m_bufs[slot], q_ref[0, h],
                                          [k_ref[0, h]], [None]),
            lambda h, slot: _softmax_stage(s_bufs[slot], m_bufs[slot], p_bufs[slot]),
            lambda h, slot: finish(h, _values_stage(p_bufs[slot], [v_ref[0, h]])))


def _diff_attention(q, k, v, lam_vecs, subln_g, *, nct, n_ctx, lam_init):
    bsz, _, n_tok, _ = q.shape
    nt = n_tok // TOK_TILE
    tile = pl.BlockSpec((1, N_PAIRS, TOK_TILE, LANES), lambda b, t: (b, 0, t, 0))
    whole_k = pl.BlockSpec((1, DA_HEADS, n_tok, LANES), lambda b, t: (b, 0, 0, 0))
    whole_v = pl.BlockSpec((1, DA_HEADS, n_tok, LANES), lambda b, t: (b, 0, 0, 0))
    scratch = _attn_scratch(n_tok)
    return pl.pallas_call(
        functools.partial(_diff_attn_kernel, nct=nct, n_ctx=n_ctx, lam_init=lam_init),
        grid=(bsz, nt),
        in_specs=[_resident((4, HEAD_DIM)), _resident((1, LANES)), tile, whole_k, whole_v],
        out_specs=tile,
        out_shape=jax.ShapeDtypeStruct(q.shape, BF16),
        scratch_shapes=scratch,
        compiler_params=_params(2),
        name="diff_attention",
    )(lam_vecs, subln_g.reshape(1, LANES), q, k, v)


def _pair_output(outs):
    low = lax.broadcasted_iota(jnp.int32, outs[0].shape, 1) < HEAD_DIM
    return jnp.where(low, outs[0], outs[1]).astype(BF16)


def _gqa_attn_kernel(q_ref, k_ref, v_ref, o_ref, s0, s1, m0, m1, p0, p1):
    s_bufs, m_bufs, p_bufs = (s0, s1), (m0, m1), (p0, p1)

    def scores(u, slot):
        i, p = divmod(u, N_PAIRS)
        _scores_stage(s_bufs[slot], m_bufs[slot], q_ref[i, p], [k_ref[i, p // 2]], [None])

    def values(u, slot):
        i, p = divmod(u, N_PAIRS)
        o_ref[i, p] = _pair_output(_values_stage(p_bufs[slot], [v_ref[i, p // 2]]))

    _pipeline3(
        GQA_BATCH_BLOCK * N_PAIRS, scores,
        lambda u, slot: _softmax_stage(s_bufs[slot], m_bufs[slot], p_bufs[slot]),
        values)


GQA_BATCH_BLOCK = 2


def _gqa_attention(q, k, v, *, nct):
    bsz, _, n_tok, _ = q.shape
    nt = n_tok // TOK_TILE - nct
    bb = GQA_BATCH_BLOCK
    tile = pl.BlockSpec((bb, N_PAIRS, TOK_TILE, LANES), lambda b, t: (b, 0, t + nct, 0))
    whole_k = pl.BlockSpec((bb, GQA_KV_HEADS, n_tok, LANES), lambda b, t: (b, 0, 0, 0))
    whole_v = pl.BlockSpec((bb, GQA_KV_HEADS, n_tok, LANES), lambda b, t: (b, 0, 0, 0))
    scratch = _attn_scratch(n_tok)
    return pl.pallas_call(
        _gqa_attn_kernel,
        grid=(bsz // bb, nt),
        in_specs=[tile, whole_k, whole_v],
        out_specs=tile,
        out_shape=jax.ShapeDtypeStruct(q.shape, BF16),
        scratch_shapes=scratch,
        compiler_params=_params(2),
        name="gqa_attention",
    )(q, k, v)


def _na_attn_kernel(bias_ref, q_ref, k_ref, v_ref, o_ref, s0, s1, m0, m1, p0, p1,
                    *, nct, n_ctx, n_rows):
    t = pl.program_id(1)

    @pl.when(t < nct)
    def _():
        for p in range(N_PAIRS):
            o_ref[0, p] = _pair_output(_direct_pair(
                q_ref[0, p], k_ref[0, p, 0:n_ctx, :], v_ref[0, p, 0:n_ctx, :]))

    @pl.when(t >= nct)
    def _():
        first_row = jnp.clip(NA_Q_ROWS * (t - nct) - NA_ROWS_MAX // 2, 0, n_rows - NA_K_ROWS)
        start = pl.multiple_of(n_ctx + first_row * GRID_W, GRID_W)
        s_bufs, m_bufs, p_bufs = (s0, s1), (m0, m1), (p0, p1)

        def scores(p, slot):
            keys = [k_ref[0, p, pl.ds(0, n_ctx), :], k_ref[0, p, pl.ds(start, NA_BAND), :]]
            bias = (bias_ref[0, 2 * p], bias_ref[0, 2 * p + 1])
            _scores_stage(s_bufs[slot], m_bufs[slot], q_ref[0, p], keys, [None, bias])

        def values(p, slot):
            vals = [v_ref[0, p, pl.ds(0, n_ctx), :], v_ref[0, p, pl.ds(start, NA_BAND), :]]
            o_ref[0, p] = _pair_output(_values_stage(p_bufs[slot], vals))

        _pipeline3(
            N_PAIRS, scores,
            lambda p, slot: _softmax_stage(s_bufs[slot], m_bufs[slot], p_bufs[slot]),
            values)


def _na_bias_table(rpb, n_rows):
    heads = rpb.shape[0]
    kr_rows = min(NA_ROWS_MAX, n_rows)
    variants = (0, 1, n_rows // NA_Q_ROWS - 1)
    n_tiles = len(variants)
    tt = jnp.array(variants)[:, None, None]
    qr = jnp.arange(NA_Q_ROWS)[None, :, None]
    kr = jnp.arange(NA_K_ROWS)[None, None, :]
    r = NA_Q_ROWS * tt + qr
    rp = jnp.clip(NA_Q_ROWS * tt - NA_ROWS_MAX // 2, 0, n_rows - NA_K_ROWS) + kr
    r0 = jnp.clip(r - kr_rows // 2, 0, n_rows - kr_rows)
    row_ok = (rp >= r0) & (rp < r0 + kr_rows)
    row_idx = jnp.clip(rp - r + NA_ROWS_MAX - 1, 0, 2 * NA_ROWS_MAX - 2)
    c = jnp.arange(GRID_W)[:, None]
    cp = jnp.arange(GRID_W)[None, :]
    cs = jnp.clip(c - NA_COLS // 2, 0, GRID_W - NA_COLS)
    col_ok = (cp >= cs) & (cp < cs + NA_COLS)
    col_idx = jnp.clip(cp - c, -(NA_COLS - 1), NA_COLS - 1) + NA_COLS - 1
    rows_sel = rpb[:, row_idx, :]
    onehot = (col_idx.reshape(-1)[None, :] == jnp.arange(2 * NA_COLS - 1)[:, None]).astype(F32)
    tab = jnp.dot(rows_sel.reshape(-1, 2 * NA_COLS - 1), onehot, precision=lax.Precision.HIGHEST)
    tab = tab.reshape(heads, n_tiles, NA_Q_ROWS, NA_K_ROWS, GRID_W, GRID_W)
    ok = row_ok[None, :, :, :, None, None] & col_ok[None, None, None, None, :, :]
    tab = jnp.where(ok, tab * LOG2E, MASK_VALUE)
    tab = tab.transpose(1, 0, 2, 4, 3, 5)
    return tab.reshape(n_tiles, heads, TOK_TILE, NA_BAND).astype(BF16)


def _na_attention(q, k, v, bias_tab, *, nct, n_ctx):
    bsz, _, n_tok, _ = q.shape
    nt = n_tok // TOK_TILE
    heads = bias_tab.shape[1]
    n_rows = (n_tok - n_ctx) // GRID_W
    tile = pl.BlockSpec((1, N_PAIRS, TOK_TILE, LANES), lambda b, t: (b, 0, t, 0))
    whole_k = pl.BlockSpec((1, N_PAIRS, n_tok, LANES), lambda b, t: (b, 0, 0, 0))
    whole_v = pl.BlockSpec((1, N_PAIRS, n_tok, LANES), lambda b, t: (b, 0, 0, 0))
    last = nt - nct - 1

    def bias_index(b, t):
        tile = t - nct
        return (jnp.where(tile <= 0, 0, jnp.where(tile == last, 2, 1)), 0, 0, 0)

    bias_spec = pl.BlockSpec((1, heads, TOK_TILE, NA_BAND), bias_index)
    scratch = _attn_scratch(n_ctx + NA_BAND)
    return pl.pallas_call(
        functools.partial(_na_attn_kernel, nct=nct, n_ctx=n_ctx, n_rows=n_rows),
        grid=(bsz, nt),
        in_specs=[bias_spec, tile, whole_k, whole_v],
        out_specs=tile,
        out_shape=jax.ShapeDtypeStruct(q.shape, BF16),
        scratch_shapes=scratch,
        compiler_params=_params(2),
        name="neighbourhood_attention",
    )(bias_tab, q, k, v)


LRU_CHUNK = 256
LRU_PAD = 8


def _softplus(z):
    return jnp.maximum(z, 0.0) + jnp.log1p(jnp.exp(-jnp.abs(z)))


def _sqrt_nonneg(z):
    return z * lax.rsqrt(jnp.maximum(z, float(jnp.finfo(F32).tiny)))


def _block_scan(a_ref, b_ref, slab, r0, reverse):
    n_blk = LRU_CHUNK // SUBLANES
    order = range(SUBLANES - 1, -1, -1) if reverse else range(SUBLANES)
    acc_a = acc_b = None
    for r in order:
        rows = pl.ds(r0 + r, n_blk, stride=SUBLANES)
        a_r = a_ref[slab, rows, :]
        b_r = b_ref[slab, rows, :]
        if acc_a is None:
            acc_a, acc_b = a_r, b_r
            continue
        acc_b = a_r * acc_b + b_r
        acc_a = a_r * acc_a
        a_ref[slab, rows, :] = acc_a
        b_ref[slab, rows, :] = acc_b


def _lru_kernel(rec_ref, gate_ref, cw_ref, cb_ref,
                fwa_ref, fwx_ref, bwa_ref, bwx_ref,
                fba_ref, fbx_ref, bba_ref, bbx_ref, flam_ref, blam_ref,
                o_ref, xp_ref, af_ref, bf_ref, ab_ref, bb_ref, *, n_ctx, n_tok):
    width = LRU_BLOCK
    xp_ref[0:LRU_PAD, :] = jnp.zeros((LRU_PAD, width), F32)
    xp_ref[LRU_PAD + n_tok:2 * LRU_PAD + n_tok, :] = jnp.zeros((LRU_PAD, width), F32)
    xp_ref[LRU_PAD:LRU_PAD + n_tok, :] = rec_ref[0]

    directions = (
        (fwa_ref, fwx_ref, fba_ref, fbx_ref, (-0.5 * LRU_C) * _softplus(-flam_ref[...]),
         af_ref, bf_ref, False),
        (bwa_ref, bwx_ref, bba_ref, bbx_ref, (-0.5 * LRU_C) * _softplus(-blam_ref[...]),
         ab_ref, bb_ref, True),
    )
    n_slabs = width // LANES
    wrow = lax.broadcasted_iota(jnp.int32, (LRU_CHUNK + 2 * LRU_PAD, 1), 0)

    for r0 in range(0, n_tok, LRU_CHUNK):
        xw = xp_ref[r0:r0 + LRU_CHUNK + 2 * LRU_PAD, :]
        if r0 == n_ctx:
            xw = jnp.where(wrow < LRU_PAD, 0.0, xw)
        if r0 + LRU_CHUNK == n_ctx:
            xw = jnp.where(wrow >= LRU_PAD + LRU_CHUNK, 0.0, xw)
        u = cb_ref[...]
        for j in range(LRU_CONV):
            off = LRU_PAD + j - LRU_CONV // 2
            u = u + xw[off:off + LRU_CHUNK, :] * cw_ref[j:j + 1, :]
        ub = u.astype(BF16)
        for wa_ref, wx_ref, ba_ref, bx_ref, decay, a_ref, b_ref, reverse in directions:
            log_a = decay * jnp.tanh(_dot(ub, wa_ref[0]) + ba_ref[...]) + decay
            i = 0.5 * jnp.tanh(_dot(ub, wx_ref[0]) + bx_ref[...]) + 0.5
            a = jnp.exp(log_a)
            b = _sqrt_nonneg(-jnp.tanh(log_a) * (a * a + 1.0)) * (i * u)
            for s in range(n_slabs):
                a_ref[s, r0:r0 + LRU_CHUNK, :] = a[:, s * LANES:(s + 1) * LANES]
                b_ref[s, r0:r0 + LRU_CHUNK, :] = b[:, s * LANES:(s + 1) * LANES]
                _block_scan(a_ref, b_ref, s, r0, reverse)

    n_all = n_tok // SUBLANES
    n_cb = n_ctx // SUBLANES

    def step(i, carry, rev_end):
        cf, cr = carry
        rf = pl.multiple_of(i * SUBLANES, SUBLANES)
        rb = pl.multiple_of((rev_end - i) * SUBLANES, SUBLANES)
        a_f, b_f = af_ref[:, pl.ds(rf, SUBLANES), :], bf_ref[:, pl.ds(rf, SUBLANES), :]
        a_r, b_r = ab_ref[:, pl.ds(rb, SUBLANES), :], bb_ref[:, pl.ds(rb, SUBLANES), :]
        bf_ref[:, pl.ds(rf, SUBLANES), :] = a_f * cf + b_f
        bb_ref[:, pl.ds(rb, SUBLANES), :] = a_r * cr + b_r

        def edge(x, row):
            return jnp.broadcast_to(x[:, row:row + 1, :], x.shape)
        last = SUBLANES - 1
        return (edge(a_f, last) * cf + edge(b_f, last), edge(a_r, 0) * cr + edge(b_r, 0))

    zero = jnp.zeros((n_slabs, SUBLANES, LANES), F32)
    carry = lax.fori_loop(0, n_cb, functools.partial(step, rev_end=n_cb - 1), (zero, zero),
                          unroll=2)
    lax.fori_loop(n_cb, n_all, functools.partial(step, rev_end=n_all - 1 + n_cb), carry, unroll=2)

    gate = gate_ref[0]
    for s in range(n_slabs):
        y = (bf_ref[s] + bb_ref[s]) * gate[:, s * LANES:(s + 1) * LANES].astype(F32)
        o_ref[0, s] = y.astype(BF16)


def _lru_mixer(rec, gate, p, *, n_ctx):
    (conv_w, conv_b, f_wa, f_ba, f_wx, f_bx, f_lam, b_wa, b_ba, b_wx, b_bx, b_lam) = p
    bsz, n_tok, d = rec.shape
    n_blocks = d // LRU_BLOCK
    seq = pl.BlockSpec((1, n_tok, LRU_BLOCK), lambda b, n: (b, 0, n))
    vec = pl.BlockSpec((1, LRU_BLOCK), lambda b, n: (0, n))
    mat = pl.BlockSpec((1, LRU_BLOCK, LRU_BLOCK), lambda b, n: (n, 0, 0))
    row = lambda a: a.reshape(1, d)
    half = lambda a: 0.5 * a
    assert n_ctx % LRU_CHUNK == 0 and n_tok % LRU_CHUNK == 0
    scratch = [pltpu.VMEM((n_tok + 2 * LRU_PAD, LRU_BLOCK), F32)]
    scratch += [pltpu.VMEM((LRU_BLOCK // LANES, n_tok, LANES), F32)] * 4
    return pl.pallas_call(
        functools.partial(_lru_kernel, n_ctx=n_ctx, n_tok=n_tok),
        grid=(bsz, n_blocks),
        in_specs=[seq, seq, pl.BlockSpec((LRU_CONV, LRU_BLOCK), lambda b, n: (0, n)), vec,
                  mat, mat, mat, mat, vec, vec, vec, vec, vec, vec],
        out_specs=pl.BlockSpec((1, LRU_BLOCK // LANES, n_tok, LANES), lambda b, n: (b, n, 0, 0)),
        out_shape=jax.ShapeDtypeStruct((bsz, d // LANES, n_tok, LANES), BF16),
        scratch_shapes=scratch,
        compiler_params=_params(2),
        name="rglru_scan",
    )(rec, gate, conv_w, row(conv_b),
      half(f_wa).astype(BF16), half(f_wx).astype(BF16),
      half(b_wa).astype(BF16), half(b_wx).astype(BF16),
      row(half(f_ba)), row(half(f_bx)), row(half(b_ba)), row(half(b_bx)), row(f_lam), row(b_lam))


FF_CHUNK = 1024


def _post_kernel(o_ref, *refs, n_streams, nct):
    x_refs = refs[:n_streams]
    mod_ref, ng_ref, wo_ref, w1_ref, w2_ref, out_ref = refs[n_streams:]
    for i in range(BATCH_BLOCK):
        m = mod_ref[i, 0]
        o = jnp.concatenate([o_ref[i, p] for p in range(N_PAIRS)], axis=-1)
        x1 = _stream_tile(x_refs, i, nct) + m[2:3, :] * _dot(o, wo_ref[...])
        h = _norm_mod(x1, ng_ref[...], m[3:4, :], m[4:5, :]).astype(BF16)
        acc = jnp.zeros_like(x1)
        for c in range(0, D_FF, FF_CHUNK):
            a = jnp.maximum(_dot(h, w1_ref[:, c:c + FF_CHUNK]), 0.0)
            acc = acc + _dot((a * a).astype(BF16), w2_ref[c:c + FF_CHUNK, :])
        out_ref[i] = x1 + m[5:6, :] * acc


def _post(o, xs, modsel, norm_g, w_o, w1, w2, *, nct, skip_ctx):
    bsz, _, d = xs[0].shape
    n_tok = sum(a.shape[1] for a in xs)
    first = nct if skip_ctx else 0
    nt = n_tok // TOK_TILE - first
    if len(xs) == 1:
        x_specs = [pl.BlockSpec((BATCH_BLOCK, TOK_TILE, d), lambda b, t: (b, t + first, 0))]
    else:
        assert not skip_ctx
        x_specs = _stream_specs(xs, nct)
    return pl.pallas_call(
        functools.partial(_post_kernel, n_streams=len(xs), nct=nct),
        grid=(bsz // BATCH_BLOCK, nt),
        in_specs=[pl.BlockSpec((BATCH_BLOCK, N_PAIRS, TOK_TILE, LANES),
                               lambda b, t: (b, 0, t + first, 0))]
        + x_specs
        + [_mod_spec(nct, first),
           _resident((1, d)), _resident((d, d)), _resident((d, D_FF)), _resident((D_FF, d))],
        out_specs=pl.BlockSpec((BATCH_BLOCK, TOK_TILE, d), lambda b, t: (b, t, 0)),
        out_shape=jax.ShapeDtypeStruct((bsz, nt * TOK_TILE, d), F32),
        compiler_params=_params(2),
        name="proj_residual_mlp",
    )(o, *xs, modsel, norm_g.reshape(1, d), w_o, w1, w2)


def _rope_tables(n_ctx, n_lat):
    t = jnp.arange(n_lat)
    row = (t // GRID_W).astype(F32)
    col = (t % GRID_W).astype(F32)
    n_freq = HEAD_DIM // 4
    inv = ROPE_THETA ** (-jnp.arange(n_freq, dtype=F32) / n_freq)
    ang = jnp.concatenate([row[:, None] * inv, col[:, None] * inv], axis=-1)
    cos = jnp.repeat(jnp.cos(ang), 2, axis=-1)
    sin = jnp.repeat(jnp.sin(ang), 2, axis=-1) * jnp.tile(jnp.array([-1.0, 1.0], F32), HEAD_DIM // 2)
    cos = jnp.concatenate([jnp.ones((n_ctx, HEAD_DIM), F32), cos], axis=0)
    sin = jnp.concatenate([jnp.zeros((n_ctx, HEAD_DIM), F32), sin], axis=0)
    return jnp.tile(cos, (1, LANES // HEAD_DIM)), jnp.tile(sin, (1, LANES // HEAD_DIM))


def kernel(x, c, ctx, c_ctx, norm1_g, norm2_g, w_mod, b_mod, w_mlp1, w_mlp2, a_w_qkv, a_q_norm_g, a_k_norm_g, a_lambda_q1, a_lambda_k1, a_lambda_q2, a_lambda_k2, a_subln_g, a_w_o, b_w_qkv, b_q_norm_g, b_k_norm_g, b_rpb, b_w_o, c_w_in, c_conv_w, c_conv_b, c_fwd_w_a, c_fwd_b_a, c_fwd_w_x, c_fwd_b_x, c_fwd_lam, c_bwd_w_a, c_bwd_b_a, c_bwd_w_x, c_bwd_b_x, c_bwd_lam, c_w_o, d_w_qkv, d_q_norm_g, d_k_norm_g, d_w_o):
    bsz, n_lat, d = x.shape
    n_ctx = ctx.shape[1]
    depth = w_mod.shape[0]
    assert d == D_MODEL and depth == 4 and n_ctx % TOK_TILE == 0 and n_lat % TOK_TILE == 0
    assert bsz % BATCH_BLOCK == 0
    assert (n_lat // GRID_W) % NA_Q_ROWS == 0 and n_lat // GRID_W >= NA_K_ROWS
    nct = n_ctx // TOK_TILE

    rows = -(-(bsz + 1) // SUBLANES) * SUBLANES
    c_all = jnp.concatenate([c, c_ctx[None, :], jnp.zeros((rows - bsz - 1, d), F32)], axis=0)
    mods = _modulation(c_all, w_mod, b_mod).reshape(depth, rows, N_MOD, d)
    mod_ctx = jnp.broadcast_to(mods[:, bsz][:, None], (depth, bsz, N_MOD, d))
    modsel = jnp.stack([mod_ctx, mods[:, :bsz]], axis=2)

    rope_tabs = _rope_tables(n_ctx, n_lat)
    w1 = w_mlp1.astype(BF16)
    w2 = w_mlp2.astype(BF16)

    xs = (ctx, x)
    q, k, v = _pre_attn(xs, modsel[0], norm1_g[0], a_w_qkv[0].astype(BF16), a_q_norm_g[0],
                        a_k_norm_g[0], rope_tabs, k_width=d, dup_kv=False, nct=nct)
    lam_vecs = jnp.stack([a_lambda_q1[0], a_lambda_k1[0], a_lambda_q2[0], a_lambda_k2[0]])
    lam_init = 0.8 - 0.6 * math.exp(-0.3 * 0)
    o = _diff_attention(q, k, v, lam_vecs, a_subln_g[0], nct=nct, n_ctx=n_ctx, lam_init=lam_init)
    xs = (_post(o, xs, modsel[0], norm2_g[0], a_w_o[0].astype(BF16), w1[0], w2[0],
                nct=nct, skip_ctx=False),)

    q, k, v = _pre_attn(xs, modsel[1], norm1_g[1], b_w_qkv[0].astype(BF16), b_q_norm_g[0],
                        b_k_norm_g[0], None, k_width=d, dup_kv=False, nct=nct)
    bias_tab = _na_bias_table(b_rpb[0], n_lat // GRID_W)
    o = _na_attention(q, k, v, bias_tab, nct=nct, n_ctx=n_ctx)
    xs = (_post(o, xs, modsel[1], norm2_g[1], b_w_o[0].astype(BF16), w1[1], w2[1],
                nct=nct, skip_ctx=False),)

    gate, rec = _pre_lru(xs[0], modsel[2], norm1_g[2], c_w_in[0].astype(BF16), nct=nct)
    lru_p = (c_conv_w[0], c_conv_b[0], c_fwd_w_a[0], c_fwd_b_a[0], c_fwd_w_x[0], c_fwd_b_x[0],
             c_fwd_lam[0], c_bwd_w_a[0], c_bwd_b_a[0], c_bwd_w_x[0], c_bwd_b_x[0], c_bwd_lam[0])
    o = _lru_mixer(rec, gate, lru_p, n_ctx=n_ctx)
    xs = (_post(o, xs, modsel[2], norm2_g[2], c_w_o[0].astype(BF16), w1[2], w2[2],
                nct=nct, skip_ctx=False),)

    q, k, v = _pre_attn(xs, modsel[3], norm1_g[3], d_w_qkv[0].astype(BF16), d_q_norm_g[0],
                        d_k_norm_g[0], rope_tabs, k_width=GQA_KV_HEADS * HEAD_DIM, dup_kv=True,
                        nct=nct)
    o = _gqa_attention(q, k, v, nct=nct)
    return _post(o, xs, modsel[3], norm2_g[3], d_w_o[0].astype(BF16), w1[3], w2[3],
                 nct=nct, skip_ctx=True)
```

```python
import functools
import math

import jax
import jax.numpy as jnp
from jax import lax
from jax.experimental import pallas as pl
from jax.experimental.pallas import tpu as pltpu

F32 = jnp.float32
BF16 = jnp.bfloat16

D_MODEL = 1024
HEAD_DIM = 64
GRID_W = 64
N_MOD = 6
ATTN_SCALE = HEAD_DIM ** -0.5
LOG2E = math.log2(math.e)
ROPE_THETA = 10000.0
NORM_EPS = 1e-6
DA_HEADS = D_MODEL // (2 * HEAD_DIM)
NA_ROWS_MAX = 8
NA_COLS = 16
LRU_BLOCK = 256
LRU_CONV = 4
LRU_C = 8.0
GQA_KV_HEADS = 4
D_FF = 4 * D_MODEL

LANES = 128
SUBLANES = 8
MXU_DIM = 256
VMEM_LIMIT = 56 * 1024 * 1024

TOK_TILE = 256
BATCH_BLOCK = 4
N_PAIRS = D_MODEL // LANES
MASK_VALUE = -1e30
NA_Q_ROWS = TOK_TILE // GRID_W
NA_K_ROWS = 12
NA_BAND = NA_K_ROWS * GRID_W


def _params(n_axes):
    return pltpu.CompilerParams(
        dimension_semantics=("arbitrary",) * n_axes, vmem_limit_bytes=VMEM_LIMIT)


def _resident(shape):
    zeros = (0,) * len(shape)
    return pl.BlockSpec(shape, lambda *_: zeros, pipeline_mode=pl.Buffered(1))


def _dot(a, b):
    return jnp.dot(a, b, preferred_element_type=F32)


def _dot_nt(a, b):
    return lax.dot_general(a, b, (((1,), (1,)), ((), ())), preferred_element_type=F32)


def _norm_mod(x, gain, shift, scale):
    y = x * lax.rsqrt(jnp.mean(x * x, axis=-1, keepdims=True) + NORM_EPS) * gain
    return y * (1.0 + scale) + shift


def _group_mean_matrix():
    r = lax.broadcasted_iota(jnp.int32, (MXU_DIM, MXU_DIM), 0) // HEAD_DIM
    c = lax.broadcasted_iota(jnp.int32, (MXU_DIM, MXU_DIM), 1) // HEAD_DIM
    return jnp.where(r == c, 1.0 / HEAD_DIM, 0.0).astype(BF16)


def _head_rms(x, gain, mean_mat):
    sq = (x * x).astype(BF16)
    ms = jnp.concatenate(
        [_dot(sq[:, j:j + MXU_DIM], mean_mat) for j in range(0, x.shape[1], MXU_DIM)], axis=-1)
    return x * lax.rsqrt(ms + NORM_EPS) * gain


def _rope(x, cos, sin):
    width = x.shape[1]
    reps = width // LANES
    c = jnp.concatenate([cos] * reps, axis=-1)
    s = jnp.concatenate([sin] * reps, axis=-1)
    lane = lax.broadcasted_iota(jnp.int32, x.shape, 1)
    partner = jnp.where((lane & 1) == 0, pltpu.roll(x, width - 1, 1), pltpu.roll(x, 1, 1))
    return x * c + partner * s


def _mods_kernel(c_ref, w_ref, b_ref, o_ref):
    c = c_ref[...]
    a = (c * jax.nn.sigmoid(c)).astype(BF16)
    o_ref[0] = _dot(a, w_ref[0].astype(BF16)) + b_ref[0]


def _modulation(c_all, w_mod, b_mod):
    depth, d, n = w_mod.shape
    rows = c_all.shape[0]
    tn = n // 4
    return pl.pallas_call(
        _mods_kernel,
        grid=(depth, n // tn),
        in_specs=[pl.BlockSpec((rows, d), lambda l, j: (0, 0)),
                  pl.BlockSpec((1, d, tn), lambda l, j: (l, 0, j)),
                  pl.BlockSpec((1, 1, tn), lambda l, j: (l, 0, j))],
        out_specs=pl.BlockSpec((1, rows, tn), lambda l, j: (l, 0, j)),
        out_shape=jax.ShapeDtypeStruct((depth, rows, n), F32),
        compiler_params=_params(2),
        name="adaln_modulation",
    )(c_all, w_mod, b_mod.reshape(depth, 1, n))


def _stream_tile(refs, i, nct):
    if len(refs) == 1:
        return refs[0][i]
    return jnp.where(pl.program_id(1) < nct, refs[0][i], refs[1][i])


def _stream_specs(xs, nct):
    d = xs[0].shape[-1]
    if len(xs) == 1:
        return [pl.BlockSpec((BATCH_BLOCK, TOK_TILE, d), lambda b, t: (b, t, 0))]
    return [pl.BlockSpec((BATCH_BLOCK, TOK_TILE, d), lambda b, t: (b, jnp.minimum(t, nct - 1), 0)),
            pl.BlockSpec((BATCH_BLOCK, TOK_TILE, d), lambda b, t: (b, jnp.maximum(t - nct, 0), 0))]


def _pre_attn_kernel(*refs, **static):
    mean_mat = _group_mean_matrix()
    for i in range(BATCH_BLOCK):
        _pre_attn_sample(i, mean_mat, *refs, **static)


def _pre_attn_sample(i, mean_mat, *refs, n_streams, nct, k_width, rope, dup_kv):
    x_refs, refs = refs[:n_streams], refs[n_streams:]
    if rope:
        (mod_ref, ng_ref, w_ref, qg_ref, kg_ref, cos_ref, sin_ref, q_ref, k_ref, v_ref) = refs
    else:
        mod_ref, ng_ref, w_ref, qg_ref, kg_ref, q_ref, k_ref, v_ref = refs
    q_ref, k_ref, v_ref = q_ref.at[i], k_ref.at[i], v_ref.at[i]
    m = mod_ref[i, 0]
    h = _norm_mod(_stream_tile(x_refs, i, nct), ng_ref[...], m[0:1, :], m[1:2, :]).astype(BF16)
    d = D_MODEL
    q = _head_rms(_dot(h, w_ref[:, 0:d]), qg_ref[...], mean_mat)
    k = _head_rms(_dot(h, w_ref[:, d:d + k_width]), kg_ref[...], mean_mat)
    v = _dot(h, w_ref[:, d + k_width:d + 2 * k_width])
    if rope:
        q = _rope(q, cos_ref[...], sin_ref[...])
        k = _rope(k, cos_ref[...], sin_ref[...])
    for p in range(N_PAIRS):
        q_ref[p] = q[:, p * LANES:(p + 1) * LANES].astype(BF16)
    if dup_kv:
        for g in range(k_width // HEAD_DIM):
            kh = k[:, g * HEAD_DIM:(g + 1) * HEAD_DIM]
            k_ref[g] = jnp.concatenate([kh, kh], axis=-1).astype(BF16)
    else:
        for p in range(k_width // LANES):
            k_ref[p] = k[:, p * LANES:(p + 1) * LANES].astype(BF16)
    if dup_kv:
        for g in range(k_width // HEAD_DIM):
            vh = v[:, g * HEAD_DIM:(g + 1) * HEAD_DIM]
            v_ref[g] = jnp.concatenate([vh, vh], axis=-1).astype(BF16)
    else:
        for p in range(k_width // LANES):
            v_ref[p] = v[:, p * LANES:(p + 1) * LANES].astype(BF16)


def _mod_spec(nct, first=0):
    return pl.BlockSpec((BATCH_BLOCK, 1, N_MOD, D_MODEL),
                        lambda b, t: (b, jnp.where(t + first >= nct, 1, 0), 0, 0))


def _pre_attn(xs, modsel, norm_g, w_qkv, q_gain, k_gain, rope_tabs, *, k_width, dup_kv, nct):
    bsz, _, d = xs[0].shape
    n_tok = sum(a.shape[1] for a in xs)
    nt = n_tok // TOK_TILE
    n_w = w_qkv.shape[1]
    k_slots = k_width // HEAD_DIM if dup_kv else k_width // LANES
    rope = rope_tabs is not None
    in_specs = _stream_specs(xs, nct) + [
        _mod_spec(nct),
        _resident((1, d)),
        _resident((d, n_w)),
        _resident((1, d)),
        _resident((1, k_width))]
    args = list(xs) + [modsel, norm_g.reshape(1, d), w_qkv,
                       jnp.tile(q_gain * (ATTN_SCALE * LOG2E), d // HEAD_DIM).reshape(1, d),
                       jnp.tile(k_gain, k_width // HEAD_DIM).reshape(1, k_width)]
    if rope:
        in_specs += [pl.BlockSpec((TOK_TILE, LANES), lambda b, t: (t, 0))] * 2
        args += list(rope_tabs)
    qo = jax.ShapeDtypeStruct((bsz, N_PAIRS, n_tok, LANES), BF16)
    ko = jax.ShapeDtypeStruct((bsz, k_slots, n_tok, LANES), BF16)
    vo = jax.ShapeDtypeStruct((bsz, k_slots, n_tok, LANES), BF16)
    return pl.pallas_call(
        functools.partial(_pre_attn_kernel, n_streams=len(xs), nct=nct, k_width=k_width,
                          rope=rope, dup_kv=dup_kv),
        grid=(bsz // BATCH_BLOCK, nt),
        in_specs=in_specs,
        out_specs=[
            pl.BlockSpec((BATCH_BLOCK, N_PAIRS, TOK_TILE, LANES), lambda b, t: (b, 0, t, 0)),
            pl.BlockSpec((BATCH_BLOCK, k_slots, TOK_TILE, LANES), lambda b, t: (b, 0, t, 0)),
            pl.BlockSpec((BATCH_BLOCK, k_slots, TOK_TILE, LANES), lambda b, t: (b, 0, t, 0))],
        out_shape=[qo, ko, vo],
        compiler_params=_params(2),
        name="norm_mod_qkv",
    )(*args)


def _pre_lru_kernel(x_ref, mod_ref, ng_ref, w_ref, gate_ref, rec_ref):
    d = D_MODEL
    for i in range(BATCH_BLOCK):
        m = mod_ref[i, 0]
        h = _norm_mod(x_ref[i], ng_ref[...], m[0:1, :], m[1:2, :]).astype(BF16)
        gate_ref[i] = jax.nn.gelu(_dot(h, w_ref[:, 0:d])).astype(BF16)
        rec_ref[i] = _dot(h, w_ref[:, d:2 * d])


def _pre_lru(xs, modsel, norm_g, w_in, *, nct):
    bsz, n_tok, d = xs.shape
    nt = n_tok // TOK_TILE
    tok_spec = pl.BlockSpec((BATCH_BLOCK, TOK_TILE, d), lambda b, t: (b, t, 0))
    return pl.pallas_call(
        _pre_lru_kernel,
        grid=(bsz // BATCH_BLOCK, nt),
        in_specs=[tok_spec, _mod_spec(nct), _resident((1, d)), _resident((d, 2 * d))],
        out_specs=[tok_spec, tok_spec],
        out_shape=[jax.ShapeDtypeStruct((bsz, n_tok, d), BF16),
                   jax.ShapeDtypeStruct((bsz, n_tok, d), F32)],
        compiler_params=_params(2),
        name="norm_mod_lru_in",
    )(xs, modsel, norm_g.reshape(1, d), w_in)


def _split_pair(q2):
    low = lax.broadcasted_iota(jnp.int32, q2.shape, 1) < HEAD_DIM
    zero = jnp.zeros_like(q2)
    return jnp.where(low, q2, zero), jnp.where(low, zero, q2)


def _exp2_parts(parts):
    m = functools.reduce(jnp.maximum, [jnp.max(s, axis=-1, keepdims=True) for s in parts])
    return [jnp.exp2(s - m) for s in parts]


def _pipeline3(n_units, scores, softmax, values):
    assert n_units % 2 == 0 and n_units >= 4
    scores(0, 0)
    scores(1, 1)
    softmax(0, 0)
    for u in range(2, n_units, 2):
        scores(u, 0)
        softmax(u - 1, 1)
        values(u - 2, 0)
        scores(u + 1, 1)
        softmax(u, 0)
        values(u - 1, 1)
    softmax(n_units - 1, 1)
    values(n_units - 2, 0)
    values(n_units - 1, 1)


def _sub_layer_norm(o, gain, lam_init):
    o = o * lax.rsqrt(jnp.mean(o * o, axis=-1, keepdims=True) + NORM_EPS) * gain
    return (o * (1.0 - lam_init)).astype(BF16)


def _scores_stage(s_buf, m_buf, q2, key_parts, bias_parts):
    for half, qh in enumerate(_split_pair(q2)):
        m, col = None, 0
        for kp, bp in zip(key_parts, bias_parts):
            s = _dot_nt(qh, kp)
            if bp is not None:
                s = s + bp[half].astype(F32)
            s_buf[half, :, col:col + kp.shape[0]] = s
            part_max = jnp.max(s, axis=-1, keepdims=True)
            m = part_max if m is None else jnp.maximum(m, part_max)
            col += kp.shape[0]
        m_buf[half] = jnp.broadcast_to(m, m_buf.shape[1:])


def _softmax_stage(s_buf, m_buf, p_buf):
    for half in range(2):
        p_buf[half] = jnp.exp2(s_buf[half] - m_buf[half][:, 0:1]).astype(BF16)


def _with_ones(v):
    return jnp.concatenate([v, jnp.ones_like(v)], axis=-1)


def _values_stage(p_buf, value_parts):
    outs = []
    for half in range(2):
        acc, col = None, 0
        for vp in value_parts:
            part = _dot(p_buf[half, :, col:col + vp.shape[0]], _with_ones(vp))
            acc = part if acc is None else acc + part
            col += vp.shape[0]
        outs.append(acc[:, :LANES] / acc[:, LANES:])
    return outs


def _direct_pair(q2, k2, v2):
    outs = []
    for qh in _split_pair(q2):
        (e,) = _exp2_parts([_dot_nt(qh, k2)])
        acc = _dot(e.astype(BF16), _with_ones(v2))
        outs.append(acc[:, :LANES] / acc[:, LANES:])
    return outs


def _attn_scratch(n_cols):
    scratch = [pltpu.VMEM((2, TOK_TILE, n_cols), F32)] * 2
    scratch += [pltpu.VMEM((2, TOK_TILE, LANES), F32)] * 2
    scratch += [pltpu.VMEM((2, TOK_TILE, n_cols), BF16)] * 2
    return scratch


def _diff_attn_kernel(lam_ref, sg_ref, q_ref, k_ref, v_ref, o_ref, s0, s1, m0, m1, p0, p1,
                      *, nct, n_ctx, lam_init):
    t = pl.program_id(1)
    lv = lam_ref[...]
    lam = (jnp.exp(jnp.sum(lv[0:1, :] * lv[1:2, :], axis=-1, keepdims=True))
           - jnp.exp(jnp.sum(lv[2:3, :] * lv[3:4, :], axis=-1, keepdims=True)) + lam_init)

    def finish(h, outs):
        o_ref[0, h] = _sub_layer_norm(outs[0] - lam * outs[1], sg_ref[...], lam_init)

    @pl.when(t < nct)
    def _():
        for h in range(DA_HEADS):
            finish(h, _direct_pair(q_ref[0, h], k_ref[0, h, 0:n_ctx, :], v_ref[0, h, 0:n_ctx, :]))

    @pl.when(t >= nct)
    def _():
        s_bufs, m_bufs, p_bufs = (s0, s1), (m0, m1), (p0, p1)
        _pipeline3(
            DA_HEADS,
            lambda h, slot: _scores_stage(s_bufs[slot], m_bufs[slot], q_ref[0, h],
                                          [k_ref[0, h]], [None]),
            lambda h, slot: _softmax_stage(s_bufs[slot], m_bufs[slot], p_bufs[slot]),
            lambda h, slot: finish(h, _values_stage(p_bufs[slot], [v_ref[0, h]])))


def _diff_attention(q, k, v, lam_vecs, subln_g, *, nct, n_ctx, lam_init):
    bsz, _, n_tok, _ = q.shape
    nt = n_tok // TOK_TILE
    tile = pl.BlockSpec((1, N_PAIRS, TOK_TILE, LANES), lambda b, t: (b, 0, t, 0))
    whole_k = pl.BlockSpec((1, DA_HEADS, n_tok, LANES), lambda b, t: (b, 0, 0, 0))
    whole_v = pl.BlockSpec((1, DA_HEADS, n_tok, LANES), lambda b, t: (b, 0, 0, 0))
    scratch = _attn_scratch(n_tok)
    return pl.pallas_call(
        functools.partial(_diff_attn_kernel, nct=nct, n_ctx=n_ctx, lam_init=lam_init),
        grid=(bsz, nt),
        in_specs=[_resident((4, HEAD_DIM)), _resident((1, LANES)), tile, whole_k, whole_v],
        out_specs=tile,
        out_shape=jax.ShapeDtypeStruct(q.shape, BF16),
        scratch_shapes=scratch,
        compiler_params=_params(2),
        name="diff_attention",
    )(lam_vecs, subln_g.reshape(1, LANES), q, k, v)


def _pair_output(outs):
    low = lax.broadcasted_iota(jnp.int32, outs[0].shape, 1) < HEAD_DIM
    return jnp.where(low, outs[0], outs[1]).astype(BF16)


def _gqa_attn_kernel(q_ref, k_ref, v_ref, o_ref, s0, s1, m0, m1, p0, p1):
    s_bufs, m_bufs, p_bufs = (s0, s1), (m0, m1), (p0, p1)

    def scores(u, slot):
        i, p = divmod(u, N_PAIRS)
        _scores_stage(s_bufs[slot], m_bufs[slot], q_ref[i, p], [k_ref[i, p // 2]], [None])

    def values(u, slot):
        i, p = divmod(u, N_PAIRS)
        o_ref[i, p] = _pair_output(_values_stage(p_bufs[slot], [v_ref[i, p // 2]]))

    _pipeline3(
        GQA_BATCH_BLOCK * N_PAIRS, scores,
        lambda u, slot: _softmax_stage(s_bufs[slot], m_bufs[slot], p_bufs[slot]),
        values)


GQA_BATCH_BLOCK = 2


def _gqa_attention(q, k, v, *, nct):
    bsz, _, n_tok, _ = q.shape
    nt = n_tok // TOK_TILE - nct
    bb = GQA_BATCH_BLOCK
    q_tile = pl.BlockSpec((bb, N_PAIRS, TOK_TILE, LANES), lambda b, t: (b, 0, t + nct, 0))
    o_tile = pl.BlockSpec((bb, N_PAIRS, TOK_TILE, LANES), lambda b, t: (b, 0, t, 0))
    whole_k = pl.BlockSpec((bb, GQA_KV_HEADS, n_tok, LANES), lambda b, t: (b, 0, 0, 0))
    whole_v = pl.BlockSpec((bb, GQA_KV_HEADS, n_tok, LANES), lambda b, t: (b, 0, 0, 0))
    scratch = _attn_scratch(n_tok)
    return pl.pallas_call(
        _gqa_attn_kernel,
        grid=(bsz // bb, nt),
        in_specs=[q_tile, whole_k, whole_v],
        out_specs=o_tile,
        out_shape=jax.ShapeDtypeStruct((bsz, N_PAIRS, nt * TOK_TILE, LANES), BF16),
        scratch_shapes=scratch,
        compiler_params=_params(2),
        name="gqa_attention",
    )(q, k, v)


def _na_attn_kernel(bias_ref, q_ref, k_ref, v_ref, o_ref, s0, s1, m0, m1, p0, p1,
                    *, nct, n_ctx, n_rows):
    t = pl.program_id(1)

    @pl.when(t < nct)
    def _():
        for p in range(N_PAIRS):
            o_ref[0, p] = _pair_output(_direct_pair(
                q_ref[0, p], k_ref[0, p, 0:n_ctx, :], v_ref[0, p, 0:n_ctx, :]))

    @pl.when(t >= nct)
    def _():
        first_row = jnp.clip(NA_Q_ROWS * (t - nct) - NA_ROWS_MAX // 2, 0, n_rows - NA_K_ROWS)
        start = pl.multiple_of(n_ctx + first_row * GRID_W, GRID_W)
        s_bufs, m_bufs, p_bufs = (s0, s1), (m0, m1), (p0, p1)

        def scores(p, slot):
            keys = [k_ref[0, p, pl.ds(0, n_ctx), :], k_ref[0, p, pl.ds(start, NA_BAND), :]]
            bias = (bias_ref[0, 2 * p], bias_ref[0, 2 * p + 1])
            _scores_stage(s_bufs[slot], m_bufs[slot], q_ref[0, p], keys, [None, bias])

        def values(p, slot):
            vals = [v_ref[0, p, pl.ds(0, n_ctx), :], v_ref[0, p, pl.ds(start, NA_BAND), :]]
            o_ref[0, p] = _pair_output(_values_stage(p_bufs[slot], vals))

        _pipeline3(
            N_PAIRS, scores,
            lambda p, slot: _softmax_stage(s_bufs[slot], m_bufs[slot], p_bufs[slot]),
            values)


def _na_bias_table(rpb, n_rows):
    heads = rpb.shape[0]
    kr_rows = min(NA_ROWS_MAX, n_rows)
    variants = (0, 1, n_rows // NA_Q_ROWS - 1)
    n_tiles = len(variants)
    tt = jnp.array(variants)[:, None, None]
    qr = jnp.arange(NA_Q_ROWS)[None, :, None]
    kr = jnp.arange(NA_K_ROWS)[None, None, :]
    r = NA_Q_ROWS * tt + qr
    rp = jnp.clip(NA_Q_ROWS * tt - NA_ROWS_MAX // 2, 0, n_rows - NA_K_ROWS) + kr
    r0 = jnp.clip(r - kr_rows // 2, 0, n_rows - kr_rows)
    row_ok = (rp >= r0) & (rp < r0 + kr_rows)
    row_idx = jnp.clip(rp - r + NA_ROWS_MAX - 1, 0, 2 * NA_ROWS_MAX - 2)
    c = jnp.arange(GRID_W)[:, None]
    cp = jnp.arange(GRID_W)[None, :]
    cs = jnp.clip(c - NA_COLS // 2, 0, GRID_W - NA_COLS)
    col_ok = (cp >= cs) & (cp < cs + NA_COLS)
    col_idx = jnp.clip(cp - c, -(NA_COLS - 1), NA_COLS - 1) + NA_COLS - 1
    rows_sel = rpb[:, row_idx, :]
    onehot = (col_idx.reshape(-1)[None, :] == jnp.arange(2 * NA_COLS - 1)[:, None]).astype(F32)
    tab = jnp.dot(rows_sel.reshape(-1, 2 * NA_COLS - 1), onehot, precision=lax.Precision.HIGHEST)
    tab = tab.reshape(heads, n_tiles, NA_Q_ROWS, NA_K_ROWS, GRID_W, GRID_W)
    ok = row_ok[None, :, :, :, None, None] & col_ok[None, None, None, None, :, :]
    tab = jnp.where(ok, tab * LOG2E, MASK_VALUE)
    tab = tab.transpose(1, 0, 2, 4, 3, 5)
    return tab.reshape(n_tiles, heads, TOK_TILE, NA_BAND).astype(BF16)


def _na_attention(q, k, v, bias_tab, *, nct, n_ctx):
    bsz, _, n_tok, _ = q.shape
    nt = n_tok // TOK_TILE
    heads = bias_tab.shape[1]
    n_rows = (n_tok - n_ctx) // GRID_W
    tile = pl.BlockSpec((1, N_PAIRS, TOK_TILE, LANES), lambda b, t: (b, 0, t, 0))
    whole_k = pl.BlockSpec((1, N_PAIRS, n_tok, LANES), lambda b, t: (b, 0, 0, 0))
    whole_v = pl.BlockSpec((1, N_PAIRS, n_tok, LANES), lambda b, t: (b, 0, 0, 0))
    last = nt - nct - 1

    def bias_index(b, t):
        tile = t - nct
        return (jnp.where(tile <= 0, 0, jnp.where(tile == last, 2, 1)), 0, 0, 0)

    bias_spec = pl.BlockSpec((1, heads, TOK_TILE, NA_BAND), bias_index)
    scratch = _attn_scratch(n_ctx + NA_BAND)
    return pl.pallas_call(
        functools.partial(_na_attn_kernel, nct=nct, n_ctx=n_ctx, n_rows=n_rows),
        grid=(bsz, nt),
        in_specs=[bias_spec, tile, whole_k, whole_v],
        out_specs=tile,
        out_shape=jax.ShapeDtypeStruct(q.shape, BF16),
        scratch_shapes=scratch,
        compiler_params=_params(2),
        name="neighbourhood_attention",
    )(bias_tab, q, k, v)


LRU_CHUNK = 256
LRU_PAD = 8


def _softplus(z):
    return jnp.maximum(z, 0.0) + jnp.log1p(jnp.exp(-jnp.abs(z)))


def _sqrt_nonneg(z):
    return z * lax.rsqrt(jnp.maximum(z, float(jnp.finfo(F32).tiny)))


def _block_scan(a_ref, b_ref, slab, r0, reverse):
    n_blk = LRU_CHUNK // SUBLANES
    order = range(SUBLANES - 1, -1, -1) if reverse else range(SUBLANES)
    acc_a = acc_b = None
    for r in order:
        rows = pl.ds(r0 + r, n_blk, stride=SUBLANES)
        a_r = a_ref[slab, rows, :]
        b_r = b_ref[slab, rows, :]
        if acc_a is None:
            acc_a, acc_b = a_r, b_r
            continue
        acc_b = a_r * acc_b + b_r
        acc_a = a_r * acc_a
        a_ref[slab, rows, :] = acc_a
        b_ref[slab, rows, :] = acc_b


def _lru_kernel(rec_ref, gate_ref, cw_ref, cb_ref,
                fwa_ref, fwx_ref, bwa_ref, bwx_ref,
                fba_ref, fbx_ref, bba_ref, bbx_ref, flam_ref, blam_ref,
                o_ref, xp_ref, af_ref, bf_ref, ab_ref, bb_ref, *, n_ctx, n_tok):
    width = LRU_BLOCK
    xp_ref[0:LRU_PAD, :] = jnp.zeros((LRU_PAD, width), F32)
    xp_ref[LRU_PAD + n_tok:2 * LRU_PAD + n_tok, :] = jnp.zeros((LRU_PAD, width), F32)
    xp_ref[LRU_PAD:LRU_PAD + n_tok, :] = rec_ref[0]

    directions = (
        (fwa_ref, fwx_ref, fba_ref, fbx_ref, (-0.5 * LRU_C) * _softplus(-flam_ref[...]),
         af_ref, bf_ref, False),
        (bwa_ref, bwx_ref, bba_ref, bbx_ref, (-0.5 * LRU_C) * _softplus(-blam_ref[...]),
         ab_ref, bb_ref, True),
    )
    n_slabs = width // LANES
    wrow = lax.broadcasted_iota(jnp.int32, (LRU_CHUNK + 2 * LRU_PAD, 1), 0)

    for r0 in range(0, n_tok, LRU_CHUNK):
        xw = xp_ref[r0:r0 + LRU_CHUNK + 2 * LRU_PAD, :]
        if r0 == n_ctx:
            xw = jnp.where(wrow < LRU_PAD, 0.0, xw)
        if r0 + LRU_CHUNK == n_ctx:
            xw = jnp.where(wrow >= LRU_PAD + LRU_CHUNK, 0.0, xw)
        u = cb_ref[...]
        for j in range(LRU_CONV):
            off = LRU_PAD + j - LRU_CONV // 2
            u = u + xw[off:off + LRU_CHUNK, :] * cw_ref[j:j + 1, :]
        ub = u.astype(BF16)
        for wa_ref, wx_ref, ba_ref, bx_ref, decay, a_ref, b_ref, reverse in directions:
            log_a = decay * jnp.tanh(_dot(ub, wa_ref[0]) + ba_ref[...]) + decay
            i = 0.5 * jnp.tanh(_dot(ub, wx_ref[0]) + bx_ref[...]) + 0.5
            a = jnp.exp(log_a)
            b = _sqrt_nonneg(-jnp.tanh(log_a) * (a * a + 1.0)) * (i * u)
            for s in range(n_slabs):
                a_ref[s, r0:r0 + LRU_CHUNK, :] = a[:, s * LANES:(s + 1) * LANES]
                b_ref[s, r0:r0 + LRU_CHUNK, :] = b[:, s * LANES:(s + 1) * LANES]
                _block_scan(a_ref, b_ref, s, r0, reverse)

    n_all = n_tok // SUBLANES
    n_cb = n_ctx // SUBLANES

    def step(i, carry, rev_end):
        cf, cr = carry
        rf = pl.multiple_of(i * SUBLANES, SUBLANES)
        rb = pl.multiple_of((rev_end - i) * SUBLANES, SUBLANES)
        a_f, b_f = af_ref[:, pl.ds(rf, SUBLANES), :], bf_ref[:, pl.ds(rf, SUBLANES), :]
        a_r, b_r = ab_ref[:, pl.ds(rb, SUBLANES), :], bb_ref[:, pl.ds(rb, SUBLANES), :]
        bf_ref[:, pl.ds(rf, SUBLANES), :] = a_f * cf + b_f
        bb_ref[:, pl.ds(rb, SUBLANES), :] = a_r * cr + b_r

        def edge(x, row):
            return jnp.broadcast_to(x[:, row:row + 1, :], x.shape)
        last = SUBLANES - 1
        return (edge(a_f, last) * cf + edge(b_f, last), edge(a_r, 0) * cr + edge(b_r, 0))

    zero = jnp.zeros((n_slabs, SUBLANES, LANES), F32)
    carry = lax.fori_loop(0, n_cb, functools.partial(step, rev_end=n_cb - 1), (zero, zero),
                          unroll=2)
    lax.fori_loop(n_cb, n_all, functools.partial(step, rev_end=n_all - 1 + n_cb), carry, unroll=2)

    gate = gate_ref[0]
    for s in range(n_slabs):
        y = (bf_ref[s] + bb_ref[s]) * gate[:, s * LANES:(s + 1) * LANES].astype(F32)
        o_ref[0, s] = y.astype(BF16)


def _lru_mixer(rec, gate, p, *, n_ctx):
    (conv_w, conv_b, f_wa, f_ba, f_wx, f_bx, f_lam, b_wa, b_ba, b_wx, b_bx, b_lam) = p
    bsz, n_tok, d = rec.shape
    n_blocks = d // LRU_BLOCK
    seq = pl.BlockSpec((1, n_tok, LRU_BLOCK), lambda b, n: (b, 0, n))
    vec = pl.BlockSpec((1, LRU_BLOCK), lambda b, n: (0, n))
    mat = pl.BlockSpec((1, LRU_BLOCK, LRU_BLOCK), lambda b, n: (n, 0, 0))
    row = lambda a: a.reshape(1, d)
    half = lambda a: 0.5 * a
    assert n_ctx % LRU_CHUNK == 0 and n_tok % LRU_CHUNK == 0
    scratch = [pltpu.VMEM((n_tok + 2 * LRU_PAD, LRU_BLOCK), F32)]
    scratch += [pltpu.VMEM((LRU_BLOCK // LANES, n_tok, LANES), F32)] * 4
    return pl.pallas_call(
        functools.partial(_lru_kernel, n_ctx=n_ctx, n_tok=n_tok),
        grid=(bsz, n_blocks),
        in_specs=[seq, seq, pl.BlockSpec((LRU_CONV, LRU_BLOCK), lambda b, n: (0, n)), vec,
                  mat, mat, mat, mat, vec, vec, vec, vec, vec, vec],
        out_specs=pl.BlockSpec((1, LRU_BLOCK // LANES, n_tok, LANES), lambda b, n: (b, n, 0, 0)),
        out_shape=jax.ShapeDtypeStruct((bsz, d // LANES, n_tok, LANES), BF16),
        scratch_shapes=scratch,
        compiler_params=_params(2),
        name="rglru_scan",
    )(rec, gate, conv_w, row(conv_b),
      half(f_wa).astype(BF16), half(f_wx).astype(BF16),
      half(b_wa).astype(BF16), half(b_wx).astype(BF16),
      row(half(f_ba)), row(half(f_bx)), row(half(b_ba)), row(half(b_bx)), row(f_lam), row(b_lam))


FF_CHUNK = 1024


def _post_kernel(o_ref, *refs, n_streams, nct):
    x_refs = refs[:n_streams]
    mod_ref, ng_ref, wo_ref, w1_ref, w2_ref, out_ref = refs[n_streams:]
    for i in range(BATCH_BLOCK):
        m = mod_ref[i, 0]
        o = jnp.concatenate([o_ref[i, p] for p in range(N_PAIRS)], axis=-1)
        x1 = _stream_tile(x_refs, i, nct) + m[2:3, :] * _dot(o, wo_ref[...])
        h = _norm_mod(x1, ng_ref[...], m[3:4, :], m[4:5, :]).astype(BF16)
        acc = jnp.zeros_like(x1)
        for c in range(0, D_FF, FF_CHUNK):
            a = jnp.maximum(_dot(h, w1_ref[:, c:c + FF_CHUNK]), 0.0)
            acc = acc + _dot((a * a).astype(BF16), w2_ref[c:c + FF_CHUNK, :])
        out_ref[i] = x1 + m[5:6, :] * acc


def _post(o, xs, modsel, norm_g, w_o, w1, w2, *, nct, skip_ctx):
    bsz, _, d = xs[0].shape
    n_tok = sum(a.shape[1] for a in xs)
    first = nct if skip_ctx else 0
    nt = n_tok // TOK_TILE - first
    assert o.shape[2] == nt * TOK_TILE
    if len(xs) == 1:
        x_specs = [pl.BlockSpec((BATCH_BLOCK, TOK_TILE, d), lambda b, t: (b, t + first, 0))]
    else:
        assert not skip_ctx
        x_specs = _stream_specs(xs, nct)
    return pl.pallas_call(
        functools.partial(_post_kernel, n_streams=len(xs), nct=nct),
        grid=(bsz // BATCH_BLOCK, nt),
        in_specs=[pl.BlockSpec((BATCH_BLOCK, N_PAIRS, TOK_TILE, LANES), lambda b, t: (b, 0, t, 0))]
        + x_specs
        + [_mod_spec(nct, first),
           _resident((1, d)), _resident((d, d)), _resident((d, D_FF)), _resident((D_FF, d))],
        out_specs=pl.BlockSpec((BATCH_BLOCK, TOK_TILE, d), lambda b, t: (b, t, 0)),
        out_shape=jax.ShapeDtypeStruct((bsz, nt * TOK_TILE, d), F32),
        compiler_params=_params(2),
        name="proj_residual_mlp",
    )(o, *xs, modsel, norm_g.reshape(1, d), w_o, w1, w2)


def _rope_tables(n_ctx, n_lat):
    t = jnp.arange(n_lat)
    row = (t // GRID_W).astype(F32)
    col = (t % GRID_W).astype(F32)
    n_freq = HEAD_DIM // 4
    inv = ROPE_THETA ** (-jnp.arange(n_freq, dtype=F32) / n_freq)
    ang = jnp.concatenate([row[:, None] * inv, col[:, None] * inv], axis=-1)
    cos = jnp.repeat(jnp.cos(ang), 2, axis=-1)
    sin = jnp.repeat(jnp.sin(ang), 2, axis=-1) * jnp.tile(jnp.array([-1.0, 1.0], F32), HEAD_DIM // 2)
    cos = jnp.concatenate([jnp.ones((n_ctx, HEAD_DIM), F32), cos], axis=0)
    sin = jnp.concatenate([jnp.zeros((n_ctx, HEAD_DIM), F32), sin], axis=0)
    return jnp.tile(cos, (1, LANES // HEAD_DIM)), jnp.tile(sin, (1, LANES // HEAD_DIM))


def kernel(x, c, ctx, c_ctx, norm1_g, norm2_g, w_mod, b_mod, w_mlp1, w_mlp2, a_w_qkv, a_q_norm_g, a_k_norm_g, a_lambda_q1, a_lambda_k1, a_lambda_q2, a_lambda_k2, a_subln_g, a_w_o, b_w_qkv, b_q_norm_g, b_k_norm_g, b_rpb, b_w_o, c_w_in, c_conv_w, c_conv_b, c_fwd_w_a, c_fwd_b_a, c_fwd_w_x, c_fwd_b_x, c_fwd_lam, c_bwd_w_a, c_bwd_b_a, c_bwd_w_x, c_bwd_b_x, c_bwd_lam, c_w_o, d_w_qkv, d_q_norm_g, d_k_norm_g, d_w_o):
    bsz, n_lat, d = x.shape
    n_ctx = ctx.shape[1]
    depth = w_mod.shape[0]
    assert d == D_MODEL and depth == 4 and n_ctx % TOK_TILE == 0 and n_lat % TOK_TILE == 0
    assert bsz % BATCH_BLOCK == 0
    assert (n_lat // GRID_W) % NA_Q_ROWS == 0 and n_lat // GRID_W >= NA_K_ROWS
    nct = n_ctx // TOK_TILE

    rows = -(-(bsz + 1) // SUBLANES) * SUBLANES
    c_all = jnp.concatenate([c, c_ctx[None, :], jnp.zeros((rows - bsz - 1, d), F32)], axis=0)
    mods = _modulation(c_all, w_mod, b_mod).reshape(depth, rows, N_MOD, d)
    mod_ctx = jnp.broadcast_to(mods[:, bsz][:, None], (depth, bsz, N_MOD, d))
    modsel = jnp.stack([mod_ctx, mods[:, :bsz]], axis=2)

    rope_tabs = _rope_tables(n_ctx, n_lat)
    w1 = w_mlp1.astype(BF16)
    w2 = w_mlp2.astype(BF16)

    xs = (ctx, x)
    q, k, v = _pre_attn(xs, modsel[0], norm1_g[0], a_w_qkv[0].astype(BF16), a_q_norm_g[0],
                        a_k_norm_g[0], rope_tabs, k_width=d, dup_kv=False, nct=nct)
    lam_vecs = jnp.stack([a_lambda_q1[0], a_lambda_k1[0], a_lambda_q2[0], a_lambda_k2[0]])
    lam_init = 0.8 - 0.6 * math.exp(-0.3 * 0)
    o = _diff_attention(q, k, v, lam_vecs, a_subln_g[0], nct=nct, n_ctx=n_ctx, lam_init=lam_init)
    xs = (_post(o, xs, modsel[0], norm2_g[0], a_w_o[0].astype(BF16), w1[0], w2[0],
                nct=nct, skip_ctx=False),)

    q, k, v = _pre_attn(xs, modsel[1], norm1_g[1], b_w_qkv[0].astype(BF16), b_q_norm_g[0],
                        b_k_norm_g[0], None, k_width=d, dup_kv=False, nct=nct)
    bias_tab = _na_bias_table(b_rpb[0], n_lat // GRID_W)
    o = _na_attention(q, k, v, bias_tab, nct=nct, n_ctx=n_ctx)
    xs = (_post(o, xs, modsel[1], norm2_g[1], b_w_o[0].astype(BF16), w1[1], w2[1],
                nct=nct, skip_ctx=False),)

    gate, rec = _pre_lru(xs[0], modsel[2], norm1_g[2], c_w_in[0].astype(BF16), nct=nct)
    lru_p = (c_conv_w[0], c_conv_b[0], c_fwd_w_a[0], c_fwd_b_a[0], c_fwd_w_x[0], c_fwd_b_x[0],
             c_fwd_lam[0], c_bwd_w_a[0], c_bwd_b_a[0], c_bwd_w_x[0], c_bwd_b_x[0], c_bwd_lam[0])
    o = _lru_mixer(rec, gate, lru_p, n_ctx=n_ctx)
    xs = (_post(o, xs, modsel[2], norm2_g[2], c_w_o[0].astype(BF16), w1[2], w2[2],
                nct=nct, skip_ctx=False),)

    q, k, v = _pre_attn(xs, modsel[3], norm1_g[3], d_w_qkv[0].astype(BF16), d_q_norm_g[0],
                        d_k_norm_g[0], rope_tabs, k_width=GQA_KV_HEADS * HEAD_DIM, dup_kv=True,
                        nct=nct)
    o = _gqa_attention(q, k, v, nct=nct)
    return _post(o, xs, modsel[3], norm2_g[3], d_w_o[0].astype(BF16), w1[3], w2[3],
                 nct=nct, skip_ctx=True)
```

```python
import functools
import math

import jax
import jax.numpy as jnp
from jax import lax
from jax.experimental import pallas as pl
from jax.experimental.pallas import tpu as pltpu

F32 = jnp.float32
BF16 = jnp.bfloat16

D_MODEL = 1024
HEAD_DIM = 64
GRID_W = 64
N_MOD = 6
ATTN_SCALE = HEAD_DIM ** -0.5
LOG2E = math.log2(math.e)
ROPE_THETA = 10000.0
NORM_EPS = 1e-6
DA_HEADS = D_MODEL // (2 * HEAD_DIM)
NA_ROWS_MAX = 8
NA_COLS = 16
LRU_BLOCK = 256
LRU_CONV = 4
LRU_C = 8.0
GQA_KV_HEADS = 4
D_FF = 4 * D_MODEL

LANES = 128
SUBLANES = 8
MXU_DIM = 256
VMEM_LIMIT = 56 * 1024 * 1024

TOK_TILE = 256
BATCH_BLOCK = 4
N_PAIRS = D_MODEL // LANES
MASK_VALUE = -1e30
NA_Q_ROWS = TOK_TILE // GRID_W
NA_K_ROWS = 12
NA_BAND = NA_K_ROWS * GRID_W


def _params(n_axes):
    return pltpu.CompilerParams(
        dimension_semantics=("arbitrary",) * n_axes, vmem_limit_bytes=VMEM_LIMIT)


def _resident(shape):
    zeros = (0,) * len(shape)
    return pl.BlockSpec(shape, lambda *_: zeros, pipeline_mode=pl.Buffered(1))


def _dot(a, b):
    return jnp.dot(a, b, preferred_element_type=F32)


def _dot_nt(a, b):
    return lax.dot_general(a, b, (((1,), (1,)), ((), ())), preferred_element_type=F32)


def _norm_mod(x, gain, shift, scale):
    y = x * lax.rsqrt(jnp.mean(x * x, axis=-1, keepdims=True) + NORM_EPS) * gain
    return y * (1.0 + scale) + shift


def _group_mean_matrix():
    r = lax.broadcasted_iota(jnp.int32, (MXU_DIM, MXU_DIM), 0) // HEAD_DIM
    c = lax.broadcasted_iota(jnp.int32, (MXU_DIM, MXU_DIM), 1) // HEAD_DIM
    return jnp.where(r == c, 1.0 / HEAD_DIM, 0.0).astype(BF16)


def _head_rms(x, gain, mean_mat):
    sq = (x * x).astype(BF16)
    ms = jnp.concatenate(
        [_dot(sq[:, j:j + MXU_DIM], mean_mat) for j in range(0, x.shape[1], MXU_DIM)], axis=-1)
    return x * lax.rsqrt(ms + NORM_EPS) * gain


def _rope(x, cos, sin):
    width = x.shape[1]
    reps = width // LANES
    c = jnp.concatenate([cos] * reps, axis=-1)
    s = jnp.concatenate([sin] * reps, axis=-1)
    lane = lax.broadcasted_iota(jnp.int32, x.shape, 1)
    partner = jnp.where((lane & 1) == 0, pltpu.roll(x, width - 1, 1), pltpu.roll(x, 1, 1))
    return x * c + partner * s


def _mods_kernel(c_ref, w_ref, b_ref, o_ref):
    c = c_ref[...]
    a = (c * jax.nn.sigmoid(c)).astype(BF16)
    o_ref[0] = _dot(a, w_ref[0].astype(BF16)) + b_ref[0]


def _modulation(c_all, w_mod, b_mod):
    depth, d, n = w_mod.shape
    rows = c_all.shape[0]
    tn = n // 4
    return pl.pallas_call(
        _mods_kernel,
        grid=(depth, n // tn),
        in_specs=[pl.BlockSpec((rows, d), lambda l, j: (0, 0)),
                  pl.BlockSpec((1, d, tn), lambda l, j: (l, 0, j)),
                  pl.BlockSpec((1, 1, tn), lambda l, j: (l, 0, j))],
        out_specs=pl.BlockSpec((1, rows, tn), lambda l, j: (l, 0, j)),
        out_shape=jax.ShapeDtypeStruct((depth, rows, n), F32),
        compiler_params=_params(2),
        name="adaln_modulation",
    )(c_all, w_mod, b_mod.reshape(depth, 1, n))


def _stream_tile(refs, i, nct):
    if len(refs) == 1:
        return refs[0][i]
    return jnp.where(pl.program_id(1) < nct, refs[0][i], refs[1][i])


def _stream_specs(xs, nct):
    d = xs[0].shape[-1]
    if len(xs) == 1:
        return [pl.BlockSpec((BATCH_BLOCK, TOK_TILE, d), lambda b, t: (b, t, 0))]
    return [pl.BlockSpec((BATCH_BLOCK, TOK_TILE, d), lambda b, t: (b, jnp.minimum(t, nct - 1), 0)),
            pl.BlockSpec((BATCH_BLOCK, TOK_TILE, d), lambda b, t: (b, jnp.maximum(t - nct, 0), 0))]


def _pre_attn_kernel(*refs, **static):
    mean_mat = _group_mean_matrix()
    for i in range(BATCH_BLOCK):
        _pre_attn_sample(i, mean_mat, *refs, **static)


def _pre_attn_sample(i, mean_mat, *refs, n_streams, nct, k_width, rope, dup_kv):
    x_refs, refs = refs[:n_streams], refs[n_streams:]
    if rope:
        (mod_ref, ng_ref, w_ref, qg_ref, kg_ref, cos_ref, sin_ref, q_ref, k_ref, v_ref) = refs
    else:
        mod_ref, ng_ref, w_ref, qg_ref, kg_ref, q_ref, k_ref, v_ref = refs
    q_ref, k_ref, v_ref = q_ref.at[i], k_ref.at[i], v_ref.at[i]
    m = mod_ref[i, 0]
    h = _norm_mod(_stream_tile(x_refs, i, nct), ng_ref[...], m[0:1, :], m[1:2, :]).astype(BF16)
    d = D_MODEL
    q = _head_rms(_dot(h, w_ref[:, 0:d]), qg_ref[...], mean_mat)
    k = _head_rms(_dot(h, w_ref[:, d:d + k_width]), kg_ref[...], mean_mat)
    v = _dot(h, w_ref[:, d + k_width:d + 2 * k_width])
    if rope:
        q = _rope(q, cos_ref[...], sin_ref[...])
        k = _rope(k, cos_ref[...], sin_ref[...])
    for p in range(N_PAIRS):
        q_ref[p] = q[:, p * LANES:(p + 1) * LANES].astype(BF16)
    if dup_kv:
        for g in range(k_width // HEAD_DIM):
            kh = k[:, g * HEAD_DIM:(g + 1) * HEAD_DIM]
            k_ref[g] = jnp.concatenate([kh, kh], axis=-1).astype(BF16)
    else:
        for p in range(k_width // LANES):
            k_ref[p] = k[:, p * LANES:(p + 1) * LANES].astype(BF16)
    if dup_kv:
        for g in range(k_width // HEAD_DIM):
            vh = v[:, g * HEAD_DIM:(g + 1) * HEAD_DIM]
            v_ref[g] = jnp.concatenate([vh, vh], axis=-1).astype(BF16)
    else:
        for p in range(k_width // LANES):
            v_ref[p] = v[:, p * LANES:(p + 1) * LANES].astype(BF16)


def _mod_spec(nct, first=0):
    return pl.BlockSpec((BATCH_BLOCK, 1, N_MOD, D_MODEL),
                        lambda b, t: (b, jnp.where(t + first >= nct, 1, 0), 0, 0))


def _pre_attn(xs, modsel, norm_g, w_qkv, q_gain, k_gain, rope_tabs, *, k_width, dup_kv, nct):
    bsz, _, d = xs[0].shape
    n_tok = sum(a.shape[1] for a in xs)
    nt = n_tok // TOK_TILE
    n_w = w_qkv.shape[1]
    k_slots = k_width // HEAD_DIM if dup_kv else k_width // LANES
    rope = rope_tabs is not None
    in_specs = _stream_specs(xs, nct) + [
        _mod_spec(nct),
        _resident((1, d)),
        _resident((d, n_w)),
        _resident((1, d)),
        _resident((1, k_width))]
    args = list(xs) + [modsel, norm_g.reshape(1, d), w_qkv,
                       jnp.tile(q_gain * (ATTN_SCALE * LOG2E), d // HEAD_DIM).reshape(1, d),
                       jnp.tile(k_gain, k_width // HEAD_DIM).reshape(1, k_width)]
    if rope:
        in_specs += [pl.BlockSpec((TOK_TILE, LANES), lambda b, t: (t, 0))] * 2
        args += list(rope_tabs)
    qo = jax.ShapeDtypeStruct((bsz, N_PAIRS, n_tok, LANES), BF16)
    ko = jax.ShapeDtypeStruct((bsz, k_slots, n_tok, LANES), BF16)
    vo = jax.ShapeDtypeStruct((bsz, k_slots, n_tok, LANES), BF16)
    return pl.pallas_call(
        functools.partial(_pre_attn_kernel, n_streams=len(xs), nct=nct, k_width=k_width,
                          rope=rope, dup_kv=dup_kv),
        grid=(bsz // BATCH_BLOCK, nt),
        in_specs=in_specs,
        out_specs=[
            pl.BlockSpec((BATCH_BLOCK, N_PAIRS, TOK_TILE, LANES), lambda b, t: (b, 0, t, 0)),
            pl.BlockSpec((BATCH_BLOCK, k_slots, TOK_TILE, LANES), lambda b, t: (b, 0, t, 0)),
            pl.BlockSpec((BATCH_BLOCK, k_slots, TOK_TILE, LANES), lambda b, t: (b, 0, t, 0))],
        out_shape=[qo, ko, vo],
        compiler_params=_params(2),
        name="norm_mod_qkv",
    )(*args)


def _pre_lru_kernel(x_ref, mod_ref, ng_ref, w_ref, gate_ref, rec_ref):
    d = D_MODEL
    for i in range(BATCH_BLOCK):
        m = mod_ref[i, 0]
        h = _norm_mod(x_ref[i], ng_ref[...], m[0:1, :], m[1:2, :]).astype(BF16)
        gate_ref[i] = jax.nn.gelu(_dot(h, w_ref[:, 0:d])).astype(BF16)
        rec_ref[i] = _dot(h, w_ref[:, d:2 * d])


def _pre_lru(xs, modsel, norm_g, w_in, *, nct):
    bsz, n_tok, d = xs.shape
    nt = n_tok // TOK_TILE
    tok_spec = pl.BlockSpec((BATCH_BLOCK, TOK_TILE, d), lambda b, t: (b, t, 0))
    return pl.pallas_call(
        _pre_lru_kernel,
        grid=(bsz // BATCH_BLOCK, nt),
        in_specs=[tok_spec, _mod_spec(nct), _resident((1, d)), _resident((d, 2 * d))],
        out_specs=[tok_spec, tok_spec],
        out_shape=[jax.ShapeDtypeStruct((bsz, n_tok, d), BF16),
                   jax.ShapeDtypeStruct((bsz, n_tok, d), F32)],
        compiler_params=_params(2),
        name="norm_mod_lru_in",
    )(xs, modsel, norm_g.reshape(1, d), w_in)


def _split_pair(q2):
    low = lax.broadcasted_iota(jnp.int32, q2.shape, 1) < HEAD_DIM
    zero = jnp.zeros_like(q2)
    return jnp.where(low, q2, zero), jnp.where(low, zero, q2)


def _exp2_parts(parts):
    m = functools.reduce(jnp.maximum, [jnp.max(s, axis=-1, keepdims=True) for s in parts])
    return [jnp.exp2(s - m) for s in parts]


def _pipeline3(n_units, scores, softmax, values):
    assert n_units % 2 == 0 and n_units >= 4
    scores(0, 0)
    scores(1, 1)
    softmax(0, 0)
    for u in range(2, n_units, 2):
        scores(u, 0)
        softmax(u - 1, 1)
        values(u - 2, 0)
        scores(u + 1, 1)
        softmax(u, 0)
        values(u - 1, 1)
    softmax(n_units - 1, 1)
    values(n_units - 2, 0)
    values(n_units - 1, 1)


def _sub_layer_norm(o, gain, lam_init):
    o = o * lax.rsqrt(jnp.mean(o * o, axis=-1, keepdims=True) + NORM_EPS) * gain
    return (o * (1.0 - lam_init)).astype(BF16)


def _scores_stage(s_buf, m_buf, q2, key_parts, bias_parts):
    for half, qh in enumerate(_split_pair(q2)):
        m, col = None, 0
        for kp, bp in zip(key_parts, bias_parts):
            s = _dot_nt(qh, kp)
            if bp is not None:
                s = s + bp[half].astype(F32)
            s_buf[half, :, col:col + kp.shape[0]] = s
            part_max = jnp.max(s, axis=-1, keepdims=True)
            m = part_max if m is None else jnp.maximum(m, part_max)
            col += kp.shape[0]
        m_buf[half] = jnp.broadcast_to(m, m_buf.shape[1:])


def _softmax_stage(s_buf, m_buf, p_buf):
    for half in range(2):
        p_buf[half] = jnp.exp2(s_buf[half] - m_buf[half][:, 0:1]).astype(BF16)


def _with_ones(v):
    return jnp.concatenate([v, jnp.ones_like(v)], axis=-1)


def _values_stage(p_buf, value_parts):
    outs = []
    for half in range(2):
        acc, col = None, 0
        for vp in value_parts:
            part = _dot(p_buf[half, :, col:col + vp.shape[0]], _with_ones(vp))
            acc = part if acc is None else acc + part
            col += vp.shape[0]
        outs.append(acc[:, :LANES] / acc[:, LANES:])
    return outs


def _direct_pair(q2, k2, v2):
    outs = []
    for qh in _split_pair(q2):
        (e,) = _exp2_parts([_dot_nt(qh, k2)])
        acc = _dot(e.astype(BF16), _with_ones(v2))
        outs.append(acc[:, :LANES] / acc[:, LANES:])
    return outs


def _attn_scratch(n_cols):
    scratch = [pltpu.VMEM((2, TOK_TILE, n_cols), F32)] * 2
    scratch += [pltpu.VMEM((2, TOK_TILE, LANES), F32)] * 2
    scratch += [pltpu.VMEM((2, TOK_TILE, n_cols), BF16)] * 2
    return scratch


def _diff_attn_kernel(lam_ref, sg_ref, q_ref, k_ref, v_ref, o_ref, s0, s1, m0, m1, p0, p1,
                      *, nct, n_ctx, lam_init):
    t = pl.program_id(1)
    lv = lam_ref[...]
    lam = (jnp.exp(jnp.sum(lv[0:1, :] * lv[1:2, :], axis=-1, keepdims=True))
           - jnp.exp(jnp.sum(lv[2:3, :] * lv[3:4, :], axis=-1, keepdims=True)) + lam_init)

    def finish(h, outs):
        o_ref[0, h] = _sub_layer_norm(outs[0] - lam * outs[1], sg_ref[...], lam_init)

    @pl.when(t < nct)
    def _():
        for h in range(DA_HEADS):
            finish(h, _direct_pair(q_ref[0, h], k_ref[0, h, 0:n_ctx, :], v_ref[0, h, 0:n_ctx, :]))

    @pl.when(t >= nct)
    def _():
        s_bufs, m_bufs, p_bufs = (s0, s1), (m0, m1), (p0, p1)
        _pipeline3(
            DA_HEADS,
            lambda h, slot: _scores_stage(s_bufs[slot], m_bufs[slot], q_ref[0, h],
                                          [k_ref[0, h]], [None]),
            lambda h, slot: _softmax_stage(s_bufs[slot], m_bufs[slot], p_bufs[slot]),
            lambda h, slot: finish(h, _values_stage(p_bufs[slot], [v_ref[0, h]])))


def _diff_attention(q, k, v, lam_vecs, subln_g, *, nct, n_ctx, lam_init):
    bsz, _, n_tok, _ = q.shape
    nt = n_tok // TOK_TILE
    tile = pl.BlockSpec((1, N_PAIRS, TOK_TILE, LANES), lambda b, t: (b, 0, t, 0))
    whole_k = pl.BlockSpec((1, DA_HEADS, n_tok, LANES), lambda b, t: (b, 0, 0, 0))
    whole_v = pl.BlockSpec((1, DA_HEADS, n_tok, LANES), lambda b, t: (b, 0, 0, 0))
    scratch = _attn_scratch(n_tok)
    return pl.pallas_call(
        functools.partial(_diff_attn_kernel, nct=nct, n_ctx=n_ctx, lam_init=lam_init),
        grid=(bsz, nt),
        in_specs=[_resident((4, HEAD_DIM)), _resident((1, LANES)), tile, whole_k, whole_v],
        out_specs=tile,
        out_shape=jax.ShapeDtypeStruct(q.shape, BF16),
        scratch_shapes=scratch,
        compiler_params=_params(2),
        name="diff_attention",
    )(lam_vecs, subln_g.reshape(1, LANES), q, k, v)


def _pair_output(outs):
    low = lax.broadcasted_iota(jnp.int32, outs[0].shape, 1) < HEAD_DIM
    return jnp.where(low, outs[0], outs[1]).astype(BF16)


def _gqa_attn_kernel(q_ref, k_ref, v_ref, o_ref, s0, s1, m0, m1, p0, p1):
    s_bufs, m_bufs, p_bufs = (s0, s1), (m0, m1), (p0, p1)

    def scores(u, slot):
        i, p = divmod(u, N_PAIRS)
        _scores_stage(s_bufs[slot], m_bufs[slot], q_ref[i, p], [k_ref[i, p // 2]], [None])

    def values(u, slot):
        i, p = divmod(u, N_PAIRS)
        o_ref[i, p] = _pair_output(_values_stage(p_bufs[slot], [v_ref[i, p // 2]]))

    _pipeline3(
        GQA_BATCH_BLOCK * N_PAIRS, scores,
        lambda u, slot: _softmax_stage(s_bufs[slot], m_bufs[slot], p_bufs[slot]),
        values)


GQA_BATCH_BLOCK = 2


def _gqa_attention(q, k, v, *, nct):
    bsz, _, n_tok, _ = q.shape
    nt = n_tok // TOK_TILE - nct
    bb = GQA_BATCH_BLOCK
    q_tile = pl.BlockSpec((bb, N_PAIRS, TOK_TILE, LANES), lambda b, t: (b, 0, t + nct, 0))
    o_tile = pl.BlockSpec((bb, N_PAIRS, TOK_TILE, LANES), lambda b, t: (b, 0, t, 0))
    whole_k = pl.BlockSpec((bb, GQA_KV_HEADS, n_tok, LANES), lambda b, t: (b, 0, 0, 0))
    whole_v = pl.BlockSpec((bb, GQA_KV_HEADS, n_tok, LANES), lambda b, t: (b, 0, 0, 0))
    scratch = _attn_scratch(n_tok)
    return pl.pallas_call(
        _gqa_attn_kernel,
        grid=(bsz // bb, nt),
        in_specs=[q_tile, whole_k, whole_v],
        out_specs=o_tile,
        out_shape=jax.ShapeDtypeStruct((bsz, N_PAIRS, nt * TOK_TILE, LANES), BF16),
        scratch_shapes=scratch,
        compiler_params=_params(2),
        name="gqa_attention",
    )(q, k, v)


def _na_attn_kernel(bias_ref, q_ref, k_ref, v_ref, o_ref, s0, s1, m0, m1, p0, p1,
                    *, nct, n_ctx, n_rows):
    t = pl.program_id(1)

    @pl.when(t < nct)
    def _():
        for p in range(N_PAIRS):
            o_ref[0, p] = _pair_output(_direct_pair(
                q_ref[0, p], k_ref[0, p, 0:n_ctx, :], v_ref[0, p, 0:n_ctx, :]))

    @pl.when(t >= nct)
    def _():
        first_row = jnp.clip(NA_Q_ROWS * (t - nct) - NA_ROWS_MAX // 2, 0, n_rows - NA_K_ROWS)
        start = pl.multiple_of(n_ctx + first_row * GRID_W, GRID_W)
        s_bufs, m_bufs, p_bufs = (s0, s1), (m0, m1), (p0, p1)

        def joined(ref, p):
            return jnp.concatenate(
                [ref[0, p, pl.ds(0, n_ctx), :], ref[0, p, pl.ds(start, NA_BAND), :]], axis=0)

        def scores(p, slot):
            k2 = joined(k_ref, p)
            for half, qh in enumerate(_split_pair(q_ref[0, p])):
                s = _dot_nt(qh, k2)
                s = jnp.concatenate(
                    [s[:, :n_ctx], s[:, n_ctx:] + bias_ref[0, 2 * p + half].astype(F32)], axis=1)
                s_bufs[slot][half] = s
                m_bufs[slot][half] = jnp.broadcast_to(
                    jnp.max(s, axis=-1, keepdims=True), m_bufs[slot].shape[1:])

        def values(p, slot):
            o_ref[0, p] = _pair_output(_values_stage(p_bufs[slot], [joined(v_ref, p)]))

        _pipeline3(
            N_PAIRS, scores,
            lambda p, slot: _softmax_stage(s_bufs[slot], m_bufs[slot], p_bufs[slot]),
            values)


def _na_bias_table(rpb, n_rows):
    heads = rpb.shape[0]
    kr_rows = min(NA_ROWS_MAX, n_rows)
    variants = (0, 1, n_rows // NA_Q_ROWS - 1)
    n_tiles = len(variants)
    tt = jnp.array(variants)[:, None, None]
    qr = jnp.arange(NA_Q_ROWS)[None, :, None]
    kr = jnp.arange(NA_K_ROWS)[None, None, :]
    r = NA_Q_ROWS * tt + qr
    rp = jnp.clip(NA_Q_ROWS * tt - NA_ROWS_MAX // 2, 0, n_rows - NA_K_ROWS) + kr
    r0 = jnp.clip(r - kr_rows // 2, 0, n_rows - kr_rows)
    row_ok = (rp >= r0) & (rp < r0 + kr_rows)
    row_idx = jnp.clip(rp - r + NA_ROWS_MAX - 1, 0, 2 * NA_ROWS_MAX - 2)
    c = jnp.arange(GRID_W)[:, None]
    cp = jnp.arange(GRID_W)[None, :]
    cs = jnp.clip(c - NA_COLS // 2, 0, GRID_W - NA_COLS)
    col_ok = (cp >= cs) & (cp < cs + NA_COLS)
    col_idx = jnp.clip(cp - c, -(NA_COLS - 1), NA_COLS - 1) + NA_COLS - 1
    rows_sel = rpb[:, row_idx, :]
    onehot = (col_idx.reshape(-1)[None, :] == jnp.arange(2 * NA_COLS - 1)[:, None]).astype(F32)
    tab = jnp.dot(rows_sel.reshape(-1, 2 * NA_COLS - 1), onehot, precision=lax.Precision.HIGHEST)
    tab = tab.reshape(heads, n_tiles, NA_Q_ROWS, NA_K_ROWS, GRID_W, GRID_W)
    ok = row_ok[None, :, :, :, None, None] & col_ok[None, None, None, None, :, :]
    tab = jnp.where(ok, tab * LOG2E, MASK_VALUE)
    tab = tab.transpose(1, 0, 2, 4, 3, 5)
    return tab.reshape(n_tiles, heads, TOK_TILE, NA_BAND).astype(BF16)


def _na_attention(q, k, v, bias_tab, *, nct, n_ctx):
    bsz, _, n_tok, _ = q.shape
    nt = n_tok // TOK_TILE
    heads = bias_tab.shape[1]
    n_rows = (n_tok - n_ctx) // GRID_W
    tile = pl.BlockSpec((1, N_PAIRS, TOK_TILE, LANES), lambda b, t: (b, 0, t, 0))
    whole_k = pl.BlockSpec((1, N_PAIRS, n_tok, LANES), lambda b, t: (b, 0, 0, 0))
    whole_v = pl.BlockSpec((1, N_PAIRS, n_tok, LANES), lambda b, t: (b, 0, 0, 0))
    last = nt - nct - 1

    def bias_index(b, t):
        tile = t - nct
        return (jnp.where(tile <= 0, 0, jnp.where(tile == last, 2, 1)), 0, 0, 0)

    bias_spec = pl.BlockSpec((1, heads, TOK_TILE, NA_BAND), bias_index)
    scratch = _attn_scratch(n_ctx + NA_BAND)
    return pl.pallas_call(
        functools.partial(_na_attn_kernel, nct=nct, n_ctx=n_ctx, n_rows=n_rows),
        grid=(bsz, nt),
        in_specs=[bias_spec, tile, whole_k, whole_v],
        out_specs=tile,
        out_shape=jax.ShapeDtypeStruct(q.shape, BF16),
        scratch_shapes=scratch,
        compiler_params=_params(2),
        name="neighbourhood_attention",
    )(bias_tab, q, k, v)


LRU_CHUNK = 256
LRU_PAD = 8


def _softplus(z):
    return jnp.maximum(z, 0.0) + jnp.log1p(jnp.exp(-jnp.abs(z)))


def _sqrt_nonneg(z):
    return z * lax.rsqrt(jnp.maximum(z, float(jnp.finfo(F32).tiny)))


def _block_scan(a_ref, b_ref, slab, r0, reverse):
    n_blk = LRU_CHUNK // SUBLANES
    order = range(SUBLANES - 1, -1, -1) if reverse else range(SUBLANES)
    acc_a = acc_b = None
    for r in order:
        rows = pl.ds(r0 + r, n_blk, stride=SUBLANES)
        a_r = a_ref[slab, rows, :]
        b_r = b_ref[slab, rows, :]
        if acc_a is None:
            acc_a, acc_b = a_r, b_r
            continue
        acc_b = a_r * acc_b + b_r
        acc_a = a_r * acc_a
        a_ref[slab, rows, :] = acc_a
        b_ref[slab, rows, :] = acc_b


def _lru_kernel(rec_ref, gate_ref, cw_ref, cb_ref,
                fwa_ref, fwx_ref, bwa_ref, bwx_ref,
                fba_ref, fbx_ref, bba_ref, bbx_ref, flam_ref, blam_ref,
                o_ref, xp_ref, af_ref, bf_ref, ab_ref, bb_ref, *, n_ctx, n_tok):
    width = LRU_BLOCK
    xp_ref[0:LRU_PAD, :] = jnp.zeros((LRU_PAD, width), F32)
    xp_ref[LRU_PAD + n_tok:2 * LRU_PAD + n_tok, :] = jnp.zeros((LRU_PAD, width), F32)
    xp_ref[LRU_PAD:LRU_PAD + n_tok, :] = rec_ref[0]

    directions = (
        (fwa_ref, fwx_ref, fba_ref, fbx_ref, (-0.5 * LRU_C) * _softplus(-flam_ref[...]),
         af_ref, bf_ref, False),
        (bwa_ref, bwx_ref, bba_ref, bbx_ref, (-0.5 * LRU_C) * _softplus(-blam_ref[...]),
         ab_ref, bb_ref, True),
    )
    n_slabs = width // LANES
    wrow = lax.broadcasted_iota(jnp.int32, (LRU_CHUNK + 2 * LRU_PAD, 1), 0)

    for r0 in range(0, n_tok, LRU_CHUNK):
        xw = xp_ref[r0:r0 + LRU_CHUNK + 2 * LRU_PAD, :]
        if r0 == n_ctx:
            xw = jnp.where(wrow < LRU_PAD, 0.0, xw)
        if r0 + LRU_CHUNK == n_ctx:
            xw = jnp.where(wrow >= LRU_PAD + LRU_CHUNK, 0.0, xw)
        u = cb_ref[...]
        for j in range(LRU_CONV):
            off = LRU_PAD + j - LRU_CONV // 2
            u = u + xw[off:off + LRU_CHUNK, :] * cw_ref[j:j + 1, :]
        ub = u.astype(BF16)
        for wa_ref, wx_ref, ba_ref, bx_ref, decay, a_ref, b_ref, reverse in directions:
            log_a = decay * jnp.tanh(_dot(ub, wa_ref[0]) + ba_ref[...]) + decay
            i = 0.5 * jnp.tanh(_dot(ub, wx_ref[0]) + bx_ref[...]) + 0.5
            a = jnp.exp(log_a)
            b = _sqrt_nonneg(-jnp.tanh(log_a) * (a * a + 1.0)) * (i * u)
            for s in range(n_slabs):
                a_ref[s, r0:r0 + LRU_CHUNK, :] = a[:, s * LANES:(s + 1) * LANES]
                b_ref[s, r0:r0 + LRU_CHUNK, :] = b[:, s * LANES:(s + 1) * LANES]
                _block_scan(a_ref, b_ref, s, r0, reverse)

    n_all = n_tok // SUBLANES
    n_cb = n_ctx // SUBLANES

    def step(i, carry, rev_end):
        cf, cr = carry
        rf = pl.multiple_of(i * SUBLANES, SUBLANES)
        rb = pl.multiple_of((rev_end - i) * SUBLANES, SUBLANES)
        a_f, b_f = af_ref[:, pl.ds(rf, SUBLANES), :], bf_ref[:, pl.ds(rf, SUBLANES), :]
        a_r, b_r = ab_ref[:, pl.ds(rb, SUBLANES), :], bb_ref[:, pl.ds(rb, SUBLANES), :]
        bf_ref[:, pl.ds(rf, SUBLANES), :] = a_f * cf + b_f
        bb_ref[:, pl.ds(rb, SUBLANES), :] = a_r * cr + b_r

        def edge(x, row):
            return jnp.broadcast_to(x[:, row:row + 1, :], x.shape)
        last = SUBLANES - 1
        return (edge(a_f, last) * cf + edge(b_f, last), edge(a_r, 0) * cr + edge(b_r, 0))

    zero = jnp.zeros((n_slabs, SUBLANES, LANES), F32)
    carry = lax.fori_loop(0, n_cb, functools.partial(step, rev_end=n_cb - 1), (zero, zero),
                          unroll=2)
    lax.fori_loop(n_cb, n_all, functools.partial(step, rev_end=n_all - 1 + n_cb), carry, unroll=2)

    gate = gate_ref[0]
    for s in range(n_slabs):
        y = (bf_ref[s] + bb_ref[s]) * gate[:, s * LANES:(s + 1) * LANES].astype(F32)
        o_ref[0, s] = y.astype(BF16)


def _lru_mixer(rec, gate, p, *, n_ctx):
    (conv_w, conv_b, f_wa, f_ba, f_wx, f_bx, f_lam, b_wa, b_ba, b_wx, b_bx, b_lam) = p
    bsz, n_tok, d = rec.shape
    n_blocks = d // LRU_BLOCK
    seq = pl.BlockSpec((1, n_tok, LRU_BLOCK), lambda b, n: (b, 0, n))
    vec = pl.BlockSpec((1, LRU_BLOCK), lambda b, n: (0, n))
    mat = pl.BlockSpec((1, LRU_BLOCK, LRU_BLOCK), lambda b, n: (n, 0, 0))
    row = lambda a: a.reshape(1, d)
    half = lambda a: 0.5 * a
    assert n_ctx % LRU_CHUNK == 0 and n_tok % LRU_CHUNK == 0
    scratch = [pltpu.VMEM((n_tok + 2 * LRU_PAD, LRU_BLOCK), F32)]
    scratch += [pltpu.VMEM((LRU_BLOCK // LANES, n_tok, LANES), F32)] * 4
    return pl.pallas_call(
        functools.partial(_lru_kernel, n_ctx=n_ctx, n_tok=n_tok),
        grid=(bsz, n_blocks),
        in_specs=[seq, seq, pl.BlockSpec((LRU_CONV, LRU_BLOCK), lambda b, n: (0, n)), vec,
                  mat, mat, mat, mat, vec, vec, vec, vec, vec, vec],
        out_specs=pl.BlockSpec((1, LRU_BLOCK // LANES, n_tok, LANES), lambda b, n: (b, n, 0, 0)),
        out_shape=jax.ShapeDtypeStruct((bsz, d // LANES, n_tok, LANES), BF16),
        scratch_shapes=scratch,
        compiler_params=_params(2),
        name="rglru_scan",
    )(rec, gate, conv_w, row(conv_b),
      half(f_wa).astype(BF16), half(f_wx).astype(BF16),
      half(b_wa).astype(BF16), half(b_wx).astype(BF16),
      row(half(f_ba)), row(half(f_bx)), row(half(b_ba)), row(half(b_bx)), row(f_lam), row(b_lam))


FF_CHUNK = 1024


def _post_kernel(o_ref, *refs, n_streams, nct):
    x_refs = refs[:n_streams]
    mod_ref, ng_ref, wo_ref, w1_ref, w2_ref, out_ref = refs[n_streams:]
    for i in range(BATCH_BLOCK):
        m = mod_ref[i, 0]
        o = jnp.concatenate([o_ref[i, p] for p in range(N_PAIRS)], axis=-1)
        x1 = _stream_tile(x_refs, i, nct) + m[2:3, :] * _dot(o, wo_ref[...])
        h = _norm_mod(x1, ng_ref[...], m[3:4, :], m[4:5, :]).astype(BF16)
        acc = jnp.zeros_like(x1)
        for c in range(0, D_FF, FF_CHUNK):
            a = jnp.maximum(_dot(h, w1_ref[:, c:c + FF_CHUNK]), 0.0)
            acc = acc + _dot((a * a).astype(BF16), w2_ref[c:c + FF_CHUNK, :])
        out_ref[i] = x1 + m[5:6, :] * acc


def _post(o, xs, modsel, norm_g, w_o, w1, w2, *, nct, skip_ctx):
    bsz, _, d = xs[0].shape
    n_tok = sum(a.shape[1] for a in xs)
    first = nct if skip_ctx else 0
    nt = n_tok // TOK_TILE - first
    assert o.shape[2] == nt * TOK_TILE
    if len(xs) == 1:
        x_specs = [pl.BlockSpec((BATCH_BLOCK, TOK_TILE, d), lambda b, t: (b, t + first, 0))]
    else:
        assert not skip_ctx
        x_specs = _stream_specs(xs, nct)
    return pl.pallas_call(
        functools.partial(_post_kernel, n_streams=len(xs), nct=nct),
        grid=(bsz // BATCH_BLOCK, nt),
        in_specs=[pl.BlockSpec((BATCH_BLOCK, N_PAIRS, TOK_TILE, LANES), lambda b, t: (b, 0, t, 0))]
        + x_specs
        + [_mod_spec(nct, first),
           _resident((1, d)), _resident((d, d)), _resident((d, D_FF)), _resident((D_FF, d))],
        out_specs=pl.BlockSpec((BATCH_BLOCK, TOK_TILE, d), lambda b, t: (b, t, 0)),
        out_shape=jax.ShapeDtypeStruct((bsz, nt * TOK_TILE, d), F32),
        compiler_params=_params(2),
        name="proj_residual_mlp",
    )(o, *xs, modsel, norm_g.reshape(1, d), w_o, w1, w2)


def _rope_tables(n_ctx, n_lat):
    t = jnp.arange(n_lat)
    row = (t // GRID_W).astype(F32)
    col = (t % GRID_W).astype(F32)
    n_freq = HEAD_DIM // 4
    inv = ROPE_THETA ** (-jnp.arange(n_freq, dtype=F32) / n_freq)
    ang = jnp.concatenate([row[:, None] * inv, col[:, None] * inv], axis=-1)
    cos = jnp.repeat(jnp.cos(ang), 2, axis=-1)
    sin = jnp.repeat(jnp.sin(ang), 2, axis=-1) * jnp.tile(jnp.array([-1.0, 1.0], F32), HEAD_DIM // 2)
    cos = jnp.concatenate([jnp.ones((n_ctx, HEAD_DIM), F32), cos], axis=0)
    sin = jnp.concatenate([jnp.zeros((n_ctx, HEAD_DIM), F32), sin], axis=0)
    return jnp.tile(cos, (1, LANES // HEAD_DIM)), jnp.tile(sin, (1, LANES // HEAD_DIM))


def kernel(x, c, ctx, c_ctx, norm1_g, norm2_g, w_mod, b_mod, w_mlp1, w_mlp2, a_w_qkv, a_q_norm_g, a_k_norm_g, a_lambda_q1, a_lambda_k1, a_lambda_q2, a_lambda_k2, a_subln_g, a_w_o, b_w_qkv, b_q_norm_g, b_k_norm_g, b_rpb, b_w_o, c_w_in, c_conv_w, c_conv_b, c_fwd_w_a, c_fwd_b_a, c_fwd_w_x, c_fwd_b_x, c_fwd_lam, c_bwd_w_a, c_bwd_b_a, c_bwd_w_x, c_bwd_b_x, c_bwd_lam, c_w_o, d_w_qkv, d_q_norm_g, d_k_norm_g, d_w_o):
    bsz, n_lat, d = x.shape
    n_ctx = ctx.shape[1]
    depth = w_mod.shape[0]
    assert d == D_MODEL and depth == 4 and n_ctx % TOK_TILE == 0 and n_lat % TOK_TILE == 0
    assert bsz % BATCH_BLOCK == 0
    assert (n_lat // GRID_W) % NA_Q_ROWS == 0 and n_lat // GRID_W >= NA_K_ROWS
    nct = n_ctx // TOK_TILE

    rows = -(-(bsz + 1) // SUBLANES) * SUBLANES
    c_all = jnp.concatenate([c, c_ctx[None, :], jnp.zeros((rows - bsz - 1, d), F32)], axis=0)
    mods = _modulation(c_all, w_mod, b_mod).reshape(depth, rows, N_MOD, d)
    mod_ctx = jnp.broadcast_to(mods[:, bsz][:, None], (depth, bsz, N_MOD, d))
    modsel = jnp.stack([mod_ctx, mods[:, :bsz]], axis=2)

    rope_tabs = _rope_tables(n_ctx, n_lat)
    w1 = w_mlp1.astype(BF16)
    w2 = w_mlp2.astype(BF16)

    xs = (ctx, x)
    q, k, v = _pre_attn(xs, modsel[0], norm1_g[0], a_w_qkv[0].astype(BF16), a_q_norm_g[0],
                        a_k_norm_g[0], rope_tabs, k_width=d, dup_kv=False, nct=nct)
    lam_vecs = jnp.stack([a_lambda_q1[0], a_lambda_k1[0], a_lambda_q2[0], a_lambda_k2[0]])
    lam_init = 0.8 - 0.6 * math.exp(-0.3 * 0)
    o = _diff_attention(q, k, v, lam_vecs, a_subln_g[0], nct=nct, n_ctx=n_ctx, lam_init=lam_init)
    xs = (_post(o, xs, modsel[0], norm2_g[0], a_w_o[0].astype(BF16), w1[0], w2[0],
                nct=nct, skip_ctx=False),)

    q, k, v = _pre_attn(xs, modsel[1], norm1_g[1], b_w_qkv[0].astype(BF16), b_q_norm_g[0],
                        b_k_norm_g[0], None, k_width=d, dup_kv=False, nct=nct)
    bias_tab = _na_bias_table(b_rpb[0], n_lat // GRID_W)
    o = _na_attention(q, k, v, bias_tab, nct=nct, n_ctx=n_ctx)
    xs = (_post(o, xs, modsel[1], norm2_g[1], b_w_o[0].astype(BF16), w1[1], w2[1],
                nct=nct, skip_ctx=False),)

    gate, rec = _pre_lru(xs[0], modsel[2], norm1_g[2], c_w_in[0].astype(BF16), nct=nct)
    lru_p = (c_conv_w[0], c_conv_b[0], c_fwd_w_a[0], c_fwd_b_a[0], c_fwd_w_x[0], c_fwd_b_x[0],
             c_fwd_lam[0], c_bwd_w_a[0], c_bwd_b_a[0], c_bwd_w_x[0], c_bwd_b_x[0], c_bwd_lam[0])
    o = _lru_mixer(rec, gate, lru_p, n_ctx=n_ctx)
    xs = (_post(o, xs, modsel[2], norm2_g[2], c_w_o[0].astype(BF16), w1[2], w2[2],
                nct=nct, skip_ctx=False),)

    q, k, v = _pre_attn(xs, modsel[3], norm1_g[3], d_w_qkv[0].astype(BF16), d_q_norm_g[0],
                        d_k_norm_g[0], rope_tabs, k_width=GQA_KV_HEADS * HEAD_DIM, dup_kv=True,
                        nct=nct)
    o = _gqa_attention(q, k, v, nct=nct)
    return _post(o, xs, modsel[3], norm2_g[3], d_w_o[0].astype(BF16), w1[3], w2[3],
                 nct=nct, skip_ctx=True)
```
